```python
import math
import jax
import jax.numpy as jnp
from jax import lax
import numpy as np

D_MODEL = 2048
BATCH = 16
SEQ = 256
DEPTH = 2
DEC_BATCH = 4
DEC_SEQ = 2048
PAST_LEN = 512

GRID_W = 64
EPS = 1e-6
Q_BLOCK = 128
N_BRANCH = 4
BRANCH_W = D_MODEL // 2

MLA_HEADS = 8
MLA_NOPE = 128
MLA_ROPE = 64
MLA_V = BRANCH_W // MLA_HEADS
MLA_Q_RANK = D_MODEL // 4
MLA_KV_RANK = D_MODEL // 8
MLA_SCALE = (MLA_NOPE + MLA_ROPE) ** -0.5
ROPE_THETA = 10000.0

DN_HEADS = 8
DN_DK = 128
DN_DV = BRANCH_W // DN_HEADS
DN_CONV = 5
DN_CHUNK = 64
DN_CONV_CH = 2 * DN_HEADS * DN_DK + DN_HEADS * DN_DV

HY_WIDTH = BRANCH_W
HY_ORDER = 2
HY_SHORT = 3
HY_EMB = 33
HY_HID = 64
HY_FAST_DECAY = 0.3
HY_SLOW_DECAY = 1.5
HY_DECAY_TARGET = 1e-2

NA_HEADS = 8
NA_HD = BRANCH_W // NA_HEADS
NA_KH = 8
NA_KW = 16
NA_SCALE = NA_HD ** -0.5

PEER_HEADS = 8
PEER_NKEYS = 128
PEER_N = PEER_NKEYS * PEER_NKEYS
PEER_DKEY = 256
PEER_TOPK = 16
PEER_BLOCK = 128

IN_SIZES = (MLA_Q_RANK, MLA_KV_RANK, MLA_ROPE, DN_CONV_CH, DN_HEADS * DN_DV, 2 * DN_HEADS, 2 * DN_HEADS,
            (HY_ORDER + 1) * HY_WIDTH, 3 * NA_HEADS * NA_HD, N_BRANCH * D_MODEL)
IN_COLS = sum(IN_SIZES)

kernel_name = "hybrid_flow_trunk_step"

F32 = jnp.float32


def rms_norm(x, g):
    xf = x.astype(F32)
    y = xf * lax.rsqrt(jnp.mean(xf * xf, axis=-1, keepdims=True) + EPS)
    return (y * g.astype(F32)).astype(x.dtype)


def l2_norm(x):
    xf = x.astype(F32)
    return xf * lax.rsqrt(jnp.sum(xf * xf, axis=-1, keepdims=True) + EPS)


def split_in(p):
    cuts, acc = [], 0
    for s in IN_SIZES[:-1]:
        acc += s
        cuts.append(acc)
    return jnp.split(p, cuts, axis=-1)


def ada_modulation(cond, lp):
    m = jax.nn.silu(cond) @ lp["w_ada"] + lp["b_ada"]
    return jnp.split(m, 6, axis=-1)


def modulate(x, g, shift, scale):
    return rms_norm(x, g) * (1 + scale) + shift


def depthwise_conv(x, w):
    k, ch = w.shape
    pad = k // 2
    return lax.conv_general_dilated(x, w[:, None, :].astype(x.dtype), (1,), [(pad, pad)],
                                    dimension_numbers=("NWC", "WIO", "NWC"), feature_group_count=ch)


def block_attention(q, k, v, scale):
    B, Lq, H, dq = q.shape
    dv = v.shape[-1]
    nb = Lq // Q_BLOCK
    qb = q.reshape(B, nb, Q_BLOCK, H, dq).transpose(1, 0, 2, 3, 4)

    def one(qi):
        s = jnp.einsum("bqhd,bkhd->bhqk", qi, k, preferred_element_type=F32) * scale
        p = jax.nn.softmax(s, axis=-1).astype(v.dtype)
        return jnp.einsum("bhqk,bkhd->bqhd", p, v)

    o = lax.map(one, qb)
    return o.transpose(1, 0, 2, 3, 4).reshape(B, Lq, H, dv)


def apply_axial_rope(x):
    L = x.shape[1]
    half = x.shape[-1] // 2
    t = jnp.arange(L)
    inv = ROPE_THETA ** (-jnp.arange(0, half, 2, dtype=F32) / half)
    xf = x.astype(F32)
    out = []
    for pos, xa in ((t // GRID_W, xf[..., :half]), (t % GRID_W, xf[..., half:])):
        ang = pos.astype(F32)[:, None] * inv[None, :]
        cos = jnp.cos(ang)[:, None, :]
        sin = jnp.sin(ang)[:, None, :]
        x1, x2 = xa[..., : half // 2], xa[..., half // 2:]
        out += [x1 * cos - x2 * sin, x2 * cos + x1 * sin]
    return jnp.concatenate(out, axis=-1).astype(x.dtype)


def mla_queries(cq, lp, rotate):
    B, L, _ = cq.shape
    q = (rms_norm(cq, lp["mla_q_norm"]) @ lp["mla_w_qb"]).reshape(B, L, MLA_HEADS, MLA_NOPE + MLA_ROPE)
    if rotate:
        q = jnp.concatenate([q[..., :MLA_NOPE], apply_axial_rope(q[..., MLA_NOPE:])], axis=-1)
    return q


def mla_keys_values(ckv_n, krope, lp):
    B, L, _ = ckv_n.shape
    kv = (ckv_n @ lp["mla_w_kvb"]).reshape(B, L, MLA_HEADS, MLA_NOPE + MLA_V)
    k = jnp.concatenate([kv[..., :MLA_NOPE],
                         jnp.broadcast_to(krope[:, :, None, :], (B, L, MLA_HEADS, MLA_ROPE))], axis=-1)
    return k, kv[..., MLA_NOPE:]


def chunk_gated_delta(q, k, v, g, beta, s0):
    B, H, L, DK = q.shape
    DV = v.shape[-1]
    C = DN_CHUNK
    N = L // C
    q = q.reshape(B, H, N, C, DK)
    k = k.reshape(B, H, N, C, DK)
    v = v.reshape(B, H, N, C, DV)
    beta = beta.reshape(B, H, N, C)
    gc = jnp.cumsum(g.reshape(B, H, N, C), axis=-1)
    tri_incl = jnp.tril(jnp.ones((C, C), bool))
    tri_strict = jnp.tril(jnp.ones((C, C), bool), -1)
    decay = jnp.exp(jnp.where(tri_incl, gc[..., :, None] - gc[..., None, :], -jnp.inf))
    kk = jnp.einsum("bhncd,bhnsd->bhncs", k, k)
    a_low = jnp.where(tri_strict, beta[..., None] * kk * decay, 0.0)
    rhs = jnp.concatenate([v * beta[..., None], k * (beta * jnp.exp(gc))[..., None]], axis=-1)
    sol = lax.linalg.triangular_solve(jnp.eye(C, dtype=F32) + a_low, rhs, left_side=True, lower=True,
                                      unit_diagonal=True)
    u, w = sol[..., :DV], sol[..., DV:]
    qk = jnp.where(tri_incl, jnp.einsum("bhncd,bhnsd->bhncs", q, k) * decay, 0.0)

    def step(S, inp):
        qi, ki, ui, wi, gi, qki = inp
        v_new = ui - jnp.einsum("bhck,bhkv->bhcv", wi, S)
        o = (jnp.einsum("bhck,bhkv->bhcv", qi * jnp.exp(gi)[..., None], S)
             + jnp.einsum("bhcs,bhsv->bhcv", qki, v_new))
        g_last = gi[..., -1]
        S = (S * jnp.exp(g_last)[..., None, None]
             + jnp.einsum("bhck,bhcv->bhkv", ki * jnp.exp(g_last[..., None] - gi)[..., None], v_new))
        return S, o

    xs = tuple(jnp.moveaxis(t, 2, 0) for t in (q, k, u, w, gc, qk))
    S, o = lax.scan(step, s0, xs)
    return jnp.moveaxis(o, 0, 2).reshape(B, H, L, DV), S


def gated_deltanet(qkv_raw, z, a, b, lp, s_f0, s_b0):
    B, L, _ = qkv_raw.shape
    qkv = jax.nn.silu(depthwise_conv(qkv_raw, lp["dn_conv"]))
    q, k, v = jnp.split(qkv, [DN_HEADS * DN_DK, 2 * DN_HEADS * DN_DK], axis=-1)
    q = (l2_norm(q.reshape(B, L, DN_HEADS, DN_DK)) * DN_DK ** -0.5).transpose(0, 2, 1, 3)
    k = l2_norm(k.reshape(B, L, DN_HEADS, DN_DK)).transpose(0, 2, 1, 3)
    v = v.reshape(B, L, DN_HEADS, DN_DV).astype(F32).transpose(0, 2, 1, 3)
    a = a.reshape(B, L, 2, DN_HEADS).astype(F32)
    g = -jnp.exp(lp["dn_a_log"].astype(F32)) * jax.nn.softplus(a + lp["dn_dt_bias"].astype(F32))
    beta = jax.nn.sigmoid(b.reshape(B, L, 2, DN_HEADS).astype(F32))
    g = g.transpose(2, 0, 3, 1)
    beta = beta.transpose(2, 0, 3, 1)
    o_f, s_f = chunk_gated_delta(q, k, v, g[0], beta[0], s_f0.astype(F32))
    fl = lambda t: jnp.flip(t, axis=2)
    o_b, s_b = chunk_gated_delta(fl(q), fl(k), fl(v), fl(g[1]), fl(beta[1]), s_b0.astype(F32))
    o = (o_f + fl(o_b)).transpose(0, 2, 1, 3)
    o = rms_norm(o, lp["dn_out_norm"]) * jax.nn.silu(z.reshape(B, L, DN_HEADS, DN_DV).astype(F32))
    return o.reshape(B, L, BRANCH_W).astype(qkv_raw.dtype), s_f, s_b


def hyena_filters(L, lp):
    t01 = jnp.linspace(0.0, 1.0, L, dtype=F32)[:, None]
    bands = (HY_EMB - 1) // 2
    w = 2.0 * math.pi * jnp.arange(L, dtype=F32)[:, None] / L
    f = jnp.linspace(1e-4, bands - 1, bands, dtype=F32)[None, :]
    z = jnp.concatenate([t01, jnp.cos(f * w), -jnp.sin(f * w)], axis=-1)
    h = jnp.sin(z @ lp["hy_w1"].astype(F32) + lp["hy_b1"].astype(F32))
    h = jnp.sin(h @ lp["hy_w2"].astype(F32) + lp["hy_b2"].astype(F32))
    h = (h @ lp["hy_w3"].astype(F32)).reshape(L, HY_ORDER, 2, HY_WIDTH)
    max_decay = math.log(HY_DECAY_TARGET) / HY_FAST_DECAY
    min_decay = math.log(HY_DECAY_TARGET) / HY_SLOW_DECAY
    deltas = jnp.linspace(min_decay, max_decay, HY_WIDTH, dtype=F32)
    window = jnp.exp(-t01 * jnp.abs(deltas)[None, :])
    return h * window[:, None, None, :]


def long_conv(u, hf, hb, bias):
    L = u.shape[1]
    kern = jnp.concatenate([hf, jnp.zeros_like(hf[:1]), jnp.flip(hb[1:], axis=0)], axis=0)
    uf = jnp.fft.rfft(u, n=2 * L, axis=1)
    kf = jnp.fft.rfft(kern, axis=0)
    y = jnp.fft.irfft(uf * kf[None], n=2 * L, axis=1)[:, :L]
    return y + u * bias


def hyena(hy_raw, lp):
    L = hy_raw.shape[1]
    zc = depthwise_conv(hy_raw, lp["hy_conv"]).astype(F32)
    v, x1, x2 = jnp.split(zc, 3, axis=-1)
    filt = hyena_filters(L, lp)
    bias = lp["hy_bias"].astype(F32)
    y = v
    for o, gate in enumerate((x1, x2)):
        y = gate * long_conv(y, filt[:, o, 0], filt[:, o, 1], bias[o])
    return y.astype(hy_raw.dtype)


def na_split(na_in):
    B, L, _ = na_in.shape
    q, k, v = jnp.split(na_in, 3, axis=-1)
    return tuple(t.reshape(B, L, NA_HEADS, NA_HD) for t in (q, k, v))


def na_latent(q, k, v, k_ctx, v_ctx, rpb):
    B, L, H, d = q.shape
    rows = L // GRID_W
    kh = min(NA_KH, rows)
    kw = NA_KW
    nk = kh * kw
    cols = jnp.arange(GRID_W)
    col_idx = jnp.clip(cols - kw // 2, 0, GRID_W - kw)[:, None] + jnp.arange(kw)[None, :]
    dcol = col_idx - cols[:, None] + (NA_KW - 1)
    q_rows = q.reshape(B, rows, GRID_W, H, d).transpose(1, 0, 2, 3, 4)

    def one_row(args):
        r, q_row = args
        key_rows = jnp.clip(r - kh // 2, 0, rows - kh) + jnp.arange(kh)
        idx = (key_rows[None, :, None] * GRID_W + col_idx[:, None, :]).reshape(GRID_W, nk)
        k_win = k[:, idx]
        v_win = v[:, idx]
        drow = key_rows - r + (NA_KH - 1)
        bias = rpb[:, drow[None, :, None], dcol[:, None, :]].reshape(H, GRID_W, nk).astype(F32)
        s_win = jnp.einsum("bqhd,bqkhd->bhqk", q_row, k_win, preferred_element_type=F32) * NA_SCALE + bias[None]
        s_ctx = jnp.einsum("bqhd,bkhd->bhqk", q_row, k_ctx, preferred_element_type=F32) * NA_SCALE
        p = jax.nn.softmax(jnp.concatenate([s_win, s_ctx], axis=-1), axis=-1).astype(v.dtype)
        return (jnp.einsum("bhqk,bqkhd->bqhd", p[..., :nk], v_win)
                + jnp.einsum("bhqk,bkhd->bqhd", p[..., nk:], v_ctx))

    o = lax.map(one_row, (jnp.arange(rows), q_rows))
    return o.transpose(1, 0, 2, 3, 4).reshape(B, L, H * d)


def merge_branches(branches, gate_logits, lp):
    B, L, _ = gate_logits.shape
    br = jnp.stack(branches, axis=2)
    proj = jnp.einsum("blnw,nwd->blnd", br, lp["w_branch"])
    g = jax.nn.sigmoid(gate_logits.reshape(B, L, N_BRANCH, D_MODEL))
    return jnp.sum(g * proj, axis=2) @ lp["w_out"]


def peer(h, lp):
    B, L, D = h.shape
    T = B * L
    xt = h.reshape(T, D)
    q = (xt @ lp["peer_wq"]).reshape(T, PEER_HEADS, 2, PEER_DKEY // 2)
    s = jnp.einsum("thsc,snc->thsn", q, lp["peer_keys"], preferred_element_type=F32)
    top_s, top_i = lax.top_k(s, PEER_TOPK)
    cand_s = (top_s[:, :, 0, :, None] + top_s[:, :, 1, None, :]).reshape(T, PEER_HEADS, PEER_TOPK * PEER_TOPK)
    cand_i = (top_i[:, :, 0, :, None] * PEER_NKEYS + top_i[:, :, 1, None, :]).reshape(T, PEER_HEADS, PEER_TOPK * PEER_TOPK)
    best_s, best_j = lax.top_k(cand_s, PEER_TOPK)
    experts = jnp.take_along_axis(cand_i, best_j, axis=-1)
    gates = jax.nn.softmax(best_s, axis=-1)
    nb = T // PEER_BLOCK
    ne = PEER_HEADS * PEER_TOPK

    def block(args):
        xb, eb, gb = args
        act = jax.nn.gelu(jnp.einsum("td,ted->te", xb, lp["peer_u"][eb], preferred_element_type=F32),
                          approximate=False)
        return jnp.einsum("te,ted->td", (gb * act).astype(xb.dtype), lp["peer_v"][eb])

    y = lax.map(block, (xt.reshape(nb, PEER_BLOCK, D), experts.reshape(nb, PEER_BLOCK, ne),
                        gates.reshape(nb, PEER_BLOCK, ne)))
    return y.reshape(B, L, D)


def context_layer(x, lp, c_ctx):
    B, L, _ = x.shape
    sh1, sc1, g1, sh2, sc2, g2 = ada_modulation(c_ctx[None, None, :], lp)
    h = modulate(x, lp["norm1_g"], sh1, sc1)
    cq, ckv, krope, dn_qkv, dn_z, dn_a, dn_b, hy_in, na_in, gate_logits = split_in(h @ lp["w_in"])
    ckv_n = rms_norm(ckv, lp["mla_kv_norm"])
    k, v = mla_keys_values(ckv_n, krope, lp)
    o_a = block_attention(mla_queries(cq, lp, rotate=False), k, v, MLA_SCALE).reshape(B, L, BRANCH_W)
    s0 = jnp.zeros((B, DN_HEADS, DN_DK, DN_DV), F32)
    o_b, s_f, s_b = gated_deltanet(dn_qkv, dn_z, dn_a, dn_b, lp, s0, s0)
    o_c = hyena(hy_in, lp)
    qn, kn, vn = na_split(na_in)
    o_d = block_attention(qn, kn, vn, NA_SCALE).reshape(B, L, BRANCH_W)
    x = x + g1 * merge_branches((o_a, o_b, o_c, o_d), gate_logits, lp)
    x = x + g2 * peer(modulate(x, lp["norm2_g"], sh2, sc2), lp)
    return x, ckv_n, krope, kn, vn, s_f.astype(x.dtype), s_b.astype(x.dtype)


def latent_layer(x, lp, c, ckv_ctx, krope_ctx, na_k_ctx, na_v_ctx, s_f0, s_b0):
    B, L, _ = x.shape
    sh1, sc1, g1, sh2, sc2, g2 = ada_modulation(c[:, None, :], lp)
    h = modulate(x, lp["norm1_g"], sh1, sc1)
    cq, ckv, krope, dn_qkv, dn_z, dn_a, dn_b, hy_in, na_in, gate_logits = split_in(h @ lp["w_in"])
    q = mla_queries(cq, lp, rotate=True)
    k_lat, v_lat = mla_keys_values(rms_norm(ckv, lp["mla_kv_norm"]),
                                   apply_axial_rope(krope[:, :, None, :])[:, :, 0, :], lp)
    k_ctx, v_ctx = mla_keys_values(ckv_ctx, krope_ctx, lp)
    o_a = block_attention(q, jnp.concatenate([k_lat, k_ctx], axis=1), jnp.concatenate([v_lat, v_ctx], axis=1),
                          MLA_SCALE).reshape(B, L, BRANCH_W)
    o_b, _, _ = gated_deltanet(dn_qkv, dn_z, dn_a, dn_b, lp, s_f0, s_b0)
    o_c = hyena(hy_in, lp)
    qn, kn, vn = na_split(na_in)
    o_d = na_latent(qn, kn, vn, na_k_ctx, na_v_ctx, lp["na_rpb"])
    x = x + g1 * merge_branches((o_a, o_b, o_c, o_d), gate_logits, lp)
    x = x + g2 * peer(modulate(x, lp["norm2_g"], sh2, sc2), lp)
    return x


def setup_inputs(seed: int = 0) -> dict:
    key = jax.random.key(seed)
    ks = iter(jax.random.split(key, 48))
    nrm = lambda shape, s: jax.random.normal(next(ks), shape, F32) * s
    gain = lambda shape: 1.0 + 0.01 * jax.random.normal(next(ks), shape, F32)
    dt = jnp.exp(jax.random.uniform(next(ks), (DEPTH, 2, DN_HEADS), F32, math.log(1e-3), math.log(1e-1)))
    return {
        "x_prompt": nrm((BATCH, SEQ, D_MODEL), 1.0),
        "x_sample": nrm((DEC_BATCH, DEC_SEQ, D_MODEL), 1.0),
        "cache_mla_ckv": nrm((DEC_BATCH, DEPTH, PAST_LEN, MLA_KV_RANK), 1.0),
        "cache_mla_krope": nrm((DEC_BATCH, DEPTH, PAST_LEN, MLA_ROPE), 1.0),
        "cache_na_k": nrm((DEC_BATCH, DEPTH, PAST_LEN, NA_HEADS, NA_HD), 1.0),
        "cache_na_v": nrm((DEC_BATCH, DEPTH, PAST_LEN, NA_HEADS, NA_HD), 1.0),
        "state_dn_fwd": nrm((DEC_BATCH, DEPTH, DN_HEADS, DN_DK, DN_DV), 0.1),
        "state_dn_bwd": nrm((DEC_BATCH, DEPTH, DN_HEADS, DN_DK, DN_DV), 0.1),
        "c": nrm((DEC_BATCH, D_MODEL), 1.0),
        "c_ctx": nrm((D_MODEL,), 1.0),
        "norm1_g": gain((DEPTH, D_MODEL)),
        "w_ada": nrm((DEPTH, D_MODEL, 6 * D_MODEL), 0.5 * D_MODEL ** -0.5),
        "b_ada": nrm((DEPTH, 6 * D_MODEL), 0.01),
        "w_in": nrm((DEPTH, D_MODEL, IN_COLS), D_MODEL ** -0.5),
        "mla_q_norm": gain((DEPTH, MLA_Q_RANK)),
        "mla_w_qb": nrm((DEPTH, MLA_Q_RANK, MLA_HEADS * (MLA_NOPE + MLA_ROPE)), MLA_Q_RANK ** -0.5),
        "mla_kv_norm": gain((DEPTH, MLA_KV_RANK)),
        "mla_w_kvb": nrm((DEPTH, MLA_KV_RANK, MLA_HEADS * (MLA_NOPE + MLA_V)), MLA_KV_RANK ** -0.5),
        "dn_conv": nrm((DEPTH, DN_CONV, DN_CONV_CH), DN_CONV ** -0.5),
        "dn_a_log": jnp.log(jax.random.uniform(next(ks), (DEPTH, 2, DN_HEADS), F32, 1.0, 16.0)),
        "dn_dt_bias": dt + jnp.log(-jnp.expm1(-dt)),
        "dn_out_norm": gain((DEPTH, DN_DV)),
        "hy_conv": nrm((DEPTH, HY_SHORT, (HY_ORDER + 1) * HY_WIDTH), HY_SHORT ** -0.5),
        "hy_w1": nrm((DEPTH, HY_EMB, HY_HID), HY_EMB ** -0.5),
        "hy_b1": nrm((DEPTH, HY_HID), 0.01),
        "hy_w2": nrm((DEPTH, HY_HID, HY_HID), HY_HID ** -0.5),
        "hy_b2": nrm((DEPTH, HY_HID), 0.01),
        "hy_w3": nrm((DEPTH, HY_HID, HY_ORDER * 2 * HY_WIDTH), 0.02),
        "hy_bias": nrm((DEPTH, HY_ORDER, HY_WIDTH), 0.1),
        "na_rpb": nrm((DEPTH, NA_HEADS, 2 * NA_KH - 1, 2 * NA_KW - 1), 0.02),
        "w_branch": nrm((DEPTH, N_BRANCH, BRANCH_W, D_MODEL), BRANCH_W ** -0.5),
        "w_out": nrm((DEPTH, D_MODEL, D_MODEL), D_MODEL ** -0.5),
        "norm2_g": gain((DEPTH, D_MODEL)),
        "peer_wq": nrm((DEPTH, D_MODEL, PEER_HEADS * PEER_DKEY), D_MODEL ** -0.5),
        "peer_keys": nrm((DEPTH, 2, PEER_NKEYS, PEER_DKEY // 2), (PEER_DKEY // 2) ** -0.5),
        "peer_u": nrm((DEPTH, PEER_N, D_MODEL), D_MODEL ** -0.5),
        "peer_v": nrm((DEPTH, PEER_N, D_MODEL), PEER_HEADS ** -0.5),
        "final_g": gain((D_MODEL,)),
    }


def reference(x_prompt, x_sample, cache_mla_ckv, cache_mla_krope, cache_na_k, cache_na_v, state_dn_fwd,
              state_dn_bwd, c, c_ctx, norm1_g, w_ada, b_ada, w_in, mla_q_norm, mla_w_qb, mla_kv_norm, mla_w_kvb,
              dn_conv, dn_a_log, dn_dt_bias, dn_out_norm, hy_conv, hy_w1, hy_b1, hy_w2, hy_b2, hy_w3, hy_bias,
              na_rpb, w_branch, w_out, norm2_g, peer_wq, peer_keys, peer_u, peer_v, final_g):
    layers = [dict(norm1_g=norm1_g[l], w_ada=w_ada[l], b_ada=b_ada[l], w_in=w_in[l], mla_q_norm=mla_q_norm[l],
                   mla_w_qb=mla_w_qb[l], mla_kv_norm=mla_kv_norm[l], mla_w_kvb=mla_w_kvb[l], dn_conv=dn_conv[l],
                   dn_a_log=dn_a_log[l], dn_dt_bias=dn_dt_bias[l], dn_out_norm=dn_out_norm[l], hy_conv=hy_conv[l],
                   hy_w1=hy_w1[l], hy_b1=hy_b1[l], hy_w2=hy_w2[l], hy_b2=hy_b2[l], hy_w3=hy_w3[l],
                   hy_bias=hy_bias[l], na_rpb=na_rpb[l], w_branch=w_branch[l], w_out=w_out[l],
                   norm2_g=norm2_g[l], peer_wq=peer_wq[l], peer_keys=peer_keys[l], peer_u=peer_u[l],
                   peer_v=peer_v[l])
              for l in range(DEPTH)]

    xp = x_prompt
    ctx = []
    for l in range(DEPTH):
        xp, ckv_n, krope, na_k, na_v, s_f, s_b = context_layer(xp, layers[l], c_ctx)
        ctx.append((ckv_n, krope, na_k, na_v, s_f, s_b))
    y_prompt = rms_norm(xp, final_g)

    xs = x_sample
    for l in range(DEPTH):
        xs = latent_layer(xs, layers[l], c, cache_mla_ckv[:, l], cache_mla_krope[:, l], cache_na_k[:, l],
                          cache_na_v[:, l], state_dn_fwd[:, l], state_dn_bwd[:, l])
    y_sample = rms_norm(xs, final_g)

    new_mla_ckv = jnp.stack([t[0] for t in ctx], axis=1)
    new_mla_krope = jnp.stack([t[1] for t in ctx], axis=1)
    new_na_k = jnp.stack([t[2] for t in ctx], axis=1)
    new_na_v = jnp.stack([t[3] for t in ctx], axis=1)
    new_dn_fwd = jnp.stack([t[4] for t in ctx], axis=1)
    new_dn_bwd = jnp.stack([t[5] for t in ctx], axis=1)
    return (y_prompt, y_sample, new_mla_ckv, new_mla_krope, new_na_k, new_na_v, new_dn_fwd, new_dn_bwd)
```

```python
import functools
import math

import numpy as np
import jax
import jax.numpy as jnp
from jax import lax
from jax.experimental import pallas as pl
from jax.experimental.pallas import tpu as pltpu

F32 = jnp.float32
BF16 = jnp.bfloat16

V7X_VMEM_BYTES = 64 * 1024 * 1024
VMEM_LIMIT = V7X_VMEM_BYTES - 8 * 1024 * 1024
LANES = 128
SUBLANES = 8

D_MODEL = 2048
GRID_W = 64
EPS = 1e-6
N_BRANCH = 4
BRANCH_W = D_MODEL // 2

MLA_HEADS = 8
MLA_NOPE = 128
MLA_ROPE = 64
MLA_V = BRANCH_W // MLA_HEADS
MLA_Q_RANK = D_MODEL // 4
MLA_KV_RANK = D_MODEL // 8
MLA_SCALE = (MLA_NOPE + MLA_ROPE) ** -0.5
MLA_QK_PAD = 256
ROPE_THETA = 10000.0

DN_HEADS = 8
DN_DK = 128
DN_DV = BRANCH_W // DN_HEADS
DN_CONV = 5
DN_CHUNK = 64
DN_CONV_CH = 2 * DN_HEADS * DN_DK + DN_HEADS * DN_DV

HY_WIDTH = BRANCH_W
HY_ORDER = 2
HY_SHORT = 3
HY_EMB = 33
HY_FAST_DECAY = 0.3
HY_SLOW_DECAY = 1.5
HY_DECAY_TARGET = 1e-2

NA_HEADS = 8
NA_HD = BRANCH_W // NA_HEADS
NA_KH = 8
NA_KW = 16
NA_SCALE = NA_HD ** -0.5
NA_QROWS = 4
NA_REGION = NA_KH + NA_QROWS
NEG_BIG = -1e30

PEER_HEADS = 8
PEER_NKEYS = 128
PEER_N = PEER_NKEYS * PEER_NKEYS
PEER_DKEY = 256
PEER_TOPK = 16

IN_SIZES = (MLA_Q_RANK, MLA_KV_RANK, MLA_ROPE, DN_CONV_CH, DN_HEADS * DN_DV, 2 * DN_HEADS, 2 * DN_HEADS,
            (HY_ORDER + 1) * HY_WIDTH, 3 * NA_HEADS * NA_HD, N_BRANCH * D_MODEL)


def _params(*sem):
    return pltpu.CompilerParams(dimension_semantics=sem, vmem_limit_bytes=VMEM_LIMIT)


def _pick(n, prefs):
    for p in prefs:
        if n % p == 0:
            return p
    return n


def _mm_kernel(x_ref, w_ref, o_ref):
    o_ref[...] = jnp.dot(x_ref[...], w_ref[...], preferred_element_type=F32).astype(o_ref.dtype)


def matmul(x, w, out_dtype=F32, tm=None, tn=None):
    M, K = x.shape
    N = w.shape[1]
    tm = tm or _pick(M, (1024, 512, 256, 128))
    tn = tn or _pick(N, (512, 256, 128))
    return pl.pallas_call(
        _mm_kernel,
        grid=(M // tm, N // tn),
        in_specs=[pl.BlockSpec((tm, K), lambda i, j: (i, 0)),
                  pl.BlockSpec((K, tn), lambda i, j: (0, j))],
        out_specs=pl.BlockSpec((tm, tn), lambda i, j: (i, j)),
        out_shape=jax.ShapeDtypeStruct((M, N), out_dtype),
        compiler_params=_params("parallel", "parallel"),
    )(x, w)


def _ada_kernel(c_ref, w_ref, b_ref, o_ref):
    c = c_ref[...]
    o_ref[...] = jnp.dot(c * jax.nn.sigmoid(c), w_ref[...], preferred_element_type=F32) + b_ref[...]


def ada_modulation(cond, w_ada, b_ada):
    R, K = cond.shape
    N = w_ada.shape[1]
    tn = _pick(N, (1536, 1024, 512))
    return pl.pallas_call(
        _ada_kernel,
        grid=(N // tn,),
        in_specs=[pl.BlockSpec((R, K), lambda j: (0, 0)),
                  pl.BlockSpec((K, tn), lambda j: (0, j)),
                  pl.BlockSpec((1, tn), lambda j: (0, j))],
        out_specs=pl.BlockSpec((R, tn), lambda j: (0, j)),
        out_shape=jax.ShapeDtypeStruct((R, N), F32),
        compiler_params=_params("parallel"),
    )(cond, w_ada, b_ada.reshape(1, N))


def _rms(x, g):
    return x * lax.rsqrt(jnp.mean(x * x, axis=-1, keepdims=True) + EPS) * g


def _modulate_kernel(x_ref, g_ref, sh_ref, sc_ref, o_ref):
    y = _rms(x_ref[...], g_ref[...])
    o_ref[...] = (y * (1.0 + sc_ref[0]) + sh_ref[0]).astype(o_ref.dtype)


def modulate(x, g, mods, cidx, tm, shift_slot, scale_slot):
    T, D = x.shape
    return pl.pallas_call(
        _modulate_kernel,
        grid=(T // tm,),
        in_specs=[pl.BlockSpec((tm, D), lambda i: (i, 0)),
                  pl.BlockSpec((1, D), lambda i: (0, 0)),
                  pl.BlockSpec((1, 1, D), lambda i: (cidx(i) * 6 + shift_slot, 0, 0)),
                  pl.BlockSpec((1, 1, D), lambda i: (cidx(i) * 6 + scale_slot, 0, 0))],
        out_specs=pl.BlockSpec((tm, D), lambda i: (i, 0)),
        out_shape=jax.ShapeDtypeStruct((T, D), BF16),
        compiler_params=_params("parallel"),
    )(x, g.reshape(1, D), mods, mods)


def _rmsnorm_kernel(x_ref, g_ref, o_ref):
    o_ref[...] = _rms(x_ref[...].astype(F32), g_ref[...]).astype(o_ref.dtype)


def rmsnorm(x, g, out_dtype=F32):
    T, D = x.shape
    tm = _pick(T, (1024, 512, 256, 128))
    return pl.pallas_call(
        _rmsnorm_kernel,
        grid=(T // tm,),
        in_specs=[pl.BlockSpec((tm, D), lambda i: (i, 0)),
                  pl.BlockSpec((1, D), lambda i: (0, 0))],
        out_specs=pl.BlockSpec((tm, D), lambda i: (i, 0)),
        out_shape=jax.ShapeDtypeStruct((T, D), out_dtype),
        compiler_params=_params("parallel"),
    )(x, g.reshape(1, D))


def _merge_kernel(a_ref, b_ref, c_ref, d_ref, ga_ref, gb_ref, gc_ref, gd_ref, w_ref, o_ref):
    acc = None
    for n, (br, gl) in enumerate(((a_ref, ga_ref), (b_ref, gb_ref), (c_ref, gc_ref), (d_ref, gd_ref))):
        proj = jnp.dot(br[...], w_ref[n], preferred_element_type=F32)
        term = jax.nn.sigmoid(gl[...]) * proj
        acc = term if acc is None else acc + term
    o_ref[...] = acc.astype(o_ref.dtype)


def merge_branches(branches, gate_logits, w_branch, tm):
    T = branches[0].shape[0]
    tn = 512
    br_spec = pl.BlockSpec((tm, BRANCH_W), lambda i, j: (i, 0))
    nj = D_MODEL // tn
    gl_specs = [pl.BlockSpec((tm, tn), functools.partial(lambda i, j, n: (i, n * nj + j), n=n))
                for n in range(N_BRANCH)]
    return pl.pallas_call(
        _merge_kernel,
        grid=(T // tm, D_MODEL // tn),
        in_specs=[br_spec, br_spec, br_spec, br_spec, *gl_specs,
                  pl.BlockSpec((N_BRANCH, BRANCH_W, tn), lambda i, j: (0, 0, j))],
        out_specs=pl.BlockSpec((tm, tn), lambda i, j: (i, j)),
        out_shape=jax.ShapeDtypeStruct((T, D_MODEL), BF16),
        compiler_params=_params("parallel", "parallel"),
    )(*branches, gate_logits, gate_logits, gate_logits, gate_logits, w_branch)


def _mm_resid_kernel(m_ref, w_ref, x_ref, g_ref, o_ref):
    o_ref[...] = x_ref[...] + g_ref[0] * jnp.dot(m_ref[...], w_ref[...], preferred_element_type=F32)


def matmul_gated_residual(m, w, x, mods, cidx, tm, gate_slot):
    T, K = m.shape
    N = w.shape[1]
    tn = 512
    return pl.pallas_call(
        _mm_resid_kernel,
        grid=(T // tm, N // tn),
        in_specs=[pl.BlockSpec((tm, K), lambda i, j: (i, 0)),
                  pl.BlockSpec((K, tn), lambda i, j: (0, j)),
                  pl.BlockSpec((tm, tn), lambda i, j: (i, j)),
                  pl.BlockSpec((1, 1, tn), lambda i, j: (cidx(i) * 6 + gate_slot, 0, j))],
        out_specs=pl.BlockSpec((tm, tn), lambda i, j: (i, j)),
        out_shape=jax.ShapeDtypeStruct((T, N), F32),
        compiler_params=_params("parallel", "parallel"),
    )(m, w, x, mods)


def _attn_kernel(q_ref, k_ref, v_ref, o_ref, *, scale):
    s = lax.dot_general(q_ref[0], k_ref[0], (((1,), (1,)), ((), ())), preferred_element_type=F32) * scale
    p = jnp.exp(s - jnp.max(s, axis=-1, keepdims=True))
    l = jnp.sum(p, axis=-1, keepdims=True)
    o = jnp.dot(p.astype(BF16), v_ref[0], preferred_element_type=F32)
    o_ref[0] = (o / l).astype(o_ref.dtype)


def attention(q, k, v, heads, dqk, dv, scale, q_off=0, k_off=0, v_off=0, v_stride=1, k_stride=1):
    B, Lq, _ = q.shape
    Lk = k.shape[1]
    tq = _pick(Lq, (256, 128))
    return pl.pallas_call(
        functools.partial(_attn_kernel, scale=scale),
        grid=(B, heads, Lq // tq),
        in_specs=[pl.BlockSpec((1, tq, dqk), lambda b, h, i: (b, i, q_off + h)),
                  pl.BlockSpec((1, Lk, dqk), lambda b, h, i: (b, 0, k_off + h * k_stride)),
                  pl.BlockSpec((1, Lk, dv), lambda b, h, i: (b, 0, v_off + h * v_stride))],
        out_specs=pl.BlockSpec((1, tq, dv), lambda b, h, i: (b, i, h)),
        out_shape=jax.ShapeDtypeStruct((B, Lq, heads * dv), BF16),
        compiler_params=_params("parallel", "parallel", "parallel"),
    )(q, k, v)


def _na_kernel(q_ref, k_ref, v_ref, kc_ref, vc_ref, bias_ref, o_ref, *, rows):
    rb = pl.program_id(2)
    nkeys = NA_REGION * GRID_W
    start = jnp.clip(rb * NA_QROWS - NA_KH // 2, 0, rows - NA_REGION) * GRID_W
    start = pl.multiple_of(start, GRID_W)
    q = q_ref[0]
    kr = k_ref[0, pl.ds(start, nkeys), :]
    vr = v_ref[0, pl.ds(start, nkeys), :]
    nt = (((1,), (1,)), ((), ()))
    s_win = lax.dot_general(q, kr, nt, preferred_element_type=F32) * NA_SCALE + bias_ref[0, 0]
    s_ctx = lax.dot_general(q, kc_ref[0], nt, preferred_element_type=F32) * NA_SCALE
    m = jnp.maximum(jnp.max(s_win, axis=-1, keepdims=True), jnp.max(s_ctx, axis=-1, keepdims=True))
    p_win = jnp.exp(s_win - m)
    p_ctx = jnp.exp(s_ctx - m)
    l = jnp.sum(p_win, axis=-1, keepdims=True) + jnp.sum(p_ctx, axis=-1, keepdims=True)
    o = (jnp.dot(p_win.astype(BF16), vr, preferred_element_type=F32)
         + jnp.dot(p_ctx.astype(BF16), vc_ref[0], preferred_element_type=F32))
    o_ref[0] = (o / l).astype(o_ref.dtype)


def _na_bias_index(rows):
    nblk = rows // NA_QROWS
    rb = np.arange(nblk)[:, None, None]
    qi = np.arange(NA_QROWS * GRID_W)[None, :, None]
    kj = np.arange(NA_REGION * GRID_W)[None, None, :]
    r = rb * NA_QROWS + qi // GRID_W
    c = qi % GRID_W
    start = np.clip(rb * NA_QROWS - NA_KH // 2, 0, rows - NA_REGION)
    kr = start + kj // GRID_W
    kc = kj % GRID_W
    row0 = np.clip(r - NA_KH // 2, 0, rows - NA_KH)
    col0 = np.clip(c - NA_KW // 2, 0, GRID_W - NA_KW)
    valid = (kr >= row0) & (kr < row0 + NA_KH) & (kc >= col0) & (kc < col0 + NA_KW)
    drow = np.clip(kr - r + (NA_KH - 1), 0, 2 * NA_KH - 2)
    dcol = np.clip(kc - c + (NA_KW - 1), 0, 2 * NA_KW - 2)
    idx = drow * (2 * NA_KW - 1) + dcol
    return idx.astype(np.int32), valid


def neighbourhood_attention(na_in, k_ctx, v_ctx, rpb):
    B, L, _ = na_in.shape
    Lc = k_ctx.shape[1]
    rows = L // GRID_W
    assert rows >= NA_REGION and rows % NA_QROWS == 0
    nblk = rows // NA_QROWS
    tq = NA_QROWS * GRID_W
    nkeys = NA_REGION * GRID_W
    idx, valid = _na_bias_index(rows)
    bias = jnp.where(valid[None], jnp.take(rpb.reshape(NA_HEADS, -1).astype(F32), idx, axis=1), NEG_BIG)
    H = NA_HEADS
    return pl.pallas_call(
        functools.partial(_na_kernel, rows=rows),
        grid=(B, H, nblk),
        in_specs=[pl.BlockSpec((1, tq, NA_HD), lambda b, h, r: (b, r, h)),
                  pl.BlockSpec((1, L, NA_HD), lambda b, h, r: (b, 0, H + h)),
                  pl.BlockSpec((1, L, NA_HD), lambda b, h, r: (b, 0, 2 * H + h)),
                  pl.BlockSpec((1, Lc, NA_HD), lambda b, h, r: (b, 0, h)),
                  pl.BlockSpec((1, Lc, NA_HD), lambda b, h, r: (b, 0, h)),
                  pl.BlockSpec((1, 1, tq, nkeys), lambda b, h, r: (h, r, 0, 0))],
        out_specs=pl.BlockSpec((1, tq, NA_HD), lambda b, h, r: (b, r, h)),
        out_shape=jax.ShapeDtypeStruct((B, L, H * NA_HD), BF16),
        compiler_params=_params("parallel", "parallel", "parallel"),
    )(na_in, na_in, na_in, k_ctx, v_ctx, bias)


def _top_rows(work, k):
    out = []
    for _ in range(k):
        mx = jnp.max(work, axis=0, keepdims=True)
        out.append(mx)
        work = jnp.where(work == mx, -jnp.inf, work)
    return out


_PEER_PAIRS = [(i, k) for i in range(PEER_TOPK) for k in range(PEER_TOPK) if (i + 1) * (k + 1) <= PEER_TOPK]


def _peer_route_kernel(q_ref, keys_ref, s1_ref, s2_ref, e1_ref, e2_ref, th_ref):
    half = PEER_DKEY // 2
    q = q_ref[...]
    nt = (((1,), (1,)), ((), ()))
    s1 = lax.dot_general(keys_ref[0], q[:, :half], nt, preferred_element_type=F32)
    s2 = lax.dot_general(keys_ref[1], q[:, half:], nt, preferred_element_type=F32)
    t1 = _top_rows(s1, PEER_TOPK)
    t2 = _top_rows(s2, PEER_TOPK)
    cand = jnp.concatenate([t1[i] + t2[k] for i, k in _PEER_PAIRS], axis=0)
    theta = _top_rows(cand, PEER_TOPK)[-1]
    top = t1[0] + t2[0]
    z = jnp.sum(jnp.where(cand >= theta, jnp.exp(cand - top), 0.0), axis=0, keepdims=True)
    s1_ref[0] = s1
    s2_ref[0] = s2
    e1_ref[0] = jnp.exp(s1 - t1[0]) / z
    e2_ref[0] = jnp.exp(s2 - t2[0])
    th_ref[0] = theta


def peer_route(q, keys):
    T = q.shape[0]
    tt = _pick(T, (512, 256, 128))
    H = PEER_HEADS
    big = jax.ShapeDtypeStruct((H, PEER_NKEYS, T), F32)
    big_spec = pl.BlockSpec((1, PEER_NKEYS, tt), lambda i, h: (h, 0, i))
    return pl.pallas_call(
        _peer_route_kernel,
        grid=(T // tt, H),
        in_specs=[pl.BlockSpec((tt, PEER_DKEY), lambda i, h: (i, h)),
                  pl.BlockSpec((2, PEER_NKEYS, PEER_DKEY // 2), lambda i, h: (0, 0, 0))],
        out_specs=[big_spec, big_spec, big_spec, big_spec,
                   pl.BlockSpec((1, 1, tt), lambda i, h: (h, 0, i))],
        out_shape=[big, big, big, big, jax.ShapeDtypeStruct((H, 1, T), F32)],
        compiler_params=_params("parallel", "parallel"),
    )(q, keys)


PEER_TE = 8 * PEER_NKEYS


def _peer_dense_kernel(h_ref, u_ref, v_ref, s1_ref, s2_ref, e1_ref, e2_ref, th_ref, x_ref, g_ref, o_ref,
                       xu_ref, w_ref, acc_ref):
    j = pl.program_id(1)
    tt = h_ref.shape[0]

    @pl.when(j == 0)
    def _():
        acc_ref[...] = jnp.zeros_like(acc_ref)

    xu_ref[...] = lax.dot_general(u_ref[...], h_ref[...], (((1,), (1,)), ((), ())), preferred_element_type=F32)

    n_tc = tt // LANES

    def slab(aa, tc, carry):
        r0 = aa * PEER_NKEYS
        c0 = pl.multiple_of(tc * LANES, LANES)
        gate = jnp.zeros((PEER_NKEYS, LANES), F32)
        for h in range(PEER_HEADS):
            s1 = s1_ref[h, aa:aa + 1, pl.ds(c0, LANES)]
            e1 = e1_ref[h, aa:aa + 1, pl.ds(c0, LANES)]
            th = th_ref[h, :, pl.ds(c0, LANES)]
            s2 = s2_ref[h, :, pl.ds(c0, LANES)]
            e2 = e2_ref[h, :, pl.ds(c0, LANES)]
            gate = gate + jnp.where(s1 + s2 >= th, e1 * e2, 0.0)
        pre = xu_ref[r0:r0 + PEER_NKEYS, pl.ds(c0, LANES)]
        act = 0.5 * pre * (1.0 + lax.erf(pre * (2.0 ** -0.5)))
        w_ref[r0:r0 + PEER_NKEYS, pl.ds(c0, LANES)] = (gate * act).astype(w_ref.dtype)
        return carry

    for aa in range(PEER_TE // PEER_NKEYS):
        lax.fori_loop(0, n_tc, functools.partial(slab, aa), 0)

    acc_ref[...] += lax.dot_general(w_ref[...], v_ref[...], (((0,), (0,)), ((), ())), preferred_element_type=F32)

    @pl.when(j == pl.num_programs(1) - 1)
    def _():
        o_ref[...] = x_ref[...] + g_ref[0] * acc_ref[...]


def peer_dense(h, u, v, route, x, mods, cidx_tt, tt, gate_slot):
    T, D = h.shape
    s1, s2, e1, e2, th = route
    H = PEER_HEADS
    na = PEER_TE // PEER_NKEYS
    tok = pl.BlockSpec((H, PEER_NKEYS, tt), lambda i, j: (0, 0, i))
    sub = pl.BlockSpec((H, na, tt), lambda i, j: (0, j, i))
    return pl.pallas_call(
        _peer_dense_kernel,
        grid=(T // tt, PEER_N // PEER_TE),
        in_specs=[pl.BlockSpec((tt, D), lambda i, j: (i, 0)),
                  pl.BlockSpec((PEER_TE, D), lambda i, j: (j, 0)),
                  pl.BlockSpec((PEER_TE, D), lambda i, j: (j, 0)),
                  sub, tok, sub, tok,
                  pl.BlockSpec((H, 1, tt), lambda i, j: (0, 0, i)),
                  pl.BlockSpec((tt, D), lambda i, j: (i, 0)),
                  pl.BlockSpec((1, 1, D), lambda i, j: (cidx_tt(i) * 6 + gate_slot, 0, 0))],
        out_specs=pl.BlockSpec((tt, D), lambda i, j: (i, 0)),
        out_shape=jax.ShapeDtypeStruct((T, D), F32),
        scratch_shapes=[pltpu.VMEM((PEER_TE, tt), F32),
                        pltpu.VMEM((PEER_TE, tt), BF16),
                        pltpu.VMEM((tt, D), F32)],
        compiler_params=_params("parallel", "arbitrary"),
    )(h, u, v, s1, s2, e1, e2, th, x, mods)


def _axial_rope(x):
    L = x.shape[1]
    half = x.shape[-1] // 2
    t = jnp.arange(L)
    inv = ROPE_THETA ** (-jnp.arange(0, half, 2, dtype=F32) / half)
    out = []
    for pos, xa in ((t // GRID_W, x[..., :half]), (t % GRID_W, x[..., half:])):
        ang = pos.astype(F32)[:, None] * inv[None, :]
        cos, sin = jnp.cos(ang)[:, None, :], jnp.sin(ang)[:, None, :]
        x1, x2 = xa[..., : half // 2], xa[..., half // 2:]
        out += [x1 * cos - x2 * sin, x2 * cos + x1 * sin]
    return jnp.concatenate(out, axis=-1)


def _l2_norm(x):
    return x * lax.rsqrt(jnp.sum(x * x, axis=-1, keepdims=True) + EPS)


def _depthwise_conv(x, w):
    k, ch = w.shape
    pad = k // 2
    return lax.conv_general_dilated(x, w[:, None, :], (1,), [(pad, pad)],
                                    dimension_numbers=("NWC", "WIO", "NWC"), feature_group_count=ch)


def _chunk_gated_delta(q, k, v, g, beta, s0):
    B, H, L, DK = q.shape
    DV = v.shape[-1]
    C = DN_CHUNK
    N = L // C
    q = q.reshape(B, H, N, C, DK)
    k = k.reshape(B, H, N, C, DK)
    v = v.reshape(B, H, N, C, DV)
    beta = beta.reshape(B, H, N, C)
    gc = jnp.cumsum(g.reshape(B, H, N, C), axis=-1)
    tri_incl = jnp.tril(jnp.ones((C, C), bool))
    tri_strict = jnp.tril(jnp.ones((C, C), bool), -1)
    decay = jnp.exp(jnp.where(tri_incl, gc[..., :, None] - gc[..., None, :], -jnp.inf))
    kk = jnp.einsum("bhncd,bhnsd->bhncs", k, k)
    a_low = jnp.where(tri_strict, beta[..., None] * kk * decay, 0.0)
    rhs = jnp.concatenate([v * beta[..., None], k * (beta * jnp.exp(gc))[..., None]], axis=-1)
    sol = lax.linalg.triangular_solve(jnp.eye(C, dtype=F32) + a_low, rhs, left_side=True, lower=True,
                                      unit_diagonal=True)
    u, w = sol[..., :DV], sol[..., DV:]
    qk = jnp.where(tri_incl, jnp.einsum("bhncd,bhnsd->bhncs", q, k) * decay, 0.0)

    def step(S, inp):
        qi, ki, ui, wi, gi, qki = inp
        v_new = ui - jnp.einsum("bhck,bhkv->bhcv", wi, S)
        o = (jnp.einsum("bhck,bhkv->bhcv", qi * jnp.exp(gi)[..., None], S)
             + jnp.einsum("bhcs,bhsv->bhcv", qki, v_new))
        g_last = gi[..., -1]
        S = (S * jnp.exp(g_last)[..., None, None]
             + jnp.einsum("bhck,bhcv->bhkv", ki * jnp.exp(g_last[..., None] - gi)[..., None], v_new))
        return S, o

    xs = tuple(jnp.moveaxis(t, 2, 0) for t in (q, k, u, w, gc, qk))
    S, o = lax.scan(step, s0, xs)
    return jnp.moveaxis(o, 0, 2).reshape(B, H, L, DV), S


def _gated_deltanet(qkv_raw, z, a, b, lp, s_f0, s_b0):
    B, L, _ = qkv_raw.shape
    qkv = jax.nn.silu(_depthwise_conv(qkv_raw, lp["dn_conv"]))
    q, k, v = jnp.split(qkv, [DN_HEADS * DN_DK, 2 * DN_HEADS * DN_DK], axis=-1)
    q = (_l2_norm(q.reshape(B, L, DN_HEADS, DN_DK)) * DN_DK ** -0.5).transpose(0, 2, 1, 3)
    k = _l2_norm(k.reshape(B, L, DN_HEADS, DN_DK)).transpose(0, 2, 1, 3)
    v = v.reshape(B, L, DN_HEADS, DN_DV).transpose(0, 2, 1, 3)
    a = a.reshape(B, L, 2, DN_HEADS)
    g = -jnp.exp(lp["dn_a_log"]) * jax.nn.softplus(a + lp["dn_dt_bias"])
    beta = jax.nn.sigmoid(b.reshape(B, L, 2, DN_HEADS))
    g = g.transpose(2, 0, 3, 1)
    beta = beta.transpose(2, 0, 3, 1)
    o_f, s_f = _chunk_gated_delta(q, k, v, g[0], beta[0], s_f0)
    fl = lambda t: jnp.flip(t, axis=2)
    o_b, s_b = _chunk_gated_delta(fl(q), fl(k), fl(v), fl(g[1]), fl(beta[1]), s_b0)
    o = (o_f + fl(o_b)).transpose(0, 2, 1, 3)
    on = o * lax.rsqrt(jnp.mean(o * o, axis=-1, keepdims=True) + EPS) * lp["dn_out_norm"]
    o = on * jax.nn.silu(z.reshape(B, L, DN_HEADS, DN_DV))
    return o.reshape(B, L, BRANCH_W), s_f, s_b


def _hyena_filters(L, lp):
    t01 = jnp.linspace(0.0, 1.0, L, dtype=F32)[:, None]
    bands = (HY_EMB - 1) // 2
    w = 2.0 * math.pi * jnp.arange(L, dtype=F32)[:, None] / L
    f = jnp.linspace(1e-4, bands - 1, bands, dtype=F32)[None, :]
    z = jnp.concatenate([t01, jnp.cos(f * w), -jnp.sin(f * w)], axis=-1)
    h = jnp.sin(z @ lp["hy_w1"] + lp["hy_b1"])
    h = jnp.sin(h @ lp["hy_w2"] + lp["hy_b2"])
    h = (h @ lp["hy_w3"]).reshape(L, HY_ORDER, 2, HY_WIDTH)
    max_decay = math.log(HY_DECAY_TARGET) / HY_FAST_DECAY
    min_decay = math.log(HY_DECAY_TARGET) / HY_SLOW_DECAY
    deltas = jnp.linspace(min_decay, max_decay, HY_WIDTH, dtype=F32)
    window = jnp.exp(-t01 * jnp.abs(deltas)[None, :])
    return h * window[:, None, None, :]


def _long_conv(u, hf, hb, bias):
    L = u.shape[1]
    kern = jnp.concatenate([hf, jnp.zeros_like(hf[:1]), jnp.flip(hb[1:], axis=0)], axis=0)
    uf = jnp.fft.rfft(u, n=2 * L, axis=1)
    kf = jnp.fft.rfft(kern, axis=0)
    y = jnp.fft.irfft(uf * kf[None], n=2 * L, axis=1)[:, :L]
    return y + u * bias


def _hyena(hy_raw, lp):
    L = hy_raw.shape[1]
    zc = _depthwise_conv(hy_raw, lp["hy_conv"])
    v, x1, x2 = jnp.split(zc, 3, axis=-1)
    filt = _hyena_filters(L, lp)
    y = v
    for o, gate in enumerate((x1, x2)):
        y = gate * _long_conv(y, filt[:, o, 0], filt[:, o, 1], lp["hy_bias"][o])
    return y


def _pad_heads(nope, rope):
    B, L, H, _ = nope.shape
    rope = jnp.broadcast_to(rope, (B, L, H, MLA_ROPE))
    pad = jnp.zeros((B, L, H, MLA_QK_PAD - MLA_NOPE - MLA_ROPE), nope.dtype)
    return jnp.concatenate([nope, rope, pad], axis=-1).reshape(B, L, H * MLA_QK_PAD).astype(BF16)


def _layer(x, lp, mods, dims, caches):
    Bc, Lc, Bl, Ll = dims
    Tc, Tl = Bc * Lc, Bl * Ll
    T = Tc + Tl
    tm = _pick(math.gcd(Tc, Ll), (1024, 512, 256, 128))

    def make_cidx(tile):
        nct, tpl = Tc // tile, Ll // tile
        return lambda i: jnp.where(i < nct, 0, 1 + (i - nct) // tpl)

    cidx = make_cidx(tm)
    h = modulate(x, lp["norm1_g"], mods, cidx, tm, 0, 1)

    w_in = lp["w_in"]
    cuts = np.cumsum((0,) + IN_SIZES)
    col = lambda a, b: w_in[:, cuts[a]:cuts[b]].astype(BF16)
    p_mla = matmul(h, col(0, 3))
    p_dn = matmul(h, col(3, 4))
    p_z = matmul(h, col(4, 5))
    p_ab = matmul(h, col(5, 7))
    p_hy = matmul(h, col(7, 8))
    p_na = matmul(h, col(8, 9))
    p_gate = matmul(h, col(9, 10))

    cq, ckv, krope = p_mla[:, :MLA_Q_RANK], p_mla[:, MLA_Q_RANK:MLA_Q_RANK + MLA_KV_RANK], p_mla[:, -MLA_ROPE:]

    cq_n = rmsnorm(cq, lp["mla_q_norm"], BF16)
    ckv_n = rmsnorm(ckv, lp["mla_kv_norm"], F32)
    q_all = matmul(cq_n, lp["mla_w_qb"].astype(BF16))
    w_kvb = lp["mla_w_kvb"].astype(BF16)
    kv_all = matmul(ckv_n.astype(BF16), w_kvb, BF16)
    ckv_ctx, krope_ctx, nak_ctx, nav_ctx, s_f0, s_b0 = caches
    kv_cache = matmul(ckv_ctx.reshape(-1, MLA_KV_RANK).astype(BF16), w_kvb, BF16)
    Lp = ckv_ctx.shape[1]

    def heads_q(t, B, L, rotate):
        t = t.reshape(B, L, MLA_HEADS, MLA_NOPE + MLA_ROPE)
        rope = t[..., MLA_NOPE:]
        if rotate:
            rope = _axial_rope(rope)
        return _pad_heads(t[..., :MLA_NOPE], rope)

    def heads_kv(kv, kr, B, L):
        kv = kv.reshape(B, L, MLA_HEADS, MLA_NOPE + MLA_V)
        return _pad_heads(kv[..., :MLA_NOPE], kr.reshape(B, L, 1, MLA_ROPE)), kv.reshape(B, L, -1)

    qc = heads_q(q_all[:Tc], Bc, Lc, False)
    kc, kvc = heads_kv(kv_all[:Tc], krope[:Tc], Bc, Lc)
    o_a_ctx = attention(qc, kc, kvc, MLA_HEADS, MLA_QK_PAD, MLA_V, MLA_SCALE, v_off=1, v_stride=2)
    ql = heads_q(q_all[Tc:], Bl, Ll, True)
    kl, kvl = heads_kv(kv_all[Tc:], _axial_rope(krope[Tc:].reshape(Bl, Ll, 1, MLA_ROPE)), Bl, Ll)
    kp, kvp = heads_kv(kv_cache, krope_ctx, Bl, Lp)
    o_a_lat = attention(ql, jnp.concatenate([kl, kp], axis=1), jnp.concatenate([kvl, kvp], axis=1),
                        MLA_HEADS, MLA_QK_PAD, MLA_V, MLA_SCALE, v_off=1, v_stride=2)
    o_a = jnp.concatenate([o_a_ctx.reshape(Tc, BRANCH_W), o_a_lat.reshape(Tl, BRANCH_W)])

    dn_a, dn_b = p_ab[:, :2 * DN_HEADS], p_ab[:, 2 * DN_HEADS:]
    zero_state = jnp.zeros((Bc, DN_HEADS, DN_DK, DN_DV), F32)
    o_b_ctx, s_f, s_b = _gated_deltanet(p_dn[:Tc].reshape(Bc, Lc, -1), p_z[:Tc].reshape(Bc, Lc, -1),
                                        dn_a[:Tc].reshape(Bc, Lc, -1), dn_b[:Tc].reshape(Bc, Lc, -1), lp,
                                        zero_state, zero_state)
    o_b_lat, _, _ = _gated_deltanet(p_dn[Tc:].reshape(Bl, Ll, -1), p_z[Tc:].reshape(Bl, Ll, -1),
                                    dn_a[Tc:].reshape(Bl, Ll, -1), dn_b[Tc:].reshape(Bl, Ll, -1), lp, s_f0, s_b0)
    o_b = jnp.concatenate([o_b_ctx.reshape(Tc, BRANCH_W), o_b_lat.reshape(Tl, BRANCH_W)]).astype(BF16)

    o_c = jnp.concatenate([_hyena(p_hy[:Tc].reshape(Bc, Lc, -1), lp).reshape(Tc, BRANCH_W),
                           _hyena(p_hy[Tc:].reshape(Bl, Ll, -1), lp).reshape(Tl, BRANCH_W)]).astype(BF16)

    na_bf = p_na.astype(BF16)
    na_ctx = na_bf[:Tc].reshape(Bc, Lc, -1)
    o_d_ctx = attention(na_ctx, na_ctx, na_ctx, NA_HEADS, NA_HD, NA_HD, NA_SCALE, k_off=NA_HEADS, v_off=2 * NA_HEADS)
    o_d_lat = neighbourhood_attention(na_bf[Tc:].reshape(Bl, Ll, -1), nak_ctx.reshape(Bl, Lp, -1).astype(BF16),
                                      nav_ctx.reshape(Bl, Lp, -1).astype(BF16), lp["na_rpb"])
    o_d = jnp.concatenate([o_d_ctx.reshape(Tc, BRANCH_W), o_d_lat.reshape(Tl, BRANCH_W)])

    merged = merge_branches((o_a, o_b, o_c, o_d), p_gate, lp["w_branch"].astype(BF16), tm)
    x = matmul_gated_residual(merged, lp["w_out"].astype(BF16), x, mods, cidx, tm, 2)

    h2 = modulate(x, lp["norm2_g"], mods, cidx, tm, 3, 4)
    pq = matmul(h2, lp["peer_wq"].astype(BF16))
    route = peer_route(pq, lp["peer_keys"])
    tt = _pick(math.gcd(Tc, Ll), (512, 256, 128))
    x = peer_dense(h2, lp["peer_u"].astype(BF16), lp["peer_v"].astype(BF16), route, x, mods, make_cidx(tt), tt, 5)

    ctx_out = (ckv_n[:Tc].reshape(Bc, Lc, MLA_KV_RANK), krope[:Tc].reshape(Bc, Lc, MLA_ROPE),
               p_na[:Tc, BRANCH_W:2 * BRANCH_W].reshape(Bc, Lc, NA_HEADS, NA_HD),
               p_na[:Tc, 2 * BRANCH_W:].reshape(Bc, Lc, NA_HEADS, NA_HD), s_f, s_b)
    return x, ctx_out


def kernel(x_prompt, x_sample, cache_mla_ckv, cache_mla_krope, cache_na_k, cache_na_v, state_dn_fwd, state_dn_bwd, c, c_ctx, norm1_g, w_ada, b_ada, w_in, mla_q_norm, mla_w_qb, mla_kv_norm, mla_w_kvb, dn_conv, dn_a_log, dn_dt_bias, dn_out_norm, hy_conv, hy_w1, hy_b1, hy_w2, hy_b2, hy_w3, hy_bias, na_rpb, w_branch, w_out, norm2_g, peer_wq, peer_keys, peer_u, peer_v, final_g):
    Bc, Lc, D = x_prompt.shape
    Bl, Ll, _ = x_sample.shape
    depth = w_in.shape[0]
    Tc = Bc * Lc
    x = jnp.concatenate([x_prompt.reshape(Tc, D), x_sample.reshape(Bl * Ll, D)])
    n_cond = 1 + Bl
    cond_rows = -(-n_cond // SUBLANES) * SUBLANES
    cond = jnp.concatenate([c_ctx[None], c, jnp.zeros((cond_rows - n_cond, D), F32)])

    names = ("norm1_g", "w_in", "mla_q_norm", "mla_w_qb", "mla_kv_norm", "mla_w_kvb", "dn_conv", "dn_a_log",
             "dn_dt_bias", "dn_out_norm", "hy_conv", "hy_w1", "hy_b1", "hy_w2", "hy_b2", "hy_w3", "hy_bias",
             "na_rpb", "w_branch", "w_out", "norm2_g", "peer_wq", "peer_keys", "peer_u", "peer_v")
    vals = (norm1_g, w_in, mla_q_norm, mla_w_qb, mla_kv_norm, mla_w_kvb, dn_conv, dn_a_log, dn_dt_bias,
            dn_out_norm, hy_conv, hy_w1, hy_b1, hy_w2, hy_b2, hy_w3, hy_bias, na_rpb, w_branch, w_out, norm2_g,
            peer_wq, peer_keys, peer_u, peer_v)
    ctx = []
    for l in range(depth):
        lp = {n: v[l] for n, v in zip(names, vals)}
        mods = ada_modulation(cond, w_ada[l], b_ada[l]).reshape(cond_rows * 6, 1, D)
        caches = (cache_mla_ckv[:, l], cache_mla_krope[:, l], cache_na_k[:, l], cache_na_v[:, l],
                  state_dn_fwd[:, l], state_dn_bwd[:, l])
        x, ctx_out = _layer(x, lp, mods, (Bc, Lc, Bl, Ll), caches)
        ctx.append(ctx_out)
    y = rmsnorm(x, final_g)
    stack = lambda k: jnp.stack([t[k] for t in ctx], axis=1)
    return (y[:Tc].reshape(Bc, Lc, D), y[Tc:].reshape(Bl, Ll, D), stack(0), stack(1), stack(2), stack(3),
            stack(4), stack(5))
```

```python
import functools
import math

import numpy as np
import jax
import jax.numpy as jnp
from jax import lax
from jax.experimental import pallas as pl
from jax.experimental.pallas import tpu as pltpu

F32 = jnp.float32
BF16 = jnp.bfloat16

V7X_VMEM_BYTES = 64 * 1024 * 1024
VMEM_LIMIT = V7X_VMEM_BYTES - 8 * 1024 * 1024
LANES = 128
SUBLANES = 8

D_MODEL = 2048
GRID_W = 64
EPS = 1e-6
N_BRANCH = 4
BRANCH_W = D_MODEL // 2

MLA_HEADS = 8
MLA_NOPE = 128
MLA_ROPE = 64
MLA_V = BRANCH_W // MLA_HEADS
MLA_Q_RANK = D_MODEL // 4
MLA_KV_RANK = D_MODEL // 8
MLA_SCALE = (MLA_NOPE + MLA_ROPE) ** -0.5
MLA_QK_PAD = 256
ROPE_THETA = 10000.0

DN_HEADS = 8
DN_DK = 128
DN_DV = BRANCH_W // DN_HEADS
DN_CONV = 5
DN_CHUNK = 64
DN_CONV_CH = 2 * DN_HEADS * DN_DK + DN_HEADS * DN_DV

HY_WIDTH = BRANCH_W
HY_ORDER = 2
HY_SHORT = 3
HY_EMB = 33
HY_FAST_DECAY = 0.3
HY_SLOW_DECAY = 1.5
HY_DECAY_TARGET = 1e-2

NA_HEADS = 8
NA_HD = BRANCH_W // NA_HEADS
NA_KH = 8
NA_KW = 16
NA_SCALE = NA_HD ** -0.5
NA_QROWS = 4
NA_REGION = NA_KH + NA_QROWS
NEG_BIG = -1e30

PEER_HEADS = 8
PEER_NKEYS = 128
PEER_N = PEER_NKEYS * PEER_NKEYS
PEER_DKEY = 256
PEER_TOPK = 16

IN_SIZES = (MLA_Q_RANK, MLA_KV_RANK, MLA_ROPE, DN_CONV_CH, DN_HEADS * DN_DV, 2 * DN_HEADS, 2 * DN_HEADS,
            (HY_ORDER + 1) * HY_WIDTH, 3 * NA_HEADS * NA_HD, N_BRANCH * D_MODEL)


def _params(*sem):
    return pltpu.CompilerParams(dimension_semantics=sem, vmem_limit_bytes=VMEM_LIMIT)


def _pick(n, prefs):
    for p in prefs:
        if n % p == 0:
            return p
    return n


def _mm_kernel(x_ref, w_ref, o_ref):
    o_ref[...] = jnp.dot(x_ref[...], w_ref[...], preferred_element_type=F32).astype(o_ref.dtype)


def matmul(x, w, out_dtype=F32, tm=None, tn=None):
    M, K = x.shape
    N = w.shape[1]
    tm = tm or _pick(M, (1024, 512, 256, 128))
    tn = tn or _pick(N, (512, 256, 128))
    return pl.pallas_call(
        _mm_kernel,
        grid=(M // tm, N // tn),
        in_specs=[pl.BlockSpec((tm, K), lambda i, j: (i, 0)),
                  pl.BlockSpec((K, tn), lambda i, j: (0, j))],
        out_specs=pl.BlockSpec((tm, tn), lambda i, j: (i, j)),
        out_shape=jax.ShapeDtypeStruct((M, N), out_dtype),
        compiler_params=_params("parallel", "parallel"),
    )(x, w)


def _ada_kernel(c_ref, w_ref, b_ref, o_ref):
    c = c_ref[...]
    o_ref[...] = jnp.dot(c * jax.nn.sigmoid(c), w_ref[...], preferred_element_type=F32) + b_ref[...]


def ada_modulation(cond, w_ada, b_ada):
    R, K = cond.shape
    N = w_ada.shape[1]
    tn = _pick(N, (1536, 1024, 512))
    return pl.pallas_call(
        _ada_kernel,
        grid=(N // tn,),
        in_specs=[pl.BlockSpec((R, K), lambda j: (0, 0)),
                  pl.BlockSpec((K, tn), lambda j: (0, j)),
                  pl.BlockSpec((1, tn), lambda j: (0, j))],
        out_specs=pl.BlockSpec((R, tn), lambda j: (0, j)),
        out_shape=jax.ShapeDtypeStruct((R, N), F32),
        compiler_params=_params("parallel"),
    )(cond, w_ada, b_ada.reshape(1, N))


def _rms(x, g):
    return x * lax.rsqrt(jnp.mean(x * x, axis=-1, keepdims=True) + EPS) * g


def _modulate_kernel(x_ref, g_ref, sh_ref, sc_ref, o_ref):
    y = _rms(x_ref[...], g_ref[...])
    o_ref[...] = (y * (1.0 + sc_ref[0]) + sh_ref[0]).astype(o_ref.dtype)


def modulate(x, g, mods, cidx, tm, shift_slot, scale_slot):
    T, D = x.shape
    return pl.pallas_call(
        _modulate_kernel,
        grid=(T // tm,),
        in_specs=[pl.BlockSpec((tm, D), lambda i: (i, 0)),
                  pl.BlockSpec((1, D), lambda i: (0, 0)),
                  pl.BlockSpec((1, 1, D), lambda i: (cidx(i) * 6 + shift_slot, 0, 0)),
                  pl.BlockSpec((1, 1, D), lambda i: (cidx(i) * 6 + scale_slot, 0, 0))],
        out_specs=pl.BlockSpec((tm, D), lambda i: (i, 0)),
        out_shape=jax.ShapeDtypeStruct((T, D), BF16),
        compiler_params=_params("parallel"),
    )(x, g.reshape(1, D), mods, mods)


def _rmsnorm_kernel(x_ref, g_ref, o_ref):
    o_ref[...] = _rms(x_ref[...].astype(F32), g_ref[...]).astype(o_ref.dtype)


def rmsnorm(x, g, out_dtype=F32):
    T, D = x.shape
    tm = _pick(T, (1024, 512, 256, 128))
    return pl.pallas_call(
        _rmsnorm_kernel,
        grid=(T // tm,),
        in_specs=[pl.BlockSpec((tm, D), lambda i: (i, 0)),
                  pl.BlockSpec((1, D), lambda i: (0, 0))],
        out_specs=pl.BlockSpec((tm, D), lambda i: (i, 0)),
        out_shape=jax.ShapeDtypeStruct((T, D), out_dtype),
        compiler_params=_params("parallel"),
    )(x, g.reshape(1, D))


def _merge_kernel(a_ref, b_ref, c_ref, d_ref, ga_ref, gb_ref, gc_ref, gd_ref, w_ref, o_ref):
    acc = None
    for n, (br, gl) in enumerate(((a_ref, ga_ref), (b_ref, gb_ref), (c_ref, gc_ref), (d_ref, gd_ref))):
        proj = jnp.dot(br[...], w_ref[n], preferred_element_type=F32)
        term = jax.nn.sigmoid(gl[...].astype(F32)) * proj
        acc = term if acc is None else acc + term
    o_ref[...] = acc.astype(o_ref.dtype)


def merge_branches(branches, gate_logits, w_branch, tm):
    T = branches[0].shape[0]
    tn = 512
    br_spec = pl.BlockSpec((tm, BRANCH_W), lambda i, j: (i, 0))
    nj = D_MODEL // tn
    gl_specs = [pl.BlockSpec((tm, tn), functools.partial(lambda i, j, n: (i, n * nj + j), n=n))
                for n in range(N_BRANCH)]
    return pl.pallas_call(
        _merge_kernel,
        grid=(T // tm, D_MODEL // tn),
        in_specs=[br_spec, br_spec, br_spec, br_spec, *gl_specs,
                  pl.BlockSpec((N_BRANCH, BRANCH_W, tn), lambda i, j: (0, 0, j))],
        out_specs=pl.BlockSpec((tm, tn), lambda i, j: (i, j)),
        out_shape=jax.ShapeDtypeStruct((T, D_MODEL), BF16),
        compiler_params=_params("parallel", "parallel"),
    )(*branches, gate_logits, gate_logits, gate_logits, gate_logits, w_branch)


def _mm_resid_kernel(m_ref, w_ref, x_ref, g_ref, o_ref):
    o_ref[...] = x_ref[...] + g_ref[0] * jnp.dot(m_ref[...], w_ref[...], preferred_element_type=F32)


def matmul_gated_residual(m, w, x, mods, cidx, tm, gate_slot):
    T, K = m.shape
    N = w.shape[1]
    tn = 512
    return pl.pallas_call(
        _mm_resid_kernel,
        grid=(T // tm, N // tn),
        in_specs=[pl.BlockSpec((tm, K), lambda i, j: (i, 0)),
                  pl.BlockSpec((K, tn), lambda i, j: (0, j)),
                  pl.BlockSpec((tm, tn), lambda i, j: (i, j)),
                  pl.BlockSpec((1, 1, tn), lambda i, j: (cidx(i) * 6 + gate_slot, 0, j))],
        out_specs=pl.BlockSpec((tm, tn), lambda i, j: (i, j)),
        out_shape=jax.ShapeDtypeStruct((T, N), F32),
        compiler_params=_params("parallel", "parallel"),
    )(m, w, x, mods)


def _attn_kernel(q_ref, k_ref, v_ref, o_ref, *, scale):
    s = lax.dot_general(q_ref[0], k_ref[0], (((1,), (1,)), ((), ())), preferred_element_type=F32) * scale
    p = jnp.exp(s - jnp.max(s, axis=-1, keepdims=True))
    l = jnp.sum(p, axis=-1, keepdims=True)
    o = jnp.dot(p.astype(BF16), v_ref[0], preferred_element_type=F32)
    o_ref[0] = (o / l).astype(o_ref.dtype)


def attention(q, k, v, heads, dqk, dv, scale, q_off=0, k_off=0, v_off=0, v_stride=1, k_stride=1):
    B, Lq, _ = q.shape
    Lk = k.shape[1]
    tq = _pick(Lq, (256, 128))
    return pl.pallas_call(
        functools.partial(_attn_kernel, scale=scale),
        grid=(B, heads, Lq // tq),
        in_specs=[pl.BlockSpec((1, tq, dqk), lambda b, h, i: (b, i, q_off + h)),
                  pl.BlockSpec((1, Lk, dqk), lambda b, h, i: (b, 0, k_off + h * k_stride)),
                  pl.BlockSpec((1, Lk, dv), lambda b, h, i: (b, 0, v_off + h * v_stride))],
        out_specs=pl.BlockSpec((1, tq, dv), lambda b, h, i: (b, i, h)),
        out_shape=jax.ShapeDtypeStruct((B, Lq, heads * dv), BF16),
        compiler_params=_params("parallel", "parallel", "parallel"),
    )(q, k, v)


def _na_kernel(q_ref, k_ref, v_ref, kc_ref, vc_ref, bias_ref, o_ref, *, rows):
    rb = pl.program_id(2)
    nkeys = NA_REGION * GRID_W
    start = jnp.clip(rb * NA_QROWS - NA_KH // 2, 0, rows - NA_REGION) * GRID_W
    start = pl.multiple_of(start, GRID_W)
    q = q_ref[0]
    kr = k_ref[0, pl.ds(start, nkeys), :]
    vr = v_ref[0, pl.ds(start, nkeys), :]
    nt = (((1,), (1,)), ((), ()))
    s_win = lax.dot_general(q, kr, nt, preferred_element_type=F32) * NA_SCALE + bias_ref[0, 0]
    s_ctx = lax.dot_general(q, kc_ref[0], nt, preferred_element_type=F32) * NA_SCALE
    m = jnp.maximum(jnp.max(s_win, axis=-1, keepdims=True), jnp.max(s_ctx, axis=-1, keepdims=True))
    p_win = jnp.exp(s_win - m)
    p_ctx = jnp.exp(s_ctx - m)
    l = jnp.sum(p_win, axis=-1, keepdims=True) + jnp.sum(p_ctx, axis=-1, keepdims=True)
    o = (jnp.dot(p_win.astype(BF16), vr, preferred_element_type=F32)
         + jnp.dot(p_ctx.astype(BF16), vc_ref[0], preferred_element_type=F32))
    o_ref[0] = (o / l).astype(o_ref.dtype)


def _na_bias_tables(rpb, rows):
    H = rpb.shape[0]
    n_dr, n_dc = 2 * NA_KH - 1, 2 * NA_KW - 1
    nblk = rows // NA_QROWS
    span = 2 * GRID_W
    lo = GRID_W - NA_KW
    v = jnp.pad(rpb.astype(F32), ((0, 0), (0, 0), (lo, span - lo - n_dc)), constant_values=NEG_BIG)
    skew = jnp.tile(v, (1, 1, GRID_W))[..., :GRID_W * (span - 1)].reshape(H, n_dr, GRID_W, span - 1)
    band = skew[..., GRID_W - 1:]
    c = np.arange(GRID_W)[:, None]
    kc = np.arange(GRID_W)[None, :]
    col0 = np.clip(c - NA_KW // 2, 0, GRID_W - NA_KW)
    col_ok = (kc >= col0) & (kc < col0 + NA_KW)
    band = jnp.where(col_ok, band, NEG_BIG)
    band = jnp.concatenate([band, jnp.full((H, 1, GRID_W, GRID_W), NEG_BIG, F32)], axis=1)
    tiles = []
    for rb in (0, 1, nblk - 1):
        start = int(np.clip(rb * NA_QROWS - NA_KH // 2, 0, rows - NA_REGION))
        for qr in range(NA_QROWS):
            r = rb * NA_QROWS + qr
            row0 = int(np.clip(r - NA_KH // 2, 0, rows - NA_KH))
            for j in range(NA_REGION):
                kr = start + j
                tiles.append(kr - r + NA_KH - 1 if row0 <= kr < row0 + NA_KH else n_dr)
    t = jnp.stack([band[:, s] for s in tiles], axis=1).reshape(H, 3, NA_QROWS, NA_REGION, GRID_W, GRID_W)
    return t.transpose(0, 1, 2, 4, 3, 5).reshape(H, 3, NA_QROWS * GRID_W, NA_REGION * GRID_W)


def neighbourhood_attention(na_in, k_ctx, v_ctx, rpb):
    B, L, _ = na_in.shape
    Lc = k_ctx.shape[1]
    rows = L // GRID_W
    assert rows >= NA_REGION + NA_QROWS and rows % NA_QROWS == 0
    nblk = rows // NA_QROWS
    tq = NA_QROWS * GRID_W
    nkeys = NA_REGION * GRID_W
    bias = _na_bias_tables(rpb, rows)
    kind = lambda r: jnp.where(r == 0, 0, jnp.where(r == nblk - 1, 2, 1))
    H = NA_HEADS
    return pl.pallas_call(
        functools.partial(_na_kernel, rows=rows),
        grid=(B, H, nblk),
        in_specs=[pl.BlockSpec((1, tq, NA_HD), lambda b, h, r: (b, r, h)),
                  pl.BlockSpec((1, L, NA_HD), lambda b, h, r: (b, 0, H + h)),
                  pl.BlockSpec((1, L, NA_HD), lambda b, h, r: (b, 0, 2 * H + h)),
                  pl.BlockSpec((1, Lc, NA_HD), lambda b, h, r: (b, 0, h)),
                  pl.BlockSpec((1, Lc, NA_HD), lambda b, h, r: (b, 0, h)),
                  pl.BlockSpec((1, 1, tq, nkeys), lambda b, h, r: (h, kind(r), 0, 0))],
        out_specs=pl.BlockSpec((1, tq, NA_HD), lambda b, h, r: (b, r, h)),
        out_shape=jax.ShapeDtypeStruct((B, L, H * NA_HD), BF16),
        compiler_params=_params("parallel", "parallel", "parallel"),
    )(na_in, na_in, na_in, k_ctx, v_ctx, bias)


def _top_rows(work, k):
    out = []
    for _ in range(k):
        mx = jnp.max(work, axis=0, keepdims=True)
        out.append(mx)
        work = jnp.where(work == mx, -jnp.inf, work)
    return out


_PEER_PAIRS = [(i, k) for i in range(PEER_TOPK) for k in range(PEER_TOPK) if (i + 1) * (k + 1) <= PEER_TOPK]


def _peer_route_kernel(q_ref, keys_ref, s1_ref, s2_ref, e1_ref, e2_ref, th_ref):
    half = PEER_DKEY // 2
    q = q_ref[...]
    nt = (((1,), (1,)), ((), ()))
    s1 = lax.dot_general(keys_ref[0], q[:, :half], nt, preferred_element_type=F32)
    s2 = lax.dot_general(keys_ref[1], q[:, half:], nt, preferred_element_type=F32)
    t1 = _top_rows(s1, PEER_TOPK)
    t2 = _top_rows(s2, PEER_TOPK)
    cand = jnp.concatenate([t1[i] + t2[k] for i, k in _PEER_PAIRS], axis=0)
    theta = _top_rows(cand, PEER_TOPK)[-1]
    top = t1[0] + t2[0]
    z = jnp.sum(jnp.where(cand >= theta, jnp.exp(cand - top), 0.0), axis=0, keepdims=True)
    s1_ref[0] = s1
    s2_ref[0] = s2
    e1_ref[0] = jnp.exp(s1 - t1[0]) / z
    e2_ref[0] = jnp.exp(s2 - t2[0])
    th_ref[0] = theta


def peer_route(q, keys):
    T = q.shape[0]
    tt = _pick(T, (512, 256, 128))
    H = PEER_HEADS
    big = jax.ShapeDtypeStruct((H, PEER_NKEYS, T), F32)
    big_spec = pl.BlockSpec((1, PEER_NKEYS, tt), lambda i, h: (h, 0, i))
    return pl.pallas_call(
        _peer_route_kernel,
        name="peer_route",
        grid=(T // tt, H),
        in_specs=[pl.BlockSpec((tt, PEER_DKEY), lambda i, h: (i, h)),
                  pl.BlockSpec((2, PEER_NKEYS, PEER_DKEY // 2), lambda i, h: (0, 0, 0))],
        out_specs=[big_spec, big_spec, big_spec, big_spec,
                   pl.BlockSpec((1, 1, tt), lambda i, h: (h, 0, i))],
        out_shape=[big, big, big, big, jax.ShapeDtypeStruct((H, 1, T), F32)],
        compiler_params=_params("parallel", "parallel"),
    )(q, keys)


PEER_TE = 8 * PEER_NKEYS


def _peer_dense_kernel(h_ref, u_ref, v_ref, s1_ref, s2_ref, e1_ref, e2_ref, th_ref, x_ref, g_ref, o_ref,
                       xu_ref, w_ref, acc_ref):
    j = pl.program_id(1)
    tt = h_ref.shape[0]

    @pl.when(j == 0)
    def _():
        acc_ref[...] = jnp.zeros_like(acc_ref)

    xu_ref[...] = lax.dot_general(u_ref[...], h_ref[...], (((1,), (1,)), ((), ())), preferred_element_type=F32)

    n_tc = tt // LANES

    def slab(aa, tc, carry):
        r0 = aa * PEER_NKEYS
        c0 = pl.multiple_of(tc * LANES, LANES)
        gate = jnp.zeros((PEER_NKEYS, LANES), F32)
        for h in range(PEER_HEADS):
            s1 = s1_ref[h, aa:aa + 1, pl.ds(c0, LANES)]
            e1 = e1_ref[h, aa:aa + 1, pl.ds(c0, LANES)]
            th = th_ref[h, :, pl.ds(c0, LANES)]
            s2 = s2_ref[h, :, pl.ds(c0, LANES)]
            e2 = e2_ref[h, :, pl.ds(c0, LANES)]
            gate = gate + jnp.where(s1 + s2 >= th, e1 * e2, 0.0)
        pre = xu_ref[r0:r0 + PEER_NKEYS, pl.ds(c0, LANES)]
        act = 0.5 * pre * (1.0 + lax.erf(pre * (2.0 ** -0.5)))
        w_ref[r0:r0 + PEER_NKEYS, pl.ds(c0, LANES)] = (gate * act).astype(w_ref.dtype)
        return carry

    for aa in range(PEER_TE // PEER_NKEYS):
        lax.fori_loop(0, n_tc, functools.partial(slab, aa), 0)

    acc_ref[...] += lax.dot_general(w_ref[...], v_ref[...], (((0,), (0,)), ((), ())), preferred_element_type=F32)

    @pl.when(j == pl.num_programs(1) - 1)
    def _():
        o_ref[...] = x_ref[...] + g_ref[0] * acc_ref[...]


def peer_dense(h, u, v, route, x, mods, cidx_tt, tt, gate_slot):
    T, D = h.shape
    s1, s2, e1, e2, th = route
    H = PEER_HEADS
    na = PEER_TE // PEER_NKEYS
    tok = pl.BlockSpec((H, PEER_NKEYS, tt), lambda i, j: (0, 0, i))
    sub = pl.BlockSpec((H, na, tt), lambda i, j: (0, j, i))
    return pl.pallas_call(
        _peer_dense_kernel,
        name="peer_dense",
        grid=(T // tt, PEER_N // PEER_TE),
        in_specs=[pl.BlockSpec((tt, D), lambda i, j: (i, 0)),
                  pl.BlockSpec((PEER_TE, D), lambda i, j: (j, 0)),
                  pl.BlockSpec((PEER_TE, D), lambda i, j: (j, 0)),
                  sub, tok, sub, tok,
                  pl.BlockSpec((H, 1, tt), lambda i, j: (0, 0, i)),
                  pl.BlockSpec((tt, D), lambda i, j: (i, 0)),
                  pl.BlockSpec((1, 1, D), lambda i, j: (cidx_tt(i) * 6 + gate_slot, 0, 0))],
        out_specs=pl.BlockSpec((tt, D), lambda i, j: (i, 0)),
        out_shape=jax.ShapeDtypeStruct((T, D), F32),
        scratch_shapes=[pltpu.VMEM((PEER_TE, tt), F32),
                        pltpu.VMEM((PEER_TE, tt), BF16),
                        pltpu.VMEM((tt, D), F32)],
        compiler_params=_params("parallel", "arbitrary"),
    )(h, u, v, s1, s2, e1, e2, th, x, mods)


HY_HID = 64
HY_BANDS = (HY_EMB - 1) // 2
HY_FREQ_CHUNK = 512


def _hy_filter_kernel(w1t_ref, w1c_ref, w1s_ref, b1_ref, w2_ref, b2_ref, w3_ref, absd_ref, sum_ref, dif_ref, *, L):
    tl = sum_ref.shape[0]
    W = HY_WIDTH
    pos_i = pl.program_id(0) * tl + lax.broadcasted_iota(jnp.int32, (tl, 1), 0)
    pos = pos_i.astype(F32)
    t01 = pos * (1.0 / (L - 1))
    w = (2.0 * math.pi) * pos / L
    band = lax.broadcasted_iota(jnp.int32, (1, HY_BANDS), 1).astype(F32)
    f = 1e-4 + band * ((HY_BANDS - 1 - 1e-4) / (HY_BANDS - 1))
    fw = f * w
    pre = (t01 * w1t_ref[...] + jnp.dot(jnp.cos(fw), w1c_ref[...], preferred_element_type=F32)
           - jnp.dot(jnp.sin(fw), w1s_ref[...], preferred_element_type=F32) + b1_ref[...])
    h = jnp.sin(pre)
    h = jnp.sin(jnp.dot(h, w2_ref[...], preferred_element_type=F32) + b2_ref[...])
    h = jnp.dot(h, w3_ref[...], preferred_element_type=F32)
    window = jnp.exp(-t01 * absd_ref[...])
    for o in range(HY_ORDER):
        hf = h[:, (2 * o) * W:(2 * o + 1) * W] * window
        hb = jnp.where(pos_i == 0, 0.0, h[:, (2 * o + 1) * W:(2 * o + 2) * W] * window)
        sum_ref[:, o * W:(o + 1) * W] = (hf + hb).astype(sum_ref.dtype)
        dif_ref[:, o * W:(o + 1) * W] = (hf - hb).astype(dif_ref.dtype)


def hyena_filter_terms(L, lp):
    W = HY_WIDTH
    tl = _pick(L, (256, 128))
    max_decay = math.log(HY_DECAY_TARGET) / HY_FAST_DECAY
    min_decay = math.log(HY_DECAY_TARGET) / HY_SLOW_DECAY
    absd = jnp.abs(jnp.linspace(min_decay, max_decay, W, dtype=F32)).reshape(1, W)
    w1 = lp["hy_w1"]
    full = lambda a: pl.BlockSpec(a.shape, lambda i: (0,) * a.ndim)
    args = (w1[0:1], w1[1:1 + HY_BANDS], w1[1 + HY_BANDS:], lp["hy_b1"].reshape(1, -1), lp["hy_w2"],
            lp["hy_b2"].reshape(1, -1), lp["hy_w3"], absd)
    out = jax.ShapeDtypeStruct((L, HY_ORDER * W), BF16)
    return pl.pallas_call(
        functools.partial(_hy_filter_kernel, L=L),
        grid=(L // tl,),
        in_specs=[full(a) for a in args],
        out_specs=[pl.BlockSpec((tl, HY_ORDER * W), lambda i: (i, 0))] * 2,
        out_shape=[out, out],
        compiler_params=_params("parallel"),
    )(*args)


def _dft_matrices(L):
    k = jnp.arange(L, dtype=jnp.int32)
    m = (k[:, None] * k[None, :]) % (2 * L)
    ang = m.astype(F32) * (math.pi / L)
    return jnp.stack([jnp.cos(ang), -jnp.sin(ang)]).astype(BF16)


def _hy_conv_kernel(v_ref, x1_ref, x2_ref, cv_ref, c1_ref, c2_ref, f_ref, ka_ref, kb_ref, ks_ref, bias_ref, o_ref,
                    y_ref, yb_ref, conv_ref):
    L = v_ref.shape[1]
    N = 2 * L
    tw = o_ref.shape[2]
    row = lax.broadcasted_iota(jnp.int32, (L, 1), 0)
    nyq = jnp.where(row % 2 == 0, 1.0, -1.0)
    fchunk = min(L, HY_FREQ_CHUNK)

    def short_conv(x_ref, c_ref):
        x = x_ref[0].astype(F32)
        prev = jnp.where(row >= 1, pltpu.roll(x, 1, 0), 0.0)
        nxt = jnp.where(row <= L - 2, pltpu.roll(x, L - 1, 0), 0.0)
        return prev * c_ref[0:1, :] + x * c_ref[1:2, :] + nxt * c_ref[2:3, :]

    y_ref[...] = short_conv(v_ref, cv_ref)
    for o, (x_ref, c_ref) in enumerate(((x1_ref, c1_ref), (x2_ref, c2_ref))):
        cols = slice(o * tw, (o + 1) * tw)
        y = y_ref[...]
        k_nyq = jnp.sum(nyq * ks_ref[:, cols].astype(F32), axis=0, keepdims=True)
        u_nyq = jnp.sum(nyq * y, axis=0, keepdims=True)
        conv_ref[...] = nyq * (u_nyq * k_nyq * (1.0 / N)) + bias_ref[o:o + 1, :] * y
        yb_ref[...] = y.astype(BF16)

        def freq_chunk(i, carry):
            f0 = pl.multiple_of(i * fchunk, fchunk)
            fr = pl.ds(f0, fchunk)
            yb = yb_ref[...]
            ka = ka_ref[fr, cols].astype(F32)
            kb = kb_ref[fr, cols].astype(F32)
            ua = jnp.dot(f_ref[0, fr, :], yb, preferred_element_type=F32)
            ub = jnp.dot(f_ref[1, fr, :], yb, preferred_element_type=F32)
            k_idx = f0 + lax.broadcasted_iota(jnp.int32, (fchunk, 1), 0)
            sc = jnp.where(k_idx == 0, 1.0 / N, 2.0 / N)
            ya = (sc * (ua * ka - ub * kb)).astype(BF16)
            yb2 = (sc * (ua * kb + ub * ka)).astype(BF16)
            conv_ref[...] += (jnp.dot(f_ref[0, :, fr], ya, preferred_element_type=F32)
                              + jnp.dot(f_ref[1, :, fr], yb2, preferred_element_type=F32))
            return carry

        lax.fori_loop(0, L // fchunk, freq_chunk, 0)
        y_ref[...] = short_conv(x_ref, c_ref) * conv_ref[...]
    o_ref[0] = y_ref[...].astype(o_ref.dtype)


def hyena(hy_raw, lp):
    B, L, _ = hy_raw.shape
    W = HY_WIDTH
    tw = 256
    nw = W // tw
    fsum, fdif = hyena_filter_terms(L, lp)
    F = _dft_matrices(L)
    ka = matmul(F[0], fsum, BF16)
    kb = matmul(F[1], fdif, BF16)
    regroup = lambda a: a.reshape(L, HY_ORDER, nw, tw).transpose(0, 2, 1, 3).reshape(L, nw * HY_ORDER * tw)
    ka, kb, ks = regroup(ka), regroup(kb), regroup(fsum)
    cw = lp["hy_conv"]
    xspec = lambda g: pl.BlockSpec((1, L, tw), lambda j, b: (b, 0, g * nw + j))
    cspec = lambda g: pl.BlockSpec((HY_SHORT, tw), lambda j, b: (0, g * nw + j))
    kspec = pl.BlockSpec((L, HY_ORDER * tw), lambda j, b: (0, j), pipeline_mode=pl.Buffered(1))
    return pl.pallas_call(
        _hy_conv_kernel,
        name="hy_conv",
        grid=(nw, B),
        in_specs=[xspec(0), xspec(1), xspec(2), cspec(0), cspec(1), cspec(2),
                  pl.BlockSpec((2, L, L), lambda j, b: (0, 0, 0), pipeline_mode=pl.Buffered(1)),
                  kspec, kspec, kspec,
                  pl.BlockSpec((HY_ORDER, tw), lambda j, b: (0, j))],
        out_specs=pl.BlockSpec((1, L, tw), lambda j, b: (b, 0, j)),
        out_shape=jax.ShapeDtypeStruct((B, L, W), BF16),
        scratch_shapes=[pltpu.VMEM((L, tw), F32), pltpu.VMEM((L, tw), BF16), pltpu.VMEM((L, tw), F32)],
        compiler_params=_params("parallel", "parallel"),
    )(hy_raw, hy_raw, hy_raw, cw, cw, cw, F, ka, kb, ks, lp["hy_bias"])


def _dn_prep_kernel(x_ref, w_ref, o_ref, *, n_q_tiles):
    L = x_ref.shape[1]
    tc = x_ref.shape[2]
    x = x_ref[0].astype(F32)
    row = lax.broadcasted_iota(jnp.int32, (L, 1), 0)
    half = DN_CONV // 2
    acc = x * w_ref[half:half + 1, :]
    for d in range(-half, half + 1):
        if d == 0:
            continue
        shifted = pltpu.roll(x, (-d) % L, 0)
        valid = jnp.logical_and(row + d >= 0, row + d <= L - 1)
        acc = acc + jnp.where(valid, shifted, 0.0) * w_ref[half + d:half + d + 1, :]
    y = acc * jax.nn.sigmoid(acc)
    j = pl.program_id(1)
    for g in range(tc // DN_DK):
        yg = y[:, g * DN_DK:(g + 1) * DN_DK]
        inv = lax.rsqrt(jnp.sum(yg * yg, axis=-1, keepdims=True) + EPS)
        fac = jnp.where(j < n_q_tiles, inv * DN_DK ** -0.5, jnp.where(j < 2 * n_q_tiles, inv, 1.0))
        o_ref[0, :, g * DN_DK:(g + 1) * DN_DK] = yg * fac


def dn_prep(qkv_raw, conv_w):
    B, L, CH = qkv_raw.shape
    tc = 256
    return pl.pallas_call(
        functools.partial(_dn_prep_kernel, n_q_tiles=DN_HEADS * DN_DK // tc),
        grid=(B, CH // tc),
        in_specs=[pl.BlockSpec((1, L, tc), lambda b, j: (b, 0, j)),
                  pl.BlockSpec((DN_CONV, tc), lambda b, j: (0, j))],
        out_specs=pl.BlockSpec((1, L, tc), lambda b, j: (b, 0, j)),
        out_shape=jax.ShapeDtypeStruct((B, L, CH), F32),
        compiler_params=_params("parallel", "parallel"),
    )(qkv_raw, conv_w)


def _dn_gate_kernel(ab_ref, alog_ref, dt_ref, o_ref):
    n = 2 * DN_HEADS
    a = ab_ref[:, :n] + dt_ref[...]
    softplus = jnp.maximum(a, 0.0) + jnp.log1p(jnp.exp(-jnp.abs(a)))
    o_ref[:, :n] = -jnp.exp(alog_ref[...]) * softplus
    o_ref[:, n:] = jax.nn.sigmoid(ab_ref[:, n:])


def dn_gates(p_ab, a_log, dt_bias):
    T, n2 = p_ab.shape
    tm = _pick(T, (2048, 1024, 512, 256, 128))
    n = 2 * DN_HEADS
    return pl.pallas_call(
        _dn_gate_kernel,
        grid=(T // tm,),
        in_specs=[pl.BlockSpec((tm, n2), lambda i: (i, 0)),
                  pl.BlockSpec((1, n), lambda i: (0, 0)),
                  pl.BlockSpec((1, n), lambda i: (0, 0))],
        out_specs=pl.BlockSpec((tm, n2), lambda i: (i, 0)),
        out_shape=jax.ShapeDtypeStruct((T, n2), F32),
        compiler_params=_params("parallel"),
    )(p_ab, a_log.reshape(1, n), dt_bias.reshape(1, n))


def _split3(x):
    hi = x.astype(BF16)
    r1 = x - hi.astype(F32)
    mid = r1.astype(BF16)
    lo = (r1 - mid.astype(F32)).astype(BF16)
    return hi, mid, lo


def _dn_chunk_kernel(qkv_ref, gcol_ref, grow_ref, s0_ref, o_ref, sout_ref, s_ref):
    d = pl.program_id(1)
    n = pl.program_id(2)
    C = DN_CHUNK
    H = DN_HEADS
    nt = (((1,), (1,)), ((), ()))

    @pl.when(n == 0)
    def _():
        s_ref[...] = s0_ref[0, 0]

    r = lax.broadcasted_iota(jnp.int32, (C, C), 0)
    c = lax.broadcasted_iota(jnp.int32, (C, C), 1)
    fwd = d == 0
    lag = (r - c) * jnp.where(fwd, 1, -1)
    incl = lag >= 0
    strict = lag > 0
    eye = (r == c).astype(F32)
    tri = jnp.where(incl, 1.0, 0.0).astype(BF16)
    g_col = gcol_ref[0, 0]
    g_row = grow_ref[0, 0, 0]
    gc_col = sum(jnp.dot(tri, p, preferred_element_type=F32) for p in _split3(g_col))
    gc_row = sum(lax.dot_general(p, tri, nt, preferred_element_type=F32) for p in _split3(g_row))
    g_tot = jnp.where(fwd, gc_row[:, C - 1:C], gc_row[:, 0:1])

    def mm(x, y, dims=None):
        x, y = x.astype(BF16), y.astype(BF16)
        if dims is None:
            return jnp.dot(x, y, preferred_element_type=F32)
        return lax.dot_general(x, y, dims, preferred_element_type=F32)

    hs = range(H)
    q = [qkv_ref[0, :, h * DN_DK:(h + 1) * DN_DK] for h in hs]
    k = [qkv_ref[0, :, (H + h) * DN_DK:(H + h + 1) * DN_DK] for h in hs]
    v = [qkv_ref[0, :, 2 * H * DN_DK + h * DN_DV:2 * H * DN_DK + (h + 1) * DN_DV] for h in hs]
    gc = [gc_col[:, h:h + 1] for h in hs]
    beta = [g_col[:, H + h:H + h + 1] for h in hs]
    g_last = [g_tot[h:h + 1, :] for h in hs]
    eg = [jnp.exp(gc[h]) for h in hs]
    decay = [jnp.exp(jnp.where(incl, gc[h] - gc_row[h:h + 1, :], NEG_BIG)) for h in hs]
    kk = [mm(k[h], k[h], nt) for h in hs]
    qk = [mm(q[h], k[h], nt) for h in hs]
    a = [jnp.where(strict, beta[h] * kk[h] * decay[h], 0.0) for h in hs]
    qk = [jnp.where(incl, qk[h] * decay[h], 0.0) for h in hs]
    inv = None
    for l in range(C.bit_length() - 1):
        couple = jnp.logical_and((r >> (l + 1)) == (c >> (l + 1)), (r >> l) != (c >> l))
        a_l = [jnp.where(couple, a[h], 0.0) for h in hs]
        if inv is None:
            inv = [eye - a_l[h] for h in hs]
        else:
            t = [mm(inv[h], a_l[h]) for h in hs]
            inv = [inv[h] - mm(t[h], inv[h]) for h in hs]
    rhs = [jnp.concatenate([v[h] * beta[h], k[h] * (beta[h] * eg[h])], axis=1) for h in hs]
    sol = [mm(inv[h], rhs[h]) for h in hs]
    S = [s_ref[h] for h in hs]
    v_new = [sol[h][:, :DN_DV] - mm(sol[h][:, DN_DV:], S[h]) for h in hs]
    o = [mm(q[h] * eg[h], S[h]) + mm(qk[h], v_new[h]) for h in hs]
    upd = [mm(k[h] * jnp.exp(g_last[h] - gc[h]), v_new[h], (((0,), (0,)), ((), ()))) for h in hs]
    for h in hs:
        o_ref[0, 0, :, h * DN_DV:(h + 1) * DN_DV] = o[h]
        s_ref[h] = S[h] * jnp.exp(g_last[h]) + upd[h]

    @pl.when(n == pl.num_programs(2) - 1)
    def _():
        sout_ref[0, 0] = s_ref[...]


def dn_scan(qkv, gates, s0):
    B, L, CH = qkv.shape
    C, H = DN_CHUNK, DN_HEADS
    N = L // C
    g4 = gates.reshape(B, L, 2, 2, H)
    gcol = g4.transpose(0, 3, 1, 2, 4).reshape(B, 2, L, 2 * H)
    grow = gcol.reshape(B, 2, N, C, 2 * H).transpose(0, 1, 2, 4, 3)
    ne = lambda d, n: n + d * (N - 1 - 2 * n)
    return pl.pallas_call(
        _dn_chunk_kernel,
        name="dn_chunk",
        grid=(B, 2, N),
        in_specs=[pl.BlockSpec((1, C, CH), lambda b, d, n: (b, ne(d, n), 0)),
                  pl.BlockSpec((1, 1, C, 2 * H), lambda b, d, n: (b, d, ne(d, n), 0)),
                  pl.BlockSpec((1, 1, 1, 2 * H, C), lambda b, d, n: (b, d, ne(d, n), 0, 0)),
                  pl.BlockSpec((1, 1, H, DN_DK, DN_DV), lambda b, d, n: (b, d, 0, 0, 0))],
        out_specs=[pl.BlockSpec((1, 1, C, H * DN_DV), lambda b, d, n: (b, d, ne(d, n), 0)),
                   pl.BlockSpec((1, 1, H, DN_DK, DN_DV), lambda b, d, n: (b, d, 0, 0, 0))],
        out_shape=[jax.ShapeDtypeStruct((B, 2, L, H * DN_DV), F32),
                   jax.ShapeDtypeStruct((B, 2, H, DN_DK, DN_DV), F32)],
        scratch_shapes=[pltpu.VMEM((H, DN_DK, DN_DV), F32)],
        compiler_params=_params("parallel", "arbitrary", "arbitrary"),
    )(qkv, gcol, grow, s0)


def _dn_out_kernel(of_ref, ob_ref, z_ref, g_ref, o_ref):
    for h in range(DN_HEADS):
        cols = slice(h * DN_DV, (h + 1) * DN_DV)
        o = of_ref[0, 0, :, cols] + ob_ref[0, 0, :, cols]
        z = z_ref[0, :, cols].astype(F32)
        o_ref[0, :, cols] = (_rms(o, g_ref[...]) * (z * jax.nn.sigmoid(z))).astype(o_ref.dtype)


def dn_output(o2, z, g):
    B, _, L, W = o2.shape
    tm = _pick(L, (512, 256, 128))
    return pl.pallas_call(
        _dn_out_kernel,
        grid=(B, L // tm),
        in_specs=[pl.BlockSpec((1, 1, tm, W), lambda b, i: (b, 0, i, 0)),
                  pl.BlockSpec((1, 1, tm, W), lambda b, i: (b, 1, i, 0)),
                  pl.BlockSpec((1, tm, W), lambda b, i: (b, i, 0)),
                  pl.BlockSpec((1, DN_DV), lambda b, i: (0, 0))],
        out_specs=pl.BlockSpec((1, tm, W), lambda b, i: (b, i, 0)),
        out_shape=jax.ShapeDtypeStruct((B, L, W), BF16),
        compiler_params=_params("parallel", "parallel"),
    )(o2, o2, z, g.reshape(1, DN_DV))


def gated_deltanet(qkv_raw, z, gates, lp, s0):
    qkv = dn_prep(qkv_raw, lp["dn_conv"])
    o2, s_fin = dn_scan(qkv, gates, s0)
    return dn_output(o2, z, lp["dn_out_norm"]), s_fin


def _axial_rope(x):
    L = x.shape[1]
    half = x.shape[-1] // 2
    t = jnp.arange(L)
    inv = ROPE_THETA ** (-jnp.arange(0, half, 2, dtype=F32) / half)
    out = []
    for pos, xa in ((t // GRID_W, x[..., :half]), (t % GRID_W, x[..., half:])):
        ang = pos.astype(F32)[:, None] * inv[None, :]
        cos, sin = jnp.cos(ang)[:, None, :], jnp.sin(ang)[:, None, :]
        x1, x2 = xa[..., : half // 2], xa[..., half // 2:]
        out += [x1 * cos - x2 * sin, x2 * cos + x1 * sin]
    return jnp.concatenate(out, axis=-1)


def _pad_heads(nope, rope):
    B, L, H, _ = nope.shape
    rope = jnp.broadcast_to(rope, (B, L, H, MLA_ROPE))
    pad = jnp.zeros((B, L, H, MLA_QK_PAD - MLA_NOPE - MLA_ROPE), nope.dtype)
    return jnp.concatenate([nope, rope, pad], axis=-1).reshape(B, L, H * MLA_QK_PAD).astype(BF16)


def _layer(x, lp, mods, dims, caches):
    Bc, Lc, Bl, Ll = dims
    Tc, Tl = Bc * Lc, Bl * Ll
    T = Tc + Tl
    tm = _pick(math.gcd(Tc, Ll), (1024, 512, 256, 128))

    def make_cidx(tile):
        nct, tpl = Tc // tile, Ll // tile
        return lambda i: jnp.where(i < nct, 0, 1 + (i - nct) // tpl)

    cidx = make_cidx(tm)
    h = modulate(x, lp["norm1_g"], mods, cidx, tm, 0, 1)

    w_in = lp["w_in"]
    cuts = np.cumsum((0,) + IN_SIZES)
    col = lambda a, b: w_in[:, cuts[a]:cuts[b]].astype(BF16)
    p_mla = matmul(h, col(0, 3))
    p_dn = matmul(h, col(3, 4), BF16)
    p_z = matmul(h, col(4, 5), BF16)
    p_ab = matmul(h, col(5, 7))
    p_hy = matmul(h, col(7, 8), BF16)
    p_na = matmul(h, col(8, 9))
    p_gate = matmul(h, col(9, 10), BF16)

    cq, ckv, krope = p_mla[:, :MLA_Q_RANK], p_mla[:, MLA_Q_RANK:MLA_Q_RANK + MLA_KV_RANK], p_mla[:, -MLA_ROPE:]

    cq_n = rmsnorm(cq, lp["mla_q_norm"], BF16)
    ckv_n = rmsnorm(ckv, lp["mla_kv_norm"], F32)
    q_all = matmul(cq_n, lp["mla_w_qb"].astype(BF16))
    w_kvb = lp["mla_w_kvb"].astype(BF16)
    kv_all = matmul(ckv_n.astype(BF16), w_kvb, BF16)
    ckv_ctx, krope_ctx, nak_ctx, nav_ctx, s_f0, s_b0 = caches
    kv_cache = matmul(ckv_ctx.reshape(-1, MLA_KV_RANK).astype(BF16), w_kvb, BF16)
    Lp = ckv_ctx.shape[1]

    def heads_q(t, B, L, rotate):
        t = t.reshape(B, L, MLA_HEADS, MLA_NOPE + MLA_ROPE)
        rope = t[..., MLA_NOPE:]
        if rotate:
            rope = _axial_rope(rope)
        return _pad_heads(t[..., :MLA_NOPE], rope)

    def heads_kv(kv, kr, B, L):
        kv = kv.reshape(B, L, MLA_HEADS, MLA_NOPE + MLA_V)
        return _pad_heads(kv[..., :MLA_NOPE], kr.reshape(B, L, 1, MLA_ROPE)), kv.reshape(B, L, -1)

    qc = heads_q(q_all[:Tc], Bc, Lc, False)
    kc, kvc = heads_kv(kv_all[:Tc], krope[:Tc], Bc, Lc)
    o_a_ctx = attention(qc, kc, kvc, MLA_HEADS, MLA_QK_PAD, MLA_V, MLA_SCALE, v_off=1, v_stride=2)
    ql = heads_q(q_all[Tc:], Bl, Ll, True)
    kl, kvl = heads_kv(kv_all[Tc:], _axial_rope(krope[Tc:].reshape(Bl, Ll, 1, MLA_ROPE)), Bl, Ll)
    kp, kvp = heads_kv(kv_cache, krope_ctx, Bl, Lp)
    o_a_lat = attention(ql, jnp.concatenate([kl, kp], axis=1), jnp.concatenate([kvl, kvp], axis=1),
                        MLA_HEADS, MLA_QK_PAD, MLA_V, MLA_SCALE, v_off=1, v_stride=2)
    o_a = jnp.concatenate([o_a_ctx.reshape(Tc, BRANCH_W), o_a_lat.reshape(Tl, BRANCH_W)])

    gates = dn_gates(p_ab, lp["dn_a_log"], lp["dn_dt_bias"])
    zero_state = jnp.zeros((Bc, 2, DN_HEADS, DN_DK, DN_DV), F32)
    o_b_ctx, s_ctx = gated_deltanet(p_dn[:Tc].reshape(Bc, Lc, -1), p_z[:Tc].reshape(Bc, Lc, -1),
                                    gates[:Tc].reshape(Bc, Lc, -1), lp, zero_state)
    o_b_lat, _ = gated_deltanet(p_dn[Tc:].reshape(Bl, Ll, -1), p_z[Tc:].reshape(Bl, Ll, -1),
                                gates[Tc:].reshape(Bl, Ll, -1), lp, jnp.stack([s_f0, s_b0], axis=1))
    s_f, s_b = s_ctx[:, 0], s_ctx[:, 1]
    o_b = jnp.concatenate([o_b_ctx.reshape(Tc, BRANCH_W), o_b_lat.reshape(Tl, BRANCH_W)])

    o_c = jnp.concatenate([hyena(p_hy[:Tc].reshape(Bc, Lc, -1), lp).reshape(Tc, BRANCH_W),
                           hyena(p_hy[Tc:].reshape(Bl, Ll, -1), lp).reshape(Tl, BRANCH_W)])

    na_bf = p_na.astype(BF16)
    na_ctx = na_bf[:Tc].reshape(Bc, Lc, -1)
    o_d_ctx = attention(na_ctx, na_ctx, na_ctx, NA_HEADS, NA_HD, NA_HD, NA_SCALE, k_off=NA_HEADS, v_off=2 * NA_HEADS)
    o_d_lat = neighbourhood_attention(na_bf[Tc:].reshape(Bl, Ll, -1), nak_ctx.reshape(Bl, Lp, -1).astype(BF16),
                                      nav_ctx.reshape(Bl, Lp, -1).astype(BF16), lp["na_rpb"])
    o_d = jnp.concatenate([o_d_ctx.reshape(Tc, BRANCH_W), o_d_lat.reshape(Tl, BRANCH_W)])

    merged = merge_branches((o_a, o_b, o_c, o_d), p_gate, lp["w_branch"].astype(BF16), tm)
    x = matmul_gated_residual(merged, lp["w_out"].astype(BF16), x, mods, cidx, tm, 2)

    h2 = modulate(x, lp["norm2_g"], mods, cidx, tm, 3, 4)
    pq = matmul(h2, lp["peer_wq"].astype(BF16))
    route = peer_route(pq, lp["peer_keys"])
    tt = _pick(math.gcd(Tc, Ll), (512, 256, 128))
    x = peer_dense(h2, lp["peer_u"].astype(BF16), lp["peer_v"].astype(BF16), route, x, mods, make_cidx(tt), tt, 5)

    ctx_out = (ckv_n[:Tc].reshape(Bc, Lc, MLA_KV_RANK), krope[:Tc].reshape(Bc, Lc, MLA_ROPE),
               p_na[:Tc, BRANCH_W:2 * BRANCH_W].reshape(Bc, Lc, NA_HEADS, NA_HD),
               p_na[:Tc, 2 * BRANCH_W:].reshape(Bc, Lc, NA_HEADS, NA_HD), s_f, s_b)
    return x, ctx_out


def kernel(x_prompt, x_sample, cache_mla_ckv, cache_mla_krope, cache_na_k, cache_na_v, state_dn_fwd, state_dn_bwd, c, c_ctx, norm1_g, w_ada, b_ada, w_in, mla_q_norm, mla_w_qb, mla_kv_norm, mla_w_kvb, dn_conv, dn_a_log, dn_dt_bias, dn_out_norm, hy_conv, hy_w1, hy_b1, hy_w2, hy_b2, hy_w3, hy_bias, na_rpb, w_branch, w_out, norm2_g, peer_wq, peer_keys, peer_u, peer_v, final_g):
    Bc, Lc, D = x_prompt.shape
    Bl, Ll, _ = x_sample.shape
    depth = w_in.shape[0]
    Tc = Bc * Lc
    x = jnp.concatenate([x_prompt.reshape(Tc, D), x_sample.reshape(Bl * Ll, D)])
    n_cond = 1 + Bl
    cond_rows = -(-n_cond // SUBLANES) * SUBLANES
    cond = jnp.concatenate([c_ctx[None], c, jnp.zeros((cond_rows - n_cond, D), F32)])

    names = ("norm1_g", "w_in", "mla_q_norm", "mla_w_qb", "mla_kv_norm", "mla_w_kvb", "dn_conv", "dn_a_log",
             "dn_dt_bias", "dn_out_norm", "hy_conv", "hy_w1", "hy_b1", "hy_w2", "hy_b2", "hy_w3", "hy_bias",
             "na_rpb", "w_branch", "w_out", "norm2_g", "peer_wq", "peer_keys", "peer_u", "peer_v")
    vals = (norm1_g, w_in, mla_q_norm, mla_w_qb, mla_kv_norm, mla_w_kvb, dn_conv, dn_a_log, dn_dt_bias,
            dn_out_norm, hy_conv, hy_w1, hy_b1, hy_w2, hy_b2, hy_w3, hy_bias, na_rpb, w_branch, w_out, norm2_g,
            peer_wq, peer_keys, peer_u, peer_v)
    ctx = []
    for l in range(depth):
        lp = {n: v[l] for n, v in zip(names, vals)}
        mods = ada_modulation(cond, w_ada[l], b_ada[l]).reshape(cond_rows * 6, 1, D)
        caches = (cache_mla_ckv[:, l], cache_mla_krope[:, l], cache_na_k[:, l], cache_na_v[:, l],
                  state_dn_fwd[:, l], state_dn_bwd[:, l])
        x, ctx_out = _layer(x, lp, mods, (Bc, Lc, Bl, Ll), caches)
        ctx.append(ctx_out)
    y = rmsnorm(x, final_g)
    stack = lambda k: jnp.stack([t[k] for t in ctx], axis=1)
    return (y[:Tc].reshape(Bc, Lc, D), y[Tc:].reshape(Bl, Ll, D), stack(0), stack(1), stack(2), stack(3),
            stack(4), stack(5))
```

```python
import functools
import math

import numpy as np
import jax
import jax.numpy as jnp
from jax import lax
from jax.experimental import pallas as pl
from jax.experimental.pallas import tpu as pltpu

F32 = jnp.float32
BF16 = jnp.bfloat16

V7X_VMEM_BYTES = 64 * 1024 * 1024
VMEM_LIMIT = V7X_VMEM_BYTES - 8 * 1024 * 1024
LANES = 128
SUBLANES = 8

D_MODEL = 2048
GRID_W = 64
EPS = 1e-6
N_BRANCH = 4
BRANCH_W = D_MODEL // 2

MLA_HEADS = 8
MLA_NOPE = 128
MLA_ROPE = 64
MLA_V = BRANCH_W // MLA_HEADS
MLA_Q_RANK = D_MODEL // 4
MLA_KV_RANK = D_MODEL // 8
MLA_SCALE = (MLA_NOPE + MLA_ROPE) ** -0.5
MLA_QK_PAD = 256
ROPE_THETA = 10000.0

DN_HEADS = 8
DN_DK = 128
DN_DV = BRANCH_W // DN_HEADS
DN_CONV = 5
DN_CHUNK = 64
DN_CONV_CH = 2 * DN_HEADS * DN_DK + DN_HEADS * DN_DV

HY_WIDTH = BRANCH_W
HY_ORDER = 2
HY_SHORT = 3
HY_EMB = 33
HY_FAST_DECAY = 0.3
HY_SLOW_DECAY = 1.5
HY_DECAY_TARGET = 1e-2

NA_HEADS = 8
NA_HD = BRANCH_W // NA_HEADS
NA_KH = 8
NA_KW = 16
NA_SCALE = NA_HD ** -0.5
NA_QROWS = 4
NA_REGION = NA_KH + NA_QROWS
NEG_BIG = -1e30

PEER_HEADS = 8
PEER_NKEYS = 128
PEER_N = PEER_NKEYS * PEER_NKEYS
PEER_DKEY = 256
PEER_TOPK = 16

IN_SIZES = (MLA_Q_RANK, MLA_KV_RANK, MLA_ROPE, DN_CONV_CH, DN_HEADS * DN_DV, 2 * DN_HEADS, 2 * DN_HEADS,
            (HY_ORDER + 1) * HY_WIDTH, 3 * NA_HEADS * NA_HD, N_BRANCH * D_MODEL)


def _params(*sem):
    return pltpu.CompilerParams(dimension_semantics=sem, vmem_limit_bytes=VMEM_LIMIT)


def _pick(n, prefs):
    for p in prefs:
        if n % p == 0:
            return p
    return n


def _mm_kernel(x_ref, w_ref, o_ref):
    o_ref[...] = jnp.dot(x_ref[...], w_ref[...], preferred_element_type=F32).astype(o_ref.dtype)


def matmul(x, w, out_dtype=F32, tm=None, tn=None, rows=None):
    M, K = x.shape
    M = rows or M
    N = w.shape[1]
    tm = tm or _pick(M, (1024, 512, 256, 128))
    tn = tn or _pick(N, (512, 256, 128))
    return pl.pallas_call(
        _mm_kernel,
        grid=(M // tm, N // tn),
        in_specs=[pl.BlockSpec((tm, K), lambda i, j: (i, 0)),
                  pl.BlockSpec((K, tn), lambda i, j: (0, j))],
        out_specs=pl.BlockSpec((tm, tn), lambda i, j: (i, j)),
        out_shape=jax.ShapeDtypeStruct((M, N), out_dtype),
        compiler_params=_params("parallel", "parallel"),
    )(x, w)


def _ada_kernel(c_ref, w_ref, b_ref, o_ref):
    c = c_ref[...]
    o_ref[...] = jnp.dot(c * jax.nn.sigmoid(c), w_ref[...], preferred_element_type=F32) + b_ref[...]


def ada_modulation(cond, w_ada, b_ada):
    R, K = cond.shape
    N = w_ada.shape[1]
    tn = _pick(N, (1536, 1024, 512))
    return pl.pallas_call(
        _ada_kernel,
        grid=(N // tn,),
        in_specs=[pl.BlockSpec((R, K), lambda j: (0, 0)),
                  pl.BlockSpec((K, tn), lambda j: (0, j)),
                  pl.BlockSpec((1, tn), lambda j: (0, j))],
        out_specs=pl.BlockSpec((R, tn), lambda j: (0, j)),
        out_shape=jax.ShapeDtypeStruct((R, N), F32),
        compiler_params=_params("parallel"),
    )(cond, w_ada, b_ada.reshape(1, N))


def _rms(x, g):
    return x * lax.rsqrt(jnp.mean(x * x, axis=-1, keepdims=True) + EPS) * g


def _modulate_kernel(x_ref, g_ref, sh_ref, sc_ref, o_ref):
    y = _rms(x_ref[...], g_ref[...])
    o_ref[...] = (y * (1.0 + sc_ref[0]) + sh_ref[0]).astype(o_ref.dtype)


def modulate(x, g, mods, cidx, tm, shift_slot, scale_slot):
    T, D = x.shape
    return pl.pallas_call(
        _modulate_kernel,
        grid=(T // tm,),
        in_specs=[pl.BlockSpec((tm, D), lambda i: (i, 0)),
                  pl.BlockSpec((1, D), lambda i: (0, 0)),
                  pl.BlockSpec((1, 1, D), lambda i: (cidx(i) * 6 + shift_slot, 0, 0)),
                  pl.BlockSpec((1, 1, D), lambda i: (cidx(i) * 6 + scale_slot, 0, 0))],
        out_specs=pl.BlockSpec((tm, D), lambda i: (i, 0)),
        out_shape=jax.ShapeDtypeStruct((T, D), BF16),
        compiler_params=_params("parallel"),
    )(x, g.reshape(1, D), mods, mods)


def _rmsnorm_kernel(x_ref, g_ref, o_ref):
    o_ref[...] = _rms(x_ref[...].astype(F32), g_ref[...]).astype(o_ref.dtype)


def rmsnorm(x, g, out_dtype=F32):
    T, D = x.shape
    tm = _pick(T, (1024, 512, 256, 128))
    return pl.pallas_call(
        _rmsnorm_kernel,
        grid=(T // tm,),
        in_specs=[pl.BlockSpec((tm, D), lambda i: (i, 0)),
                  pl.BlockSpec((1, D), lambda i: (0, 0))],
        out_specs=pl.BlockSpec((tm, D), lambda i: (i, 0)),
        out_shape=jax.ShapeDtypeStruct((T, D), out_dtype),
        compiler_params=_params("parallel"),
    )(x, g.reshape(1, D))


def _merge_kernel(a_ref, b_ref, c_ref, d_ref, ga_ref, gb_ref, gc_ref, gd_ref, w_ref, o_ref):
    acc = None
    for n, (br, gl) in enumerate(((a_ref, ga_ref), (b_ref, gb_ref), (c_ref, gc_ref), (d_ref, gd_ref))):
        proj = jnp.dot(br[...], w_ref[n], preferred_element_type=F32)
        term = jax.nn.sigmoid(gl[...].astype(F32)) * proj
        acc = term if acc is None else acc + term
    o_ref[...] = acc.astype(o_ref.dtype)


def merge_branches(branches, gate_logits, w_branch, tm):
    T = branches[0].shape[0]
    tn = 512
    br_spec = pl.BlockSpec((tm, BRANCH_W), lambda i, j: (i, 0))
    nj = D_MODEL // tn
    gl_specs = [pl.BlockSpec((tm, tn), functools.partial(lambda i, j, n: (i, n * nj + j), n=n))
                for n in range(N_BRANCH)]
    return pl.pallas_call(
        _merge_kernel,
        grid=(T // tm, D_MODEL // tn),
        in_specs=[br_spec, br_spec, br_spec, br_spec, *gl_specs,
                  pl.BlockSpec((N_BRANCH, BRANCH_W, tn), lambda i, j: (0, 0, j))],
        out_specs=pl.BlockSpec((tm, tn), lambda i, j: (i, j)),
        out_shape=jax.ShapeDtypeStruct((T, D_MODEL), BF16),
        compiler_params=_params("parallel", "parallel"),
    )(*branches, gate_logits, gate_logits, gate_logits, gate_logits, w_branch)


def _mm_resid_kernel(m_ref, w_ref, x_ref, g_ref, o_ref):
    o_ref[...] = x_ref[...] + g_ref[0] * jnp.dot(m_ref[...], w_ref[...], preferred_element_type=F32)


def matmul_gated_residual(m, w, x, mods, cidx, tm, gate_slot):
    T, K = m.shape
    N = w.shape[1]
    tn = 512
    return pl.pallas_call(
        _mm_resid_kernel,
        grid=(T // tm, N // tn),
        in_specs=[pl.BlockSpec((tm, K), lambda i, j: (i, 0)),
                  pl.BlockSpec((K, tn), lambda i, j: (0, j)),
                  pl.BlockSpec((tm, tn), lambda i, j: (i, j)),
                  pl.BlockSpec((1, 1, tn), lambda i, j: (cidx(i) * 6 + gate_slot, 0, j))],
        out_specs=pl.BlockSpec((tm, tn), lambda i, j: (i, j)),
        out_shape=jax.ShapeDtypeStruct((T, N), F32),
        compiler_params=_params("parallel", "parallel"),
    )(m, w, x, mods)


def _attn_kernel(q_ref, k_ref, v_ref, o_ref, *, scale):
    s = lax.dot_general(q_ref[0], k_ref[0], (((1,), (1,)), ((), ())), preferred_element_type=F32) * scale
    p = jnp.exp(s - jnp.max(s, axis=-1, keepdims=True))
    l = jnp.sum(p, axis=-1, keepdims=True)
    o = jnp.dot(p.astype(BF16), v_ref[0], preferred_element_type=F32)
    o_ref[0] = (o / l).astype(o_ref.dtype)


def attention(q, k, v, heads, dqk, dv, scale, q_off=0, k_off=0, v_off=0, v_stride=1, k_stride=1, batch=None,
              b_off=0):
    B, Lq, _ = q.shape
    B = batch or B
    Lk = k.shape[1]
    tq = _pick(Lq, (512, 256, 128))
    return pl.pallas_call(
        functools.partial(_attn_kernel, scale=scale),
        grid=(B, heads, Lq // tq),
        in_specs=[pl.BlockSpec((1, tq, dqk), lambda b, h, i: (b + b_off, i, q_off + h)),
                  pl.BlockSpec((1, Lk, dqk), lambda b, h, i: (b + b_off, 0, k_off + h * k_stride)),
                  pl.BlockSpec((1, Lk, dv), lambda b, h, i: (b + b_off, 0, v_off + h * v_stride))],
        out_specs=pl.BlockSpec((1, tq, dv), lambda b, h, i: (b, i, h)),
        out_shape=jax.ShapeDtypeStruct((B, Lq, heads * dv), BF16),
        compiler_params=_params("parallel", "parallel", "parallel"),
    )(q, k, v)


def _na_kernel(q_ref, k_ref, v_ref, kc_ref, vc_ref, bias_ref, o_ref, *, rows):
    rb = pl.program_id(2)
    nkeys = NA_REGION * GRID_W
    start = jnp.clip(rb * NA_QROWS - NA_KH // 2, 0, rows - NA_REGION) * GRID_W
    start = pl.multiple_of(start, GRID_W)
    q = q_ref[0]
    kr = k_ref[0, pl.ds(start, nkeys), :]
    vr = v_ref[0, pl.ds(start, nkeys), :]
    nt = (((1,), (1,)), ((), ()))
    s_win = lax.dot_general(q, kr, nt, preferred_element_type=F32) * NA_SCALE + bias_ref[0, 0]
    s_ctx = lax.dot_general(q, kc_ref[0], nt, preferred_element_type=F32) * NA_SCALE
    m = jnp.maximum(jnp.max(s_win, axis=-1, keepdims=True), jnp.max(s_ctx, axis=-1, keepdims=True))
    p_win = jnp.exp(s_win - m)
    p_ctx = jnp.exp(s_ctx - m)
    l = jnp.sum(p_win, axis=-1, keepdims=True) + jnp.sum(p_ctx, axis=-1, keepdims=True)
    o = (jnp.dot(p_win.astype(BF16), vr, preferred_element_type=F32)
         + jnp.dot(p_ctx.astype(BF16), vc_ref[0], preferred_element_type=F32))
    o_ref[0] = (o / l).astype(o_ref.dtype)


def _na_bias_tables(rpb, rows):
    H = rpb.shape[0]
    n_dr, n_dc = 2 * NA_KH - 1, 2 * NA_KW - 1
    nblk = rows // NA_QROWS
    span = 2 * GRID_W
    lo = GRID_W - NA_KW
    v = jnp.pad(rpb.astype(F32), ((0, 0), (0, 0), (lo, span - lo - n_dc)), constant_values=NEG_BIG)
    skew = jnp.tile(v, (1, 1, GRID_W))[..., :GRID_W * (span - 1)].reshape(H, n_dr, GRID_W, span - 1)
    band = skew[..., GRID_W - 1:]
    c = np.arange(GRID_W)[:, None]
    kc = np.arange(GRID_W)[None, :]
    col0 = np.clip(c - NA_KW // 2, 0, GRID_W - NA_KW)
    col_ok = (kc >= col0) & (kc < col0 + NA_KW)
    band = jnp.where(col_ok, band, NEG_BIG)
    band = jnp.concatenate([band, jnp.full((H, 1, GRID_W, GRID_W), NEG_BIG, F32)], axis=1)
    tiles = []
    for rb in (0, 1, nblk - 1):
        start = int(np.clip(rb * NA_QROWS - NA_KH // 2, 0, rows - NA_REGION))
        for qr in range(NA_QROWS):
            r = rb * NA_QROWS + qr
            row0 = int(np.clip(r - NA_KH // 2, 0, rows - NA_KH))
            for j in range(NA_REGION):
                kr = start + j
                tiles.append(kr - r + NA_KH - 1 if row0 <= kr < row0 + NA_KH else n_dr)
    t = jnp.stack([band[:, s] for s in tiles], axis=1).reshape(H, 3, NA_QROWS, NA_REGION, GRID_W, GRID_W)
    return t.transpose(0, 1, 2, 4, 3, 5).reshape(H, 3, NA_QROWS * GRID_W, NA_REGION * GRID_W)


def neighbourhood_attention(na_in, B, b_off, k_ctx, v_ctx, rpb):
    _, L, _ = na_in.shape
    Lc = k_ctx.shape[1]
    rows = L // GRID_W
    assert rows >= NA_REGION + NA_QROWS and rows % NA_QROWS == 0
    nblk = rows // NA_QROWS
    tq = NA_QROWS * GRID_W
    nkeys = NA_REGION * GRID_W
    bias = _na_bias_tables(rpb, rows)
    kind = lambda r: jnp.where(r == 0, 0, jnp.where(r == nblk - 1, 2, 1))
    H = NA_HEADS
    return pl.pallas_call(
        functools.partial(_na_kernel, rows=rows),
        grid=(B, H, nblk),
        in_specs=[pl.BlockSpec((1, tq, NA_HD), lambda b, h, r: (b + b_off, r, h)),
                  pl.BlockSpec((1, L, NA_HD), lambda b, h, r: (b + b_off, 0, H + h)),
                  pl.BlockSpec((1, L, NA_HD), lambda b, h, r: (b + b_off, 0, 2 * H + h)),
                  pl.BlockSpec((1, Lc, NA_HD), lambda b, h, r: (b, 0, h)),
                  pl.BlockSpec((1, Lc, NA_HD), lambda b, h, r: (b, 0, h)),
                  pl.BlockSpec((1, 1, tq, nkeys), lambda b, h, r: (h, kind(r), 0, 0))],
        out_specs=pl.BlockSpec((1, tq, NA_HD), lambda b, h, r: (b, r, h)),
        out_shape=jax.ShapeDtypeStruct((B, L, H * NA_HD), BF16),
        compiler_params=_params("parallel", "parallel", "parallel"),
    )(na_in, na_in, na_in, k_ctx, v_ctx, bias)


def _top_rows(work, k):
    out = []
    for _ in range(k):
        mx = jnp.max(work, axis=0, keepdims=True)
        out.append(mx)
        work = jnp.where(work == mx, -jnp.inf, work)
    return out


_PEER_PAIRS = [(i, k) for i in range(PEER_TOPK) for k in range(PEER_TOPK) if (i + 1) * (k + 1) <= PEER_TOPK]


def _peer_route_kernel(q_ref, keys_ref, s1_ref, s2_ref, e1_ref, e2_ref, th_ref):
    half = PEER_DKEY // 2
    q = q_ref[...]
    nt = (((1,), (1,)), ((), ()))
    s1 = lax.dot_general(keys_ref[0], q[:, :half], nt, preferred_element_type=F32)
    s2 = lax.dot_general(keys_ref[1], q[:, half:], nt, preferred_element_type=F32)
    t1 = _top_rows(s1, PEER_TOPK)
    t2 = _top_rows(s2, PEER_TOPK)
    cand = jnp.concatenate([t1[i] + t2[k] for i, k in _PEER_PAIRS], axis=0)
    theta = _top_rows(cand, PEER_TOPK)[-1]
    top = t1[0] + t2[0]
    z = jnp.sum(jnp.where(cand >= theta, jnp.exp(cand - top), 0.0), axis=0, keepdims=True)
    s1_ref[0] = s1
    s2_ref[0] = s2
    e1_ref[0] = jnp.exp(s1 - t1[0]) / z
    e2_ref[0] = jnp.exp(s2 - t2[0])
    th_ref[0] = theta


def peer_route(q, keys):
    T = q.shape[0]
    tt = _pick(T, (512, 256, 128))
    H = PEER_HEADS
    big = jax.ShapeDtypeStruct((H, PEER_NKEYS, T), F32)
    big_spec = pl.BlockSpec((1, PEER_NKEYS, tt), lambda i, h: (h, 0, i))
    return pl.pallas_call(
        _peer_route_kernel,
        name="peer_route",
        grid=(T // tt, H),
        in_specs=[pl.BlockSpec((tt, PEER_DKEY), lambda i, h: (i, h)),
                  pl.BlockSpec((2, PEER_NKEYS, PEER_DKEY // 2), lambda i, h: (0, 0, 0))],
        out_specs=[big_spec, big_spec, big_spec, big_spec,
                   pl.BlockSpec((1, 1, tt), lambda i, h: (h, 0, i))],
        out_shape=[big, big, big, big, jax.ShapeDtypeStruct((H, 1, T), F32)],
        compiler_params=_params("parallel", "parallel"),
    )(q, keys)


PEER_TE = 8 * PEER_NKEYS


def _peer_dense_kernel(h_ref, u_ref, v_ref, s1_ref, s2_ref, e1_ref, e2_ref, th_ref, x_ref, g_ref, o_ref,
                       xu_ref, w_ref, acc_ref):
    j = pl.program_id(1)
    tt = h_ref.shape[0]

    @pl.when(j == 0)
    def _():
        acc_ref[...] = jnp.zeros_like(acc_ref)

    xu_ref[...] = lax.dot_general(u_ref[...], h_ref[...], (((1,), (1,)), ((), ())), preferred_element_type=F32)

    for aa in range(PEER_TE // PEER_NKEYS):
        rows = slice(aa * PEER_NKEYS, (aa + 1) * PEER_NKEYS)
        for tc in range(tt // LANES):
            cols = slice(tc * LANES, (tc + 1) * LANES)
            gate = jnp.zeros((PEER_NKEYS, LANES), F32)
            for h in range(PEER_HEADS):
                s1 = s1_ref[h, aa:aa + 1, cols]
                e1 = e1_ref[h, aa:aa + 1, cols]
                gate = gate + jnp.where(s1 + s2_ref[h, :, cols] >= th_ref[h, :, cols], e1 * e2_ref[h, :, cols], 0.0)
            pre = xu_ref[rows, cols]
            act = 0.5 * pre * (1.0 + lax.erf(pre * (2.0 ** -0.5)))
            w_ref[rows, cols] = (gate * act).astype(w_ref.dtype)

    acc_ref[...] += lax.dot_general(w_ref[...], v_ref[...], (((0,), (0,)), ((), ())), preferred_element_type=F32)

    @pl.when(j == pl.num_programs(1) - 1)
    def _():
        o_ref[...] = x_ref[...] + g_ref[0] * acc_ref[...]


def peer_dense(h, u, v, route, x, mods, cidx_tt, tt, gate_slot):
    T, D = h.shape
    s1, s2, e1, e2, th = route
    H = PEER_HEADS
    na = PEER_TE // PEER_NKEYS
    once = pl.Buffered(1)
    tok = pl.BlockSpec((H, PEER_NKEYS, tt), lambda i, j: (0, 0, i), pipeline_mode=once)
    sub = pl.BlockSpec((H, na, tt), lambda i, j: (0, j, i))
    return pl.pallas_call(
        _peer_dense_kernel,
        name="peer_dense",
        grid=(T // tt, PEER_N // PEER_TE),
        in_specs=[pl.BlockSpec((tt, D), lambda i, j: (i, 0), pipeline_mode=once),
                  pl.BlockSpec((PEER_TE, D), lambda i, j: (j, 0)),
                  pl.BlockSpec((PEER_TE, D), lambda i, j: (j, 0)),
                  sub, tok, sub, tok,
                  pl.BlockSpec((H, 1, tt), lambda i, j: (0, 0, i), pipeline_mode=once),
                  pl.BlockSpec((tt, D), lambda i, j: (i, 0), pipeline_mode=once),
                  pl.BlockSpec((1, 1, D), lambda i, j: (cidx_tt(i) * 6 + gate_slot, 0, 0))],
        out_specs=pl.BlockSpec((tt, D), lambda i, j: (i, 0)),
        out_shape=jax.ShapeDtypeStruct((T, D), F32),
        scratch_shapes=[pltpu.VMEM((PEER_TE, tt), F32),
                        pltpu.VMEM((PEER_TE, tt), BF16),
                        pltpu.VMEM((tt, D), F32)],
        compiler_params=_params("parallel", "arbitrary"),
    )(h, u, v, s1, s2, e1, e2, th, x, mods)


HY_HID = 64
HY_BANDS = (HY_EMB - 1) // 2
HY_FREQ_CHUNK = 512


def _hy_filter_kernel(w1t_ref, w1c_ref, w1s_ref, b1_ref, w2_ref, b2_ref, w3_ref, absd_ref, sum_ref, dif_ref, *, L):
    tl = sum_ref.shape[0]
    W = HY_WIDTH
    pos_i = pl.program_id(0) * tl + lax.broadcasted_iota(jnp.int32, (tl, 1), 0)
    pos = pos_i.astype(F32)
    t01 = pos * (1.0 / (L - 1))
    w = (2.0 * math.pi) * pos / L
    band = lax.broadcasted_iota(jnp.int32, (1, HY_BANDS), 1).astype(F32)
    f = 1e-4 + band * ((HY_BANDS - 1 - 1e-4) / (HY_BANDS - 1))
    fw = f * w
    pre = (t01 * w1t_ref[...] + jnp.dot(jnp.cos(fw), w1c_ref[...], preferred_element_type=F32)
           - jnp.dot(jnp.sin(fw), w1s_ref[...], preferred_element_type=F32) + b1_ref[...])
    h = jnp.sin(pre)
    h = jnp.sin(jnp.dot(h, w2_ref[...], preferred_element_type=F32) + b2_ref[...])
    h = jnp.dot(h, w3_ref[...], preferred_element_type=F32)
    window = jnp.exp(-t01 * absd_ref[...])
    for o in range(HY_ORDER):
        hf = h[:, (2 * o) * W:(2 * o + 1) * W] * window
        hb = jnp.where(pos_i == 0, 0.0, h[:, (2 * o + 1) * W:(2 * o + 2) * W] * window)
        sum_ref[:, o * W:(o + 1) * W] = (hf + hb).astype(sum_ref.dtype)
        dif_ref[:, o * W:(o + 1) * W] = (hf - hb).astype(dif_ref.dtype)


def hyena_filter_terms(L, lp):
    W = HY_WIDTH
    tl = _pick(L, (256, 128))
    max_decay = math.log(HY_DECAY_TARGET) / HY_FAST_DECAY
    min_decay = math.log(HY_DECAY_TARGET) / HY_SLOW_DECAY
    absd = jnp.abs(jnp.linspace(min_decay, max_decay, W, dtype=F32)).reshape(1, W)
    w1 = lp["hy_w1"]
    full = lambda a: pl.BlockSpec(a.shape, lambda i: (0,) * a.ndim)
    args = (w1[0:1], w1[1:1 + HY_BANDS], w1[1 + HY_BANDS:], lp["hy_b1"].reshape(1, -1), lp["hy_w2"],
            lp["hy_b2"].reshape(1, -1), lp["hy_w3"], absd)
    out = jax.ShapeDtypeStruct((L, HY_ORDER * W), BF16)
    return pl.pallas_call(
        functools.partial(_hy_filter_kernel, L=L),
        grid=(L // tl,),
        in_specs=[full(a) for a in args],
        out_specs=[pl.BlockSpec((tl, HY_ORDER * W), lambda i: (i, 0))] * 2,
        out_shape=[out, out],
        compiler_params=_params("parallel"),
    )(*args)


def _dft_matrices(L):
    k = jnp.arange(L, dtype=jnp.int32)
    m = (k[:, None] * k[None, :]) % (2 * L)
    ang = m.astype(F32) * (math.pi / L)
    return jnp.stack([jnp.cos(ang), -jnp.sin(ang)]).astype(BF16)


def _hy_conv_kernel(v_ref, x1_ref, x2_ref, cv_ref, c1_ref, c2_ref, f_ref, ka_ref, kb_ref, ks_ref, bias_ref, o_ref,
                    y_ref, yb_ref, conv_ref):
    L = v_ref.shape[1]
    N = 2 * L
    tw = o_ref.shape[2]
    row = lax.broadcasted_iota(jnp.int32, (L, 1), 0)
    nyq = jnp.where(row % 2 == 0, 1.0, -1.0)
    fchunk = min(L, HY_FREQ_CHUNK)

    def short_conv(x_ref, c_ref):
        x = x_ref[0].astype(F32)
        prev = jnp.where(row >= 1, pltpu.roll(x, 1, 0), 0.0)
        nxt = jnp.where(row <= L - 2, pltpu.roll(x, L - 1, 0), 0.0)
        return prev * c_ref[0:1, :] + x * c_ref[1:2, :] + nxt * c_ref[2:3, :]

    y_ref[...] = short_conv(v_ref, cv_ref)
    for o, (x_ref, c_ref) in enumerate(((x1_ref, c1_ref), (x2_ref, c2_ref))):
        cols = slice(o * tw, (o + 1) * tw)
        y = y_ref[...]
        k_nyq = jnp.sum(nyq * ks_ref[:, cols].astype(F32), axis=0, keepdims=True)
        u_nyq = jnp.sum(nyq * y, axis=0, keepdims=True)
        conv_ref[...] = nyq * (u_nyq * k_nyq * (1.0 / N)) + bias_ref[o:o + 1, :] * y
        yb_ref[...] = y.astype(BF16)

        def freq_chunk(i, carry):
            f0 = pl.multiple_of(i * fchunk, fchunk)
            fr = pl.ds(f0, fchunk)
            yb = yb_ref[...]
            ka = ka_ref[fr, cols].astype(F32)
            kb = kb_ref[fr, cols].astype(F32)
            ua = jnp.dot(f_ref[0, fr, :], yb, preferred_element_type=F32)
            ub = jnp.dot(f_ref[1, fr, :], yb, preferred_element_type=F32)
            k_idx = f0 + lax.broadcasted_iota(jnp.int32, (fchunk, 1), 0)
            sc = jnp.where(k_idx == 0, 1.0 / N, 2.0 / N)
            ya = (sc * (ua * ka - ub * kb)).astype(BF16)
            yb2 = (sc * (ua * kb + ub * ka)).astype(BF16)
            conv_ref[...] += (jnp.dot(f_ref[0, :, fr], ya, preferred_element_type=F32)
                              + jnp.dot(f_ref[1, :, fr], yb2, preferred_element_type=F32))
            return carry

        lax.fori_loop(0, L // fchunk, freq_chunk, 0)
        y_ref[...] = short_conv(x_ref, c_ref) * conv_ref[...]
    o_ref[0] = y_ref[...].astype(o_ref.dtype)


def hyena(hy_raw, B, b_off, lp):
    _, L, _ = hy_raw.shape
    W = HY_WIDTH
    tw = 256
    nw = W // tw
    fsum, fdif = hyena_filter_terms(L, lp)
    F = _dft_matrices(L)
    ka = matmul(F[0], fsum, BF16)
    kb = matmul(F[1], fdif, BF16)
    regroup = lambda a: a.reshape(L, HY_ORDER, nw, tw).transpose(0, 2, 1, 3).reshape(L, nw * HY_ORDER * tw)
    ka, kb, ks = regroup(ka), regroup(kb), regroup(fsum)
    cw = lp["hy_conv"]
    xspec = lambda g: pl.BlockSpec((1, L, tw), lambda j, b: (b + b_off, 0, g * nw + j))
    cspec = lambda g: pl.BlockSpec((HY_SHORT, tw), lambda j, b: (0, g * nw + j))
    kspec = pl.BlockSpec((L, HY_ORDER * tw), lambda j, b: (0, j), pipeline_mode=pl.Buffered(1))
    return pl.pallas_call(
        _hy_conv_kernel,
        name="hy_conv",
        grid=(nw, B),
        in_specs=[xspec(0), xspec(1), xspec(2), cspec(0), cspec(1), cspec(2),
                  pl.BlockSpec((2, L, L), lambda j, b: (0, 0, 0), pipeline_mode=pl.Buffered(1)),
                  kspec, kspec, kspec,
                  pl.BlockSpec((HY_ORDER, tw), lambda j, b: (0, j))],
        out_specs=pl.BlockSpec((1, L, tw), lambda j, b: (b, 0, j)),
        out_shape=jax.ShapeDtypeStruct((B, L, W), BF16),
        scratch_shapes=[pltpu.VMEM((L, tw), F32), pltpu.VMEM((L, tw), BF16), pltpu.VMEM((L, tw), F32)],
        compiler_params=_params("parallel", "parallel"),
    )(hy_raw, hy_raw, hy_raw, cw, cw, cw, F, ka, kb, ks, lp["hy_bias"])


def _dn_prep_kernel(x_ref, w_ref, o_ref, *, n_q_tiles):
    L = x_ref.shape[1]
    tc = x_ref.shape[2]
    x = x_ref[0].astype(F32)
    row = lax.broadcasted_iota(jnp.int32, (L, 1), 0)
    half = DN_CONV // 2
    acc = x * w_ref[half:half + 1, :]
    for d in range(-half, half + 1):
        if d == 0:
            continue
        shifted = pltpu.roll(x, (-d) % L, 0)
        valid = jnp.logical_and(row + d >= 0, row + d <= L - 1)
        acc = acc + jnp.where(valid, shifted, 0.0) * w_ref[half + d:half + d + 1, :]
    y = acc * jax.nn.sigmoid(acc)
    j = pl.program_id(1)
    for g in range(tc // DN_DK):
        yg = y[:, g * DN_DK:(g + 1) * DN_DK]
        inv = lax.rsqrt(jnp.sum(yg * yg, axis=-1, keepdims=True) + EPS)
        fac = jnp.where(j < n_q_tiles, inv * DN_DK ** -0.5, jnp.where(j < 2 * n_q_tiles, inv, 1.0))
        o_ref[0, :, g * DN_DK:(g + 1) * DN_DK] = yg * fac


def dn_prep(qkv_raw, B, b_off, conv_w):
    _, L, CH = qkv_raw.shape
    tc = 256
    return pl.pallas_call(
        functools.partial(_dn_prep_kernel, n_q_tiles=DN_HEADS * DN_DK // tc),
        grid=(B, CH // tc),
        in_specs=[pl.BlockSpec((1, L, tc), lambda b, j: (b + b_off, 0, j)),
                  pl.BlockSpec((DN_CONV, tc), lambda b, j: (0, j))],
        out_specs=pl.BlockSpec((1, L, tc), lambda b, j: (b, 0, j)),
        out_shape=jax.ShapeDtypeStruct((B, L, CH), F32),
        compiler_params=_params("parallel", "parallel"),
    )(qkv_raw, conv_w)


def _dn_gate_kernel(ab_ref, alog_ref, dt_ref, o_ref):
    n = 2 * DN_HEADS
    a = ab_ref[:, :n] + dt_ref[...]
    softplus = jnp.maximum(a, 0.0) + jnp.log1p(jnp.exp(-jnp.abs(a)))
    o_ref[:, :n] = -jnp.exp(alog_ref[...]) * softplus
    o_ref[:, n:] = jax.nn.sigmoid(ab_ref[:, n:])


def dn_gates(p_ab, a_log, dt_bias):
    T, n2 = p_ab.shape
    tm = _pick(T, (2048, 1024, 512, 256, 128))
    n = 2 * DN_HEADS
    return pl.pallas_call(
        _dn_gate_kernel,
        grid=(T // tm,),
        in_specs=[pl.BlockSpec((tm, n2), lambda i: (i, 0)),
                  pl.BlockSpec((1, n), lambda i: (0, 0)),
                  pl.BlockSpec((1, n), lambda i: (0, 0))],
        out_specs=pl.BlockSpec((tm, n2), lambda i: (i, 0)),
        out_shape=jax.ShapeDtypeStruct((T, n2), F32),
        compiler_params=_params("parallel"),
    )(p_ab, a_log.reshape(1, n), dt_bias.reshape(1, n))


def _split3(x):
    hi = x.astype(BF16)
    r1 = x - hi.astype(F32)
    mid = r1.astype(BF16)
    lo = (r1 - mid.astype(F32)).astype(BF16)
    return hi, mid, lo


def _dn_chunk_kernel(qkv_ref, gcol_ref, grow_ref, s0_ref, o_ref, sout_ref, s_ref):
    d = pl.program_id(1)
    n = pl.program_id(2)
    C = DN_CHUNK
    H = DN_HEADS
    nt = (((1,), (1,)), ((), ()))

    @pl.when(n == 0)
    def _():
        s_ref[...] = s0_ref[0, 0]

    r = lax.broadcasted_iota(jnp.int32, (C, C), 0)
    c = lax.broadcasted_iota(jnp.int32, (C, C), 1)
    fwd = d == 0
    lag = (r - c) * jnp.where(fwd, 1, -1)
    incl = lag >= 0
    strict = lag > 0
    eye = (r == c).astype(F32)
    tri = jnp.where(incl, 1.0, 0.0).astype(BF16)
    g_col = gcol_ref[0, 0]
    g_row = grow_ref[0, 0, 0]
    gc_col = sum(jnp.dot(tri, p, preferred_element_type=F32) for p in _split3(g_col))
    gc_row = sum(lax.dot_general(p, tri, nt, preferred_element_type=F32) for p in _split3(g_row))
    g_tot = jnp.where(fwd, gc_row[:, C - 1:C], gc_row[:, 0:1])

    def mm(x, y, dims=None):
        x, y = x.astype(BF16), y.astype(BF16)
        if dims is None:
            return jnp.dot(x, y, preferred_element_type=F32)
        return lax.dot_general(x, y, dims, preferred_element_type=F32)

    hs = range(H)
    q = [qkv_ref[0, :, h * DN_DK:(h + 1) * DN_DK] for h in hs]
    k = [qkv_ref[0, :, (H + h) * DN_DK:(H + h + 1) * DN_DK] for h in hs]
    v = [qkv_ref[0, :, 2 * H * DN_DK + h * DN_DV:2 * H * DN_DK + (h + 1) * DN_DV] for h in hs]
    gc = [gc_col[:, h:h + 1] for h in hs]
    beta = [g_col[:, H + h:H + h + 1] for h in hs]
    g_last = [g_tot[h:h + 1, :] for h in hs]
    eg = [jnp.exp(gc[h]) for h in hs]
    decay = [jnp.exp(jnp.where(incl, gc[h] - gc_row[h:h + 1, :], NEG_BIG)) for h in hs]
    kk = [mm(k[h], k[h], nt) for h in hs]
    qk = [mm(q[h], k[h], nt) for h in hs]
    a = [jnp.where(strict, beta[h] * kk[h] * decay[h], 0.0) for h in hs]
    qk = [jnp.where(incl, qk[h] * decay[h], 0.0) for h in hs]
    inv = None
    for l in range(C.bit_length() - 1):
        couple = jnp.logical_and((r >> (l + 1)) == (c >> (l + 1)), (r >> l) != (c >> l))
        a_l = [jnp.where(couple, a[h], 0.0) for h in hs]
        if inv is None:
            inv = [eye - a_l[h] for h in hs]
        else:
            t = [mm(inv[h], a_l[h]) for h in hs]
            inv = [inv[h] - mm(t[h], inv[h]) for h in hs]
    rhs = [jnp.concatenate([v[h] * beta[h], k[h] * (beta[h] * eg[h])], axis=1) for h in hs]
    sol = [mm(inv[h], rhs[h]) for h in hs]
    S = [s_ref[h] for h in hs]
    v_new = [sol[h][:, :DN_DV] - mm(sol[h][:, DN_DV:], S[h]) for h in hs]
    o = [mm(q[h] * eg[h], S[h]) + mm(qk[h], v_new[h]) for h in hs]
    upd = [mm(k[h] * jnp.exp(g_last[h] - gc[h]), v_new[h], (((0,), (0,)), ((), ()))) for h in hs]
    for h in hs:
        o_ref[0, 0, :, h * DN_DV:(h + 1) * DN_DV] = o[h]
        s_ref[h] = S[h] * jnp.exp(g_last[h]) + upd[h]

    @pl.when(n == pl.num_programs(2) - 1)
    def _():
        sout_ref[0, 0] = s_ref[...]


def dn_scan(qkv, gates, s0):
    B, L, CH = qkv.shape
    C, H = DN_CHUNK, DN_HEADS
    N = L // C
    g4 = gates.reshape(B, L, 2, 2, H)
    gcol = g4.transpose(0, 3, 1, 2, 4).reshape(B, 2, L, 2 * H)
    grow = gcol.reshape(B, 2, N, C, 2 * H).transpose(0, 1, 2, 4, 3)
    ne = lambda d, n: n + d * (N - 1 - 2 * n)
    return pl.pallas_call(
        _dn_chunk_kernel,
        name="dn_chunk",
        grid=(B, 2, N),
        in_specs=[pl.BlockSpec((1, C, CH), lambda b, d, n: (b, ne(d, n), 0)),
                  pl.BlockSpec((1, 1, C, 2 * H), lambda b, d, n: (b, d, ne(d, n), 0)),
                  pl.BlockSpec((1, 1, 1, 2 * H, C), lambda b, d, n: (b, d, ne(d, n), 0, 0)),
                  pl.BlockSpec((1, 1, H, DN_DK, DN_DV), lambda b, d, n: (b, d, 0, 0, 0))],
        out_specs=[pl.BlockSpec((1, 1, C, H * DN_DV), lambda b, d, n: (b, d, ne(d, n), 0)),
                   pl.BlockSpec((1, 1, H, DN_DK, DN_DV), lambda b, d, n: (b, d, 0, 0, 0))],
        out_shape=[jax.ShapeDtypeStruct((B, 2, L, H * DN_DV), F32),
                   jax.ShapeDtypeStruct((B, 2, H, DN_DK, DN_DV), F32)],
        scratch_shapes=[pltpu.VMEM((H, DN_DK, DN_DV), F32)],
        compiler_params=_params("parallel", "arbitrary", "arbitrary"),
    )(qkv, gcol, grow, s0)


def _dn_out_kernel(of_ref, ob_ref, z_ref, g_ref, o_ref):
    for h in range(DN_HEADS):
        cols = slice(h * DN_DV, (h + 1) * DN_DV)
        o = of_ref[0, 0, :, cols] + ob_ref[0, 0, :, cols]
        z = z_ref[0, :, cols].astype(F32)
        o_ref[0, :, cols] = (_rms(o, g_ref[...]) * (z * jax.nn.sigmoid(z))).astype(o_ref.dtype)


def dn_output(o2, z, b_off, g):
    B, _, L, W = o2.shape
    tm = _pick(L, (512, 256, 128))
    return pl.pallas_call(
        _dn_out_kernel,
        grid=(B, L // tm),
        in_specs=[pl.BlockSpec((1, 1, tm, W), lambda b, i: (b, 0, i, 0)),
                  pl.BlockSpec((1, 1, tm, W), lambda b, i: (b, 1, i, 0)),
                  pl.BlockSpec((1, tm, W), lambda b, i: (b + b_off, i, 0)),
                  pl.BlockSpec((1, DN_DV), lambda b, i: (0, 0))],
        out_specs=pl.BlockSpec((1, tm, W), lambda b, i: (b, i, 0)),
        out_shape=jax.ShapeDtypeStruct((B, L, W), BF16),
        compiler_params=_params("parallel", "parallel"),
    )(o2, o2, z, g.reshape(1, DN_DV))


def gated_deltanet(qkv_raw, z, B, b_off, gates, lp, s0):
    qkv = dn_prep(qkv_raw, B, b_off, lp["dn_conv"])
    o2, s_fin = dn_scan(qkv, gates, s0)
    return dn_output(o2, z, b_off, lp["dn_out_norm"]), s_fin


def _axial_rope(x):
    L = x.shape[1]
    half = x.shape[-1] // 2
    t = jnp.arange(L)
    inv = ROPE_THETA ** (-jnp.arange(0, half, 2, dtype=F32) / half)
    out = []
    for pos, xa in ((t // GRID_W, x[..., :half]), (t % GRID_W, x[..., half:])):
        ang = pos.astype(F32)[:, None] * inv[None, :]
        cos, sin = jnp.cos(ang)[:, None, :], jnp.sin(ang)[:, None, :]
        x1, x2 = xa[..., : half // 2], xa[..., half // 2:]
        out += [x1 * cos - x2 * sin, x2 * cos + x1 * sin]
    return jnp.concatenate(out, axis=-1)


def _pad_heads(nope, rope):
    B, L, H, _ = nope.shape
    rope = jnp.broadcast_to(rope, (B, L, H, MLA_ROPE))
    pad = jnp.zeros((B, L, H, MLA_QK_PAD - MLA_NOPE - MLA_ROPE), nope.dtype)
    return jnp.concatenate([nope, rope, pad], axis=-1).reshape(B, L, H * MLA_QK_PAD).astype(BF16)


def _layer(x, lp, mods, dims, caches):
    Bc, Lc, Bl, Ll = dims
    Tc, Tl = Bc * Lc, Bl * Ll
    T = Tc + Tl
    tm = _pick(math.gcd(Tc, Ll), (1024, 512, 256, 128))

    def make_cidx(tile):
        nct, tpl = Tc // tile, Ll // tile
        return lambda i: jnp.where(i < nct, 0, 1 + (i - nct) // tpl)

    cidx = make_cidx(tm)
    h = modulate(x, lp["norm1_g"], mods, cidx, tm, 0, 1)

    w_in = lp["w_in"]
    cuts = np.cumsum((0,) + IN_SIZES)
    col = lambda a, b: w_in[:, cuts[a]:cuts[b]].astype(BF16)
    p_mla = matmul(h, col(0, 3))
    p_dn = matmul(h, col(3, 4), BF16)
    p_z = matmul(h, col(4, 5), BF16)
    p_ab = matmul(h, col(5, 7))
    p_hy = matmul(h, col(7, 8), BF16)
    p_na = matmul(h, col(8, 9), BF16)
    na_kv_ctx = matmul(h, w_in[:, cuts[8] + BRANCH_W:cuts[9]].astype(BF16), F32, rows=Tc)
    p_gate = matmul(h, col(9, 10), BF16)

    cq, ckv, krope = p_mla[:, :MLA_Q_RANK], p_mla[:, MLA_Q_RANK:MLA_Q_RANK + MLA_KV_RANK], p_mla[:, -MLA_ROPE:]

    cq_n = rmsnorm(cq, lp["mla_q_norm"], BF16)
    ckv_n = rmsnorm(ckv, lp["mla_kv_norm"], F32)
    q_all = matmul(cq_n, lp["mla_w_qb"].astype(BF16))
    w_kvb = lp["mla_w_kvb"].astype(BF16)
    kv_all = matmul(ckv_n.astype(BF16), w_kvb, BF16)
    ckv_ctx, krope_ctx, nak_ctx, nav_ctx, s_f0, s_b0 = caches
    kv_cache = matmul(ckv_ctx.reshape(-1, MLA_KV_RANK).astype(BF16), w_kvb, BF16)
    Lp = ckv_ctx.shape[1]

    def heads_q(t, B, L, rotate):
        t = t.reshape(B, L, MLA_HEADS, MLA_NOPE + MLA_ROPE)
        rope = t[..., MLA_NOPE:]
        if rotate:
            rope = _axial_rope(rope)
        return _pad_heads(t[..., :MLA_NOPE], rope)

    def heads_kv(kv, kr, B, L):
        kv = kv.reshape(B, L, MLA_HEADS, MLA_NOPE + MLA_V)
        return _pad_heads(kv[..., :MLA_NOPE], kr.reshape(B, L, 1, MLA_ROPE)), kv.reshape(B, L, -1)

    qc = heads_q(q_all[:Tc], Bc, Lc, False)
    kc, kvc = heads_kv(kv_all[:Tc], krope[:Tc], Bc, Lc)
    o_a_ctx = attention(qc, kc, kvc, MLA_HEADS, MLA_QK_PAD, MLA_V, MLA_SCALE, v_off=1, v_stride=2)
    ql = heads_q(q_all[Tc:], Bl, Ll, True)
    kl, kvl = heads_kv(kv_all[Tc:], _axial_rope(krope[Tc:].reshape(Bl, Ll, 1, MLA_ROPE)), Bl, Ll)
    kp, kvp = heads_kv(kv_cache, krope_ctx, Bl, Lp)
    o_a_lat = attention(ql, jnp.concatenate([kl, kp], axis=1), jnp.concatenate([kvl, kvp], axis=1),
                        MLA_HEADS, MLA_QK_PAD, MLA_V, MLA_SCALE, v_off=1, v_stride=2)
    o_a = jnp.concatenate([o_a_ctx.reshape(Tc, BRANCH_W), o_a_lat.reshape(Tl, BRANCH_W)])

    def seqs(p, L, start, B):
        if start % L == 0 and T % L == 0:
            return p.reshape(T // L, L, -1), start // L
        return p[start:start + B * L].reshape(B, L, -1), 0

    ctx_of = lambda p: seqs(p, Lc, 0, Bc)
    lat_of = lambda p: seqs(p, Ll, Tc, Bl)

    gates = dn_gates(p_ab, lp["dn_a_log"], lp["dn_dt_bias"])
    zero_state = jnp.zeros((Bc, 2, DN_HEADS, DN_DK, DN_DV), F32)
    (dn_c, off_c), (dn_l, off_l) = ctx_of(p_dn), lat_of(p_dn)
    o_b_ctx, s_ctx = gated_deltanet(dn_c, ctx_of(p_z)[0], Bc, off_c, gates[:Tc].reshape(Bc, Lc, -1), lp, zero_state)
    o_b_lat, _ = gated_deltanet(dn_l, lat_of(p_z)[0], Bl, off_l, gates[Tc:].reshape(Bl, Ll, -1), lp,
                                jnp.stack([s_f0, s_b0], axis=1))
    s_f, s_b = s_ctx[:, 0], s_ctx[:, 1]
    o_b = jnp.concatenate([o_b_ctx.reshape(Tc, BRANCH_W), o_b_lat.reshape(Tl, BRANCH_W)])

    (hy_c, off_c), (hy_l, off_l) = ctx_of(p_hy), lat_of(p_hy)
    o_c = jnp.concatenate([hyena(hy_c, Bc, off_c, lp).reshape(Tc, BRANCH_W),
                           hyena(hy_l, Bl, off_l, lp).reshape(Tl, BRANCH_W)])

    (na_c, off_c), (na_l, off_l) = ctx_of(p_na), lat_of(p_na)
    o_d_ctx = attention(na_c, na_c, na_c, NA_HEADS, NA_HD, NA_HD, NA_SCALE, k_off=NA_HEADS, v_off=2 * NA_HEADS,
                        batch=Bc, b_off=off_c)
    o_d_lat = neighbourhood_attention(na_l, Bl, off_l, nak_ctx.reshape(Bl, Lp, -1).astype(BF16),
                                      nav_ctx.reshape(Bl, Lp, -1).astype(BF16), lp["na_rpb"])
    o_d = jnp.concatenate([o_d_ctx.reshape(Tc, BRANCH_W), o_d_lat.reshape(Tl, BRANCH_W)])

    merged = merge_branches((o_a, o_b, o_c, o_d), p_gate, lp["w_branch"].astype(BF16), tm)
    x = matmul_gated_residual(merged, lp["w_out"].astype(BF16), x, mods, cidx, tm, 2)

    h2 = modulate(x, lp["norm2_g"], mods, cidx, tm, 3, 4)
    pq = matmul(h2, lp["peer_wq"].astype(BF16))
    route = peer_route(pq, lp["peer_keys"])
    tt = _pick(math.gcd(Tc, Ll), (512, 256, 128))
    x = peer_dense(h2, lp["peer_u"].astype(BF16), lp["peer_v"].astype(BF16), route, x, mods, make_cidx(tt), tt, 5)

    ctx_out = (ckv_n[:Tc].reshape(Bc, Lc, MLA_KV_RANK), krope[:Tc].reshape(Bc, Lc, MLA_ROPE),
               na_kv_ctx[:, :BRANCH_W].reshape(Bc, Lc, NA_HEADS, NA_HD),
               na_kv_ctx[:, BRANCH_W:].reshape(Bc, Lc, NA_HEADS, NA_HD), s_f, s_b)
    return x, ctx_out


def kernel(x_prompt, x_sample, cache_mla_ckv, cache_mla_krope, cache_na_k, cache_na_v, state_dn_fwd, state_dn_bwd, c, c_ctx, norm1_g, w_ada, b_ada, w_in, mla_q_norm, mla_w_qb, mla_kv_norm, mla_w_kvb, dn_conv, dn_a_log, dn_dt_bias, dn_out_norm, hy_conv, hy_w1, hy_b1, hy_w2, hy_b2, hy_w3, hy_bias, na_rpb, w_branch, w_out, norm2_g, peer_wq, peer_keys, peer_u, peer_v, final_g):
    Bc, Lc, D = x_prompt.shape
    Bl, Ll, _ = x_sample.shape
    depth = w_in.shape[0]
    Tc = Bc * Lc
    x = jnp.concatenate([x_prompt.reshape(Tc, D), x_sample.reshape(Bl * Ll, D)])
    n_cond = 1 + Bl
    cond_rows = -(-n_cond // SUBLANES) * SUBLANES
    cond = jnp.concatenate([c_ctx[None], c, jnp.zeros((cond_rows - n_cond, D), F32)])

    names = ("norm1_g", "w_in", "mla_q_norm", "mla_w_qb", "mla_kv_norm", "mla_w_kvb", "dn_conv", "dn_a_log",
             "dn_dt_bias", "dn_out_norm", "hy_conv", "hy_w1", "hy_b1", "hy_w2", "hy_b2", "hy_w3", "hy_bias",
             "na_rpb", "w_branch", "w_out", "norm2_g", "peer_wq", "peer_keys", "peer_u", "peer_v")
    vals = (norm1_g, w_in, mla_q_norm, mla_w_qb, mla_kv_norm, mla_w_kvb, dn_conv, dn_a_log, dn_dt_bias,
            dn_out_norm, hy_conv, hy_w1, hy_b1, hy_w2, hy_b2, hy_w3, hy_bias, na_rpb, w_branch, w_out, norm2_g,
            peer_wq, peer_keys, peer_u, peer_v)
    ctx = []
    for l in range(depth):
        lp = {n: v[l] for n, v in zip(names, vals)}
        mods = ada_modulation(cond, w_ada[l], b_ada[l]).reshape(cond_rows * 6, 1, D)
        caches = (cache_mla_ckv[:, l], cache_mla_krope[:, l], cache_na_k[:, l], cache_na_v[:, l],
                  state_dn_fwd[:, l], state_dn_bwd[:, l])
        x, ctx_out = _layer(x, lp, mods, (Bc, Lc, Bl, Ll), caches)
        ctx.append(ctx_out)
    y = rmsnorm(x, final_g)
    stack = lambda k: jnp.stack([t[k] for t in ctx], axis=1)
    return (y[:Tc].reshape(Bc, Lc, D), y[Tc:].reshape(Bl, Ll, D), stack(0), stack(1), stack(2), stack(3),
            stack(4), stack(5))
```

```python
import functools
import math

import numpy as np
import jax
import jax.numpy as jnp
from jax import lax
from jax.experimental import pallas as pl
from jax.experimental.pallas import tpu as pltpu

F32 = jnp.float32
BF16 = jnp.bfloat16

V7X_VMEM_BYTES = 64 * 1024 * 1024
VMEM_LIMIT = V7X_VMEM_BYTES - 8 * 1024 * 1024
LANES = 128
SUBLANES = 8

D_MODEL = 2048
GRID_W = 64
EPS = 1e-6
N_BRANCH = 4
BRANCH_W = D_MODEL // 2

MLA_HEADS = 8
MLA_NOPE = 128
MLA_ROPE = 64
MLA_V = BRANCH_W // MLA_HEADS
MLA_Q_RANK = D_MODEL // 4
MLA_KV_RANK = D_MODEL // 8
MLA_SCALE = (MLA_NOPE + MLA_ROPE) ** -0.5
MLA_QK_PAD = 256
ROPE_THETA = 10000.0

DN_HEADS = 8
DN_DK = 128
DN_DV = BRANCH_W // DN_HEADS
DN_CONV = 5
DN_CHUNK = 64
DN_CONV_CH = 2 * DN_HEADS * DN_DK + DN_HEADS * DN_DV

HY_WIDTH = BRANCH_W
HY_ORDER = 2
HY_SHORT = 3
HY_EMB = 33
HY_FAST_DECAY = 0.3
HY_SLOW_DECAY = 1.5
HY_DECAY_TARGET = 1e-2

NA_HEADS = 8
NA_HD = BRANCH_W // NA_HEADS
NA_KH = 8
NA_KW = 16
NA_SCALE = NA_HD ** -0.5
NA_QROWS = 4
NA_REGION = NA_KH + NA_QROWS
NEG_BIG = -1e30

PEER_HEADS = 8
PEER_NKEYS = 128
PEER_N = PEER_NKEYS * PEER_NKEYS
PEER_DKEY = 256
PEER_TOPK = 16

IN_SIZES = (MLA_Q_RANK, MLA_KV_RANK, MLA_ROPE, DN_CONV_CH, DN_HEADS * DN_DV, 2 * DN_HEADS, 2 * DN_HEADS,
            (HY_ORDER + 1) * HY_WIDTH, 3 * NA_HEADS * NA_HD, N_BRANCH * D_MODEL)


def _params(*sem):
    return pltpu.CompilerParams(dimension_semantics=sem, vmem_limit_bytes=VMEM_LIMIT)


def _pick(n, prefs):
    for p in prefs:
        if n % p == 0:
            return p
    return n


def _mm_kernel(x_ref, w_ref, o_ref):
    o_ref[...] = jnp.dot(x_ref[...], w_ref[...], preferred_element_type=F32).astype(o_ref.dtype)


def matmul(x, w, out_dtype=F32, tm=None, tn=None, rows=None):
    M, K = x.shape
    M = rows or M
    N = w.shape[1]
    tm = tm or _pick(M, (1024, 512, 256, 128))
    tn = tn or _pick(N, (512, 256, 128))
    return pl.pallas_call(
        _mm_kernel,
        grid=(M // tm, N // tn),
        in_specs=[pl.BlockSpec((tm, K), lambda i, j: (i, 0)),
                  pl.BlockSpec((K, tn), lambda i, j: (0, j))],
        out_specs=pl.BlockSpec((tm, tn), lambda i, j: (i, j)),
        out_shape=jax.ShapeDtypeStruct((M, N), out_dtype),
        compiler_params=_params("parallel", "parallel"),
    )(x, w)


def _ada_kernel(c_ref, w_ref, b_ref, o_ref):
    c = c_ref[...]
    o_ref[...] = jnp.dot(c * jax.nn.sigmoid(c), w_ref[...], preferred_element_type=F32) + b_ref[...]


def ada_modulation(cond, w_ada, b_ada):
    R, K = cond.shape
    N = w_ada.shape[1]
    tn = _pick(N, (1536, 1024, 512))
    return pl.pallas_call(
        _ada_kernel,
        grid=(N // tn,),
        in_specs=[pl.BlockSpec((R, K), lambda j: (0, 0)),
                  pl.BlockSpec((K, tn), lambda j: (0, j)),
                  pl.BlockSpec((1, tn), lambda j: (0, j))],
        out_specs=pl.BlockSpec((R, tn), lambda j: (0, j)),
        out_shape=jax.ShapeDtypeStruct((R, N), F32),
        compiler_params=_params("parallel"),
    )(cond, w_ada, b_ada.reshape(1, N))


def _rms(x, g):
    return x * lax.rsqrt(jnp.mean(x * x, axis=-1, keepdims=True) + EPS) * g


def _modulate_kernel(x_ref, g_ref, sh_ref, sc_ref, o_ref):
    y = _rms(x_ref[...], g_ref[...])
    o_ref[...] = (y * (1.0 + sc_ref[0]) + sh_ref[0]).astype(o_ref.dtype)


def modulate(x, g, mods, cidx, tm, shift_slot, scale_slot):
    T, D = x.shape
    return pl.pallas_call(
        _modulate_kernel,
        grid=(T // tm,),
        in_specs=[pl.BlockSpec((tm, D), lambda i: (i, 0)),
                  pl.BlockSpec((1, D), lambda i: (0, 0)),
                  pl.BlockSpec((1, 1, D), lambda i: (cidx(i) * 6 + shift_slot, 0, 0)),
                  pl.BlockSpec((1, 1, D), lambda i: (cidx(i) * 6 + scale_slot, 0, 0))],
        out_specs=pl.BlockSpec((tm, D), lambda i: (i, 0)),
        out_shape=jax.ShapeDtypeStruct((T, D), BF16),
        compiler_params=_params("parallel"),
    )(x, g.reshape(1, D), mods, mods)


def _rmsnorm_kernel(x_ref, g_ref, o_ref):
    o_ref[...] = _rms(x_ref[...].astype(F32), g_ref[...]).astype(o_ref.dtype)


def rmsnorm(x, g, out_dtype=F32):
    T, D = x.shape
    tm = _pick(T, (1024, 512, 256, 128))
    return pl.pallas_call(
        _rmsnorm_kernel,
        grid=(T // tm,),
        in_specs=[pl.BlockSpec((tm, D), lambda i: (i, 0)),
                  pl.BlockSpec((1, D), lambda i: (0, 0))],
        out_specs=pl.BlockSpec((tm, D), lambda i: (i, 0)),
        out_shape=jax.ShapeDtypeStruct((T, D), out_dtype),
        compiler_params=_params("parallel"),
    )(x, g.reshape(1, D))


def _merge_kernel(a_ref, b_ref, c_ref, d_ref, ga_ref, gb_ref, gc_ref, gd_ref, w_ref, o_ref):
    acc = None
    for n, (br, gl) in enumerate(((a_ref, ga_ref), (b_ref, gb_ref), (c_ref, gc_ref), (d_ref, gd_ref))):
        proj = jnp.dot(br[...], w_ref[n], preferred_element_type=F32)
        term = jax.nn.sigmoid(gl[...].astype(F32)) * proj
        acc = term if acc is None else acc + term
    o_ref[...] = acc.astype(o_ref.dtype)


def merge_branches(branches, gate_logits, w_branch, tm):
    T = branches[0].shape[0]
    tn = 512
    br_spec = pl.BlockSpec((tm, BRANCH_W), lambda i, j: (i, 0))
    nj = D_MODEL // tn
    gl_specs = [pl.BlockSpec((tm, tn), functools.partial(lambda i, j, n: (i, n * nj + j), n=n))
                for n in range(N_BRANCH)]
    return pl.pallas_call(
        _merge_kernel,
        grid=(T // tm, D_MODEL // tn),
        in_specs=[br_spec, br_spec, br_spec, br_spec, *gl_specs,
                  pl.BlockSpec((N_BRANCH, BRANCH_W, tn), lambda i, j: (0, 0, j))],
        out_specs=pl.BlockSpec((tm, tn), lambda i, j: (i, j)),
        out_shape=jax.ShapeDtypeStruct((T, D_MODEL), BF16),
        compiler_params=_params("parallel", "parallel"),
    )(*branches, gate_logits, gate_logits, gate_logits, gate_logits, w_branch)


def _mm_resid_kernel(m_ref, w_ref, x_ref, g_ref, o_ref):
    o_ref[...] = x_ref[...] + g_ref[0] * jnp.dot(m_ref[...], w_ref[...], preferred_element_type=F32)


def matmul_gated_residual(m, w, x, mods, cidx, tm, gate_slot):
    T, K = m.shape
    N = w.shape[1]
    tn = 512
    return pl.pallas_call(
        _mm_resid_kernel,
        grid=(T // tm, N // tn),
        in_specs=[pl.BlockSpec((tm, K), lambda i, j: (i, 0)),
                  pl.BlockSpec((K, tn), lambda i, j: (0, j)),
                  pl.BlockSpec((tm, tn), lambda i, j: (i, j)),
                  pl.BlockSpec((1, 1, tn), lambda i, j: (cidx(i) * 6 + gate_slot, 0, j))],
        out_specs=pl.BlockSpec((tm, tn), lambda i, j: (i, j)),
        out_shape=jax.ShapeDtypeStruct((T, N), F32),
        compiler_params=_params("parallel", "parallel"),
    )(m, w, x, mods)


def _attn_kernel(q_ref, k_ref, v_ref, o_ref, *, scale):
    s = lax.dot_general(q_ref[0], k_ref[0], (((1,), (1,)), ((), ())), preferred_element_type=F32) * scale
    p = jnp.exp(s - jnp.max(s, axis=-1, keepdims=True))
    l = jnp.sum(p, axis=-1, keepdims=True)
    o = jnp.dot(p.astype(BF16), v_ref[0], preferred_element_type=F32)
    o_ref[0] = (o / l).astype(o_ref.dtype)


def attention(q, k, v, heads, dqk, dv, scale, q_off=0, k_off=0, v_off=0, v_stride=1, k_stride=1, batch=None,
              b_off=0):
    B, Lq, _ = q.shape
    B = batch or B
    Lk = k.shape[1]
    tq = _pick(Lq, (256, 128))
    return pl.pallas_call(
        functools.partial(_attn_kernel, scale=scale),
        grid=(B, heads, Lq // tq),
        in_specs=[pl.BlockSpec((1, tq, dqk), lambda b, h, i: (b + b_off, i, q_off + h)),
                  pl.BlockSpec((1, Lk, dqk), lambda b, h, i: (b + b_off, 0, k_off + h * k_stride)),
                  pl.BlockSpec((1, Lk, dv), lambda b, h, i: (b + b_off, 0, v_off + h * v_stride))],
        out_specs=pl.BlockSpec((1, tq, dv), lambda b, h, i: (b, i, h)),
        out_shape=jax.ShapeDtypeStruct((B, Lq, heads * dv), BF16),
        compiler_params=_params("parallel", "parallel", "parallel"),
    )(q, k, v)


def _na_kernel(q_ref, k_ref, v_ref, kc_ref, vc_ref, bias_ref, o_ref, *, rows):
    rb = pl.program_id(2)
    nkeys = NA_REGION * GRID_W
    start = jnp.clip(rb * NA_QROWS - NA_KH // 2, 0, rows - NA_REGION) * GRID_W
    start = pl.multiple_of(start, GRID_W)
    q = q_ref[0]
    kr = k_ref[0, pl.ds(start, nkeys), :]
    vr = v_ref[0, pl.ds(start, nkeys), :]
    nt = (((1,), (1,)), ((), ()))
    s_win = lax.dot_general(q, kr, nt, preferred_element_type=F32) * NA_SCALE + bias_ref[0, 0]
    s_ctx = lax.dot_general(q, kc_ref[0], nt, preferred_element_type=F32) * NA_SCALE
    m = jnp.maximum(jnp.max(s_win, axis=-1, keepdims=True), jnp.max(s_ctx, axis=-1, keepdims=True))
    p_win = jnp.exp(s_win - m)
    p_ctx = jnp.exp(s_ctx - m)
    l = jnp.sum(p_win, axis=-1, keepdims=True) + jnp.sum(p_ctx, axis=-1, keepdims=True)
    o = (jnp.dot(p_win.astype(BF16), vr, preferred_element_type=F32)
         + jnp.dot(p_ctx.astype(BF16), vc_ref[0], preferred_element_type=F32))
    o_ref[0] = (o / l).astype(o_ref.dtype)


def _na_bias_tables(rpb, rows):
    H = rpb.shape[0]
    n_dr, n_dc = 2 * NA_KH - 1, 2 * NA_KW - 1
    assert rows >= NA_REGION + NA_QROWS and rows % NA_QROWS == 0
    nblk = rows // NA_QROWS
    span = 2 * GRID_W
    lo = GRID_W - NA_KW
    v = jnp.pad(rpb.astype(F32), ((0, 0), (0, 0), (lo, span - lo - n_dc)), constant_values=NEG_BIG)
    skew = jnp.tile(v, (1, 1, GRID_W))[..., :GRID_W * (span - 1)].reshape(H, n_dr, GRID_W, span - 1)
    band = skew[..., GRID_W - 1:]
    c = np.arange(GRID_W)[:, None]
    kc = np.arange(GRID_W)[None, :]
    col0 = np.clip(c - NA_KW // 2, 0, GRID_W - NA_KW)
    col_ok = (kc >= col0) & (kc < col0 + NA_KW)
    band = jnp.where(col_ok, band, NEG_BIG)
    band = jnp.concatenate([band, jnp.full((H, 1, GRID_W, GRID_W), NEG_BIG, F32)], axis=1)
    tiles = []
    for rb in (0, 1, nblk - 1):
        start = int(np.clip(rb * NA_QROWS - NA_KH // 2, 0, rows - NA_REGION))
        for qr in range(NA_QROWS):
            r = rb * NA_QROWS + qr
            row0 = int(np.clip(r - NA_KH // 2, 0, rows - NA_KH))
            for j in range(NA_REGION):
                kr = start + j
                tiles.append(kr - r + NA_KH - 1 if row0 <= kr < row0 + NA_KH else n_dr)
    t = jnp.take(band, np.asarray(tiles, np.int32), axis=1).reshape(H, 3, NA_QROWS, NA_REGION, GRID_W, GRID_W)
    return t.transpose(0, 1, 2, 4, 3, 5).reshape(H, 3, NA_QROWS * GRID_W, NA_REGION * GRID_W)


def neighbourhood_attention(na_in, B, b_off, k_ctx, v_ctx, bias):
    _, L, _ = na_in.shape
    Lc = k_ctx.shape[1]
    rows = L // GRID_W
    nblk = rows // NA_QROWS
    tq = NA_QROWS * GRID_W
    nkeys = NA_REGION * GRID_W
    kind = lambda r: jnp.where(r == 0, 0, jnp.where(r == nblk - 1, 2, 1))
    H = NA_HEADS
    return pl.pallas_call(
        functools.partial(_na_kernel, rows=rows),
        grid=(B, H, nblk),
        in_specs=[pl.BlockSpec((1, tq, NA_HD), lambda b, h, r: (b + b_off, r, h)),
                  pl.BlockSpec((1, L, NA_HD), lambda b, h, r: (b + b_off, 0, H + h)),
                  pl.BlockSpec((1, L, NA_HD), lambda b, h, r: (b + b_off, 0, 2 * H + h)),
                  pl.BlockSpec((1, Lc, NA_HD), lambda b, h, r: (b, 0, h)),
                  pl.BlockSpec((1, Lc, NA_HD), lambda b, h, r: (b, 0, h)),
                  pl.BlockSpec((1, 1, tq, nkeys), lambda b, h, r: (h, kind(r), 0, 0))],
        out_specs=pl.BlockSpec((1, tq, NA_HD), lambda b, h, r: (b, r, h)),
        out_shape=jax.ShapeDtypeStruct((B, L, H * NA_HD), BF16),
        compiler_params=_params("parallel", "parallel", "parallel"),
    )(na_in, na_in, na_in, k_ctx, v_ctx, bias)


def _top_rows(works, k):
    outs = [[] for _ in works]
    for _ in range(k):
        mx = [jnp.max(w, axis=0, keepdims=True) for w in works]
        for out, m in zip(outs, mx):
            out.append(m)
        works = [jnp.where(w == m, -jnp.inf, w) for w, m in zip(works, mx)]
    return outs


_PEER_PAIRS = [(i, k) for i in range(PEER_TOPK) for k in range(PEER_TOPK) if (i + 1) * (k + 1) <= PEER_TOPK]
PEER_ROUTE_HEADS = 2


def _peer_route_kernel(q_ref, keys_ref, s1_ref, s2_ref, e1_ref, e2_ref, th_ref):
    half = PEER_DKEY // 2
    nt = (((1,), (1,)), ((), ()))
    heads = range(s1_ref.shape[0])
    scores = []
    for g in heads:
        q = q_ref[:, g * PEER_DKEY:(g + 1) * PEER_DKEY]
        scores.append(lax.dot_general(keys_ref[0], q[:, :half], nt, preferred_element_type=F32))
        scores.append(lax.dot_general(keys_ref[1], q[:, half:], nt, preferred_element_type=F32))
    tops = _top_rows(scores, PEER_TOPK)
    cands = [jnp.concatenate([tops[2 * g][i] + tops[2 * g + 1][k] for i, k in _PEER_PAIRS], axis=0) for g in heads]
    thetas = [t[-1] for t in _top_rows(cands, PEER_TOPK)]
    for g in heads:
        s1, s2 = scores[2 * g], scores[2 * g + 1]
        m1, m2 = tops[2 * g][0], tops[2 * g + 1][0]
        z = jnp.sum(jnp.where(cands[g] >= thetas[g], jnp.exp(cands[g] - (m1 + m2)), 0.0), axis=0, keepdims=True)
        s1_ref[g] = s1
        s2_ref[g] = s2
        e1_ref[g] = jnp.exp(s1 - m1) / z
        e2_ref[g] = jnp.exp(s2 - m2)
        th_ref[g] = thetas[g]


def peer_route(q, keys):
    T = q.shape[0]
    tt = _pick(T, (512, 256, 128))
    H = PEER_HEADS
    hp = PEER_ROUTE_HEADS
    big = jax.ShapeDtypeStruct((H, PEER_NKEYS, T), F32)
    big_spec = pl.BlockSpec((hp, PEER_NKEYS, tt), lambda i, h: (h, 0, i))
    return pl.pallas_call(
        _peer_route_kernel,
        name="peer_route",
        grid=(T // tt, H // hp),
        in_specs=[pl.BlockSpec((tt, hp * PEER_DKEY), lambda i, h: (i, h)),
                  pl.BlockSpec((2, PEER_NKEYS, PEER_DKEY // 2), lambda i, h: (0, 0, 0))],
        out_specs=[big_spec, big_spec, big_spec, big_spec,
                   pl.BlockSpec((hp, 1, tt), lambda i, h: (h, 0, i))],
        out_shape=[big, big, big, big, jax.ShapeDtypeStruct((H, 1, T), F32)],
        compiler_params=_params("parallel", "parallel"),
    )(q, keys)


PEER_TE = 8 * PEER_NKEYS


def _peer_dense_kernel(h_ref, u_ref, v_ref, s1_ref, s2_ref, e1_ref, e2_ref, th_ref, x_ref, g_ref, o_ref,
                       xu_ref, w_ref, acc_ref):
    j = pl.program_id(1)
    tt = h_ref.shape[0]

    @pl.when(j == 0)
    def _():
        acc_ref[...] = jnp.zeros_like(acc_ref)

    xu_ref[...] = lax.dot_general(u_ref[...], h_ref[...], (((1,), (1,)), ((), ())), preferred_element_type=F32)

    for aa in range(PEER_TE // PEER_NKEYS):
        rows = slice(aa * PEER_NKEYS, (aa + 1) * PEER_NKEYS)
        for tc in range(tt // LANES):
            cols = slice(tc * LANES, (tc + 1) * LANES)
            gate = jnp.zeros((PEER_NKEYS, LANES), F32)
            for h in range(PEER_HEADS):
                s1 = s1_ref[h, aa:aa + 1, cols]
                e1 = e1_ref[h, aa:aa + 1, cols]
                gate = gate + jnp.where(s1 + s2_ref[h, :, cols] >= th_ref[h, :, cols], e1 * e2_ref[h, :, cols], 0.0)
            pre = xu_ref[rows, cols]
            act = 0.5 * pre * (1.0 + lax.erf(pre * (2.0 ** -0.5)))
            w_ref[rows, cols] = (gate * act).astype(w_ref.dtype)

    acc_ref[...] += lax.dot_general(w_ref[...], v_ref[...], (((0,), (0,)), ((), ())), preferred_element_type=F32)

    @pl.when(j == pl.num_programs(1) - 1)
    def _():
        o_ref[...] = x_ref[...] + g_ref[0] * acc_ref[...]


def peer_dense(h, u, v, route, x, mods, cidx_tt, tt, gate_slot):
    T, D = h.shape
    s1, s2, e1, e2, th = route
    H = PEER_HEADS
    na = PEER_TE // PEER_NKEYS
    once = pl.Buffered(1)
    tok = pl.BlockSpec((H, PEER_NKEYS, tt), lambda i, j: (0, 0, i), pipeline_mode=once)
    sub = pl.BlockSpec((H, na, tt), lambda i, j: (0, j, i))
    return pl.pallas_call(
        _peer_dense_kernel,
        name="peer_dense",
        grid=(T // tt, PEER_N // PEER_TE),
        in_specs=[pl.BlockSpec((tt, D), lambda i, j: (i, 0), pipeline_mode=once),
                  pl.BlockSpec((PEER_TE, D), lambda i, j: (j, 0)),
                  pl.BlockSpec((PEER_TE, D), lambda i, j: (j, 0)),
                  sub, tok, sub, tok,
                  pl.BlockSpec((H, 1, tt), lambda i, j: (0, 0, i), pipeline_mode=once),
                  pl.BlockSpec((tt, D), lambda i, j: (i, 0), pipeline_mode=once),
                  pl.BlockSpec((1, 1, D), lambda i, j: (cidx_tt(i) * 6 + gate_slot, 0, 0))],
        out_specs=pl.BlockSpec((tt, D), lambda i, j: (i, 0)),
        out_shape=jax.ShapeDtypeStruct((T, D), F32),
        scratch_shapes=[pltpu.VMEM((PEER_TE, tt), F32),
                        pltpu.VMEM((PEER_TE, tt), BF16),
                        pltpu.VMEM((tt, D), F32)],
        compiler_params=_params("parallel", "arbitrary"),
    )(h, u, v, s1, s2, e1, e2, th, x, mods)


HY_HID = 64
HY_BANDS = (HY_EMB - 1) // 2
HY_FREQ_CHUNK = 512


def _hy_filter_kernel(w1t_ref, w1c_ref, w1s_ref, b1_ref, w2_ref, b2_ref, w3_ref, absd_ref, sum_ref, dif_ref, *, L):
    tl = sum_ref.shape[0]
    W = HY_WIDTH
    pos_i = pl.program_id(0) * tl + lax.broadcasted_iota(jnp.int32, (tl, 1), 0)
    pos = pos_i.astype(F32)
    t01 = pos * (1.0 / (L - 1))
    w = (2.0 * math.pi) * pos / L
    band = lax.broadcasted_iota(jnp.int32, (1, HY_BANDS), 1).astype(F32)
    f = 1e-4 + band * ((HY_BANDS - 1 - 1e-4) / (HY_BANDS - 1))
    fw = f * w
    pre = (t01 * w1t_ref[...] + jnp.dot(jnp.cos(fw), w1c_ref[...], preferred_element_type=F32)
           - jnp.dot(jnp.sin(fw), w1s_ref[...], preferred_element_type=F32) + b1_ref[...])
    h = jnp.sin(pre)
    h = jnp.sin(jnp.dot(h, w2_ref[...], preferred_element_type=F32) + b2_ref[...])
    h = jnp.dot(h, w3_ref[...], preferred_element_type=F32)
    window = jnp.exp(-t01 * absd_ref[...])
    for o in range(HY_ORDER):
        hf = h[:, (2 * o) * W:(2 * o + 1) * W] * window
        hb = jnp.where(pos_i == 0, 0.0, h[:, (2 * o + 1) * W:(2 * o + 2) * W] * window)
        sum_ref[:, o * W:(o + 1) * W] = (hf + hb).astype(sum_ref.dtype)
        dif_ref[:, o * W:(o + 1) * W] = (hf - hb).astype(dif_ref.dtype)


def hyena_filter_terms(L, lp):
    W = HY_WIDTH
    tl = _pick(L, (256, 128))
    max_decay = math.log(HY_DECAY_TARGET) / HY_FAST_DECAY
    min_decay = math.log(HY_DECAY_TARGET) / HY_SLOW_DECAY
    absd = jnp.abs(jnp.linspace(min_decay, max_decay, W, dtype=F32)).reshape(1, W)
    w1 = lp["hy_w1"]
    full = lambda a: pl.BlockSpec(a.shape, lambda i: (0,) * a.ndim)
    args = (w1[0:1], w1[1:1 + HY_BANDS], w1[1 + HY_BANDS:], lp["hy_b1"].reshape(1, -1), lp["hy_w2"],
            lp["hy_b2"].reshape(1, -1), lp["hy_w3"], absd)
    out = jax.ShapeDtypeStruct((L, HY_ORDER * W), BF16)
    return pl.pallas_call(
        functools.partial(_hy_filter_kernel, L=L),
        grid=(L // tl,),
        in_specs=[full(a) for a in args],
        out_specs=[pl.BlockSpec((tl, HY_ORDER * W), lambda i: (i, 0))] * 2,
        out_shape=[out, out],
        compiler_params=_params("parallel"),
    )(*args)


def _dft_matrices(L):
    k = jnp.arange(L, dtype=jnp.int32)
    m = (k[:, None] * k[None, :]) % (2 * L)
    ang = m.astype(F32) * (math.pi / L)
    return jnp.stack([jnp.cos(ang), -jnp.sin(ang)]).astype(BF16)


def _hy_conv_kernel(v_ref, x1_ref, x2_ref, cv_ref, c1_ref, c2_ref, f_ref, ka_ref, kb_ref, ks_ref, bias_ref, o_ref,
                    y_ref, yb_ref, conv_ref):
    L = v_ref.shape[1]
    N = 2 * L
    tw = o_ref.shape[2]
    row = lax.broadcasted_iota(jnp.int32, (L, 1), 0)
    nyq = jnp.where(row % 2 == 0, 1.0, -1.0)
    fchunk = min(L, HY_FREQ_CHUNK)

    def short_conv(x_ref, c_ref):
        x = x_ref[0].astype(F32)
        prev = jnp.where(row >= 1, pltpu.roll(x, 1, 0), 0.0)
        nxt = jnp.where(row <= L - 2, pltpu.roll(x, L - 1, 0), 0.0)
        return prev * c_ref[0:1, :] + x * c_ref[1:2, :] + nxt * c_ref[2:3, :]

    y_ref[...] = short_conv(v_ref, cv_ref)
    for o, (x_ref, c_ref) in enumerate(((x1_ref, c1_ref), (x2_ref, c2_ref))):
        cols = slice(o * tw, (o + 1) * tw)
        y = y_ref[...]
        k_nyq = jnp.sum(nyq * ks_ref[:, cols].astype(F32), axis=0, keepdims=True)
        u_nyq = jnp.sum(nyq * y, axis=0, keepdims=True)
        conv_ref[...] = nyq * (u_nyq * k_nyq * (1.0 / N)) + bias_ref[o:o + 1, :] * y
        yb_ref[...] = y.astype(BF16)

        def freq_chunk(i, carry):
            f0 = pl.multiple_of(i * fchunk, fchunk)
            fr = pl.ds(f0, fchunk)
            yb = yb_ref[...]
            ka = ka_ref[fr, cols].astype(F32)
            kb = kb_ref[fr, cols].astype(F32)
            ua = jnp.dot(f_ref[0, fr, :], yb, preferred_element_type=F32)
            ub = jnp.dot(f_ref[1, fr, :], yb, preferred_element_type=F32)
            k_idx = f0 + lax.broadcasted_iota(jnp.int32, (fchunk, 1), 0)
            sc = jnp.where(k_idx == 0, 1.0 / N, 2.0 / N)
            ya = (sc * (ua * ka - ub * kb)).astype(BF16)
            yb2 = (sc * (ua * kb + ub * ka)).astype(BF16)
            conv_ref[...] += (jnp.dot(f_ref[0, :, fr], ya, preferred_element_type=F32)
                              + jnp.dot(f_ref[1, :, fr], yb2, preferred_element_type=F32))
            return carry

        lax.fori_loop(0, L // fchunk, freq_chunk, 0)
        y_ref[...] = short_conv(x_ref, c_ref) * conv_ref[...]
    o_ref[0] = y_ref[...].astype(o_ref.dtype)


def hyena(hy_raw, B, b_off, lp):
    _, L, _ = hy_raw.shape
    W = HY_WIDTH
    tw = 256
    nw = W // tw
    fsum, fdif = hyena_filter_terms(L, lp)
    F = lp["dft"][L]
    ka = matmul(F[0], fsum, BF16)
    kb = matmul(F[1], fdif, BF16)
    regroup = lambda a: a.reshape(L, HY_ORDER, nw, tw).transpose(0, 2, 1, 3).reshape(L, nw * HY_ORDER * tw)
    ka, kb, ks = regroup(ka), regroup(kb), regroup(fsum)
    cw = lp["hy_conv"]
    xspec = lambda g: pl.BlockSpec((1, L, tw), lambda j, b: (b + b_off, 0, g * nw + j))
    cspec = lambda g: pl.BlockSpec((HY_SHORT, tw), lambda j, b: (0, g * nw + j))
    kspec = pl.BlockSpec((L, HY_ORDER * tw), lambda j, b: (0, j), pipeline_mode=pl.Buffered(1))
    return pl.pallas_call(
        _hy_conv_kernel,
        name="hy_conv",
        grid=(nw, B),
        in_specs=[xspec(0), xspec(1), xspec(2), cspec(0), cspec(1), cspec(2),
                  pl.BlockSpec((2, L, L), lambda j, b: (0, 0, 0), pipeline_mode=pl.Buffered(1)),
                  kspec, kspec, kspec,
                  pl.BlockSpec((HY_ORDER, tw), lambda j, b: (0, j))],
        out_specs=pl.BlockSpec((1, L, tw), lambda j, b: (b, 0, j)),
        out_shape=jax.ShapeDtypeStruct((B, L, W), BF16),
        scratch_shapes=[pltpu.VMEM((L, tw), F32), pltpu.VMEM((L, tw), BF16), pltpu.VMEM((L, tw), F32)],
        compiler_params=_params("parallel", "parallel"),
    )(hy_raw, hy_raw, hy_raw, cw, cw, cw, F, ka, kb, ks, lp["hy_bias"])


def _dn_prep_kernel(x_ref, w_ref, o_ref, *, n_q_tiles):
    L = x_ref.shape[1]
    tc = x_ref.shape[2]
    x = x_ref[0].astype(F32)
    row = lax.broadcasted_iota(jnp.int32, (L, 1), 0)
    half = DN_CONV // 2
    acc = x * w_ref[half:half + 1, :]
    for d in range(-half, half + 1):
        if d == 0:
            continue
        shifted = pltpu.roll(x, (-d) % L, 0)
        valid = jnp.logical_and(row + d >= 0, row + d <= L - 1)
        acc = acc + jnp.where(valid, shifted, 0.0) * w_ref[half + d:half + d + 1, :]
    y = acc * jax.nn.sigmoid(acc)
    j = pl.program_id(1)
    for g in range(tc // DN_DK):
        yg = y[:, g * DN_DK:(g + 1) * DN_DK]
        inv = lax.rsqrt(jnp.sum(yg * yg, axis=-1, keepdims=True) + EPS)
        fac = jnp.where(j < n_q_tiles, inv * DN_DK ** -0.5, jnp.where(j < 2 * n_q_tiles, inv, 1.0))
        o_ref[0, :, g * DN_DK:(g + 1) * DN_DK] = yg * fac


def dn_prep(qkv_raw, B, b_off, conv_w):
    _, L, CH = qkv_raw.shape
    tc = 256
    return pl.pallas_call(
        functools.partial(_dn_prep_kernel, n_q_tiles=DN_HEADS * DN_DK // tc),
        grid=(B, CH // tc),
        in_specs=[pl.BlockSpec((1, L, tc), lambda b, j: (b + b_off, 0, j)),
                  pl.BlockSpec((DN_CONV, tc), lambda b, j: (0, j))],
        out_specs=pl.BlockSpec((1, L, tc), lambda b, j: (b, 0, j)),
        out_shape=jax.ShapeDtypeStruct((B, L, CH), F32),
        compiler_params=_params("parallel", "parallel"),
    )(qkv_raw, conv_w)


def _dn_gate_kernel(ab_ref, alog_ref, dt_ref, o_ref):
    n = 2 * DN_HEADS
    a = ab_ref[:, :n] + dt_ref[...]
    softplus = jnp.maximum(a, 0.0) + jnp.log1p(jnp.exp(-jnp.abs(a)))
    o_ref[:, :n] = -jnp.exp(alog_ref[...]) * softplus
    o_ref[:, n:] = jax.nn.sigmoid(ab_ref[:, n:])


def dn_gates(p_ab, a_log, dt_bias):
    T, n2 = p_ab.shape
    tm = _pick(T, (2048, 1024, 512, 256, 128))
    n = 2 * DN_HEADS
    return pl.pallas_call(
        _dn_gate_kernel,
        grid=(T // tm,),
        in_specs=[pl.BlockSpec((tm, n2), lambda i: (i, 0)),
                  pl.BlockSpec((1, n), lambda i: (0, 0)),
                  pl.BlockSpec((1, n), lambda i: (0, 0))],
        out_specs=pl.BlockSpec((tm, n2), lambda i: (i, 0)),
        out_shape=jax.ShapeDtypeStruct((T, n2), F32),
        compiler_params=_params("parallel"),
    )(p_ab, a_log.reshape(1, n), dt_bias.reshape(1, n))


def _split3(x):
    hi = x.astype(BF16)
    r1 = x - hi.astype(F32)
    mid = r1.astype(BF16)
    lo = (r1 - mid.astype(F32)).astype(BF16)
    return hi, mid, lo


def _dn_chunk_kernel(qf_ref, qb_ref, gcf_ref, gcb_ref, grf_ref, grb_ref, s0_ref, of_ref, ob_ref, sout_ref, s_ref):
    n = pl.program_id(1)
    C = DN_CHUNK
    H = DN_HEADS
    nt = (((1,), (1,)), ((), ()))

    @pl.when(n == 0)
    def _():
        s_ref[...] = s0_ref[0]

    r = lax.broadcasted_iota(jnp.int32, (C, C), 0)
    c = lax.broadcasted_iota(jnp.int32, (C, C), 1)
    eye = (r == c).astype(F32)

    def mm(x, y, dims=None):
        x, y = x.astype(BF16), y.astype(BF16)
        if dims is None:
            return jnp.dot(x, y, preferred_element_type=F32)
        return lax.dot_general(x, y, dims, preferred_element_type=F32)

    incl, strict, gcol, gc_col, gc_row, g_tot = [], [], [], [], [], []
    for d, (gc_ref, gr_ref) in enumerate(((gcf_ref, grf_ref), (gcb_ref, grb_ref))):
        lag = (r - c) if d == 0 else (c - r)
        incl.append(lag >= 0)
        strict.append(lag > 0)
        tri = jnp.where(incl[d], 1.0, 0.0).astype(BF16)
        g_c = gc_ref[0, 0]
        g_r = gr_ref[0, 0, 0]
        gcol.append(g_c)
        gc_col.append(sum(jnp.dot(tri, p, preferred_element_type=F32) for p in _split3(g_c)))
        row = sum(lax.dot_general(p, tri, nt, preferred_element_type=F32) for p in _split3(g_r))
        gc_row.append(row)
        g_tot.append(row[:, C - 1:C] if d == 0 else row[:, 0:1])

    chains = [(d, h) for d in range(2) for h in range(H)]
    idx = range(len(chains))
    qkv = (qf_ref, qb_ref)
    q = [qkv[d][0, :, h * DN_DK:(h + 1) * DN_DK] for d, h in chains]
    k = [qkv[d][0, :, (H + h) * DN_DK:(H + h + 1) * DN_DK] for d, h in chains]
    v = [qkv[d][0, :, 2 * H * DN_DK + h * DN_DV:2 * H * DN_DK + (h + 1) * DN_DV] for d, h in chains]
    gc = [gc_col[d][:, h:h + 1] for d, h in chains]
    beta = [gcol[d][:, H + h:H + h + 1] for d, h in chains]
    g_last = [g_tot[d][h:h + 1, :] for d, h in chains]
    eg = [jnp.exp(gc[i]) for i in idx]
    decay = [jnp.exp(jnp.where(incl[d], gc[i] - gc_row[d][h:h + 1, :], NEG_BIG)) for i, (d, h) in enumerate(chains)]
    kk = [mm(k[i], k[i], nt) for i in idx]
    qk = [mm(q[i], k[i], nt) for i in idx]
    a = [jnp.where(strict[d], beta[i] * kk[i] * decay[i], 0.0) for i, (d, h) in enumerate(chains)]
    qk = [jnp.where(incl[d], qk[i] * decay[i], 0.0) for i, (d, h) in enumerate(chains)]
    inv = None
    for l in range(C.bit_length() - 1):
        couple = jnp.logical_and((r >> (l + 1)) == (c >> (l + 1)), (r >> l) != (c >> l))
        a_l = [jnp.where(couple, a[i], 0.0) for i in idx]
        if inv is None:
            inv = [eye - a_l[i] for i in idx]
        else:
            t = [mm(inv[i], a_l[i]) for i in idx]
            inv = [inv[i] - mm(t[i], inv[i]) for i in idx]
    rhs = [jnp.concatenate([v[i] * beta[i], k[i] * (beta[i] * eg[i])], axis=1) for i in idx]
    sol = [mm(inv[i], rhs[i]) for i in idx]
    S = [s_ref[d, h] for d, h in chains]
    v_new = [sol[i][:, :DN_DV] - mm(sol[i][:, DN_DV:], S[i]) for i in idx]
    o = [mm(q[i] * eg[i], S[i]) + mm(qk[i], v_new[i]) for i in idx]
    upd = [mm(k[i] * jnp.exp(g_last[i] - gc[i]), v_new[i], (((0,), (0,)), ((), ()))) for i in idx]
    o_refs = (of_ref, ob_ref)
    for i, (d, h) in enumerate(chains):
        o_refs[d][0, :, h * DN_DV:(h + 1) * DN_DV] = o[i]
        s_ref[d, h] = S[i] * jnp.exp(g_last[i]) + upd[i]

    @pl.when(n == pl.num_programs(1) - 1)
    def _():
        sout_ref[0] = s_ref[...]


def dn_scan(qkv, gates, s0):
    B, L, CH = qkv.shape
    C, H = DN_CHUNK, DN_HEADS
    N = L // C
    g4 = gates.reshape(B, L, 2, 2, H)
    gcol = g4.transpose(0, 3, 1, 2, 4).reshape(B, 2, L, 2 * H)
    grow = gcol.reshape(B, 2, N, C, 2 * H).transpose(0, 1, 2, 4, 3)
    rev = lambda n: N - 1 - n
    out = jax.ShapeDtypeStruct((B, L, H * DN_DV), F32)
    return pl.pallas_call(
        _dn_chunk_kernel,
        name="dn_chunk",
        grid=(B, N),
        in_specs=[pl.BlockSpec((1, C, CH), lambda b, n: (b, n, 0)),
                  pl.BlockSpec((1, C, CH), lambda b, n: (b, rev(n), 0)),
                  pl.BlockSpec((1, 1, C, 2 * H), lambda b, n: (b, 0, n, 0)),
                  pl.BlockSpec((1, 1, C, 2 * H), lambda b, n: (b, 1, rev(n), 0)),
                  pl.BlockSpec((1, 1, 1, 2 * H, C), lambda b, n: (b, 0, n, 0, 0)),
                  pl.BlockSpec((1, 1, 1, 2 * H, C), lambda b, n: (b, 1, rev(n), 0, 0)),
                  pl.BlockSpec((1, 2, H, DN_DK, DN_DV), lambda b, n: (b, 0, 0, 0, 0))],
        out_specs=[pl.BlockSpec((1, C, H * DN_DV), lambda b, n: (b, n, 0)),
                   pl.BlockSpec((1, C, H * DN_DV), lambda b, n: (b, rev(n), 0)),
                   pl.BlockSpec((1, 2, H, DN_DK, DN_DV), lambda b, n: (b, 0, 0, 0, 0))],
        out_shape=[out, out, jax.ShapeDtypeStruct((B, 2, H, DN_DK, DN_DV), F32)],
        scratch_shapes=[pltpu.VMEM((2, H, DN_DK, DN_DV), F32)],
        compiler_params=_params("parallel", "arbitrary"),
    )(qkv, qkv, gcol, gcol, grow, grow, s0)


def _dn_out_kernel(of_ref, ob_ref, z_ref, g_ref, o_ref):
    for h in range(DN_HEADS):
        cols = slice(h * DN_DV, (h + 1) * DN_DV)
        o = of_ref[0, :, cols] + ob_ref[0, :, cols]
        z = z_ref[0, :, cols].astype(F32)
        o_ref[0, :, cols] = (_rms(o, g_ref[...]) * (z * jax.nn.sigmoid(z))).astype(o_ref.dtype)


def dn_output(o_f, o_b, z, b_off, g):
    B, L, W = o_f.shape
    tm = _pick(L, (512, 256, 128))
    spec = pl.BlockSpec((1, tm, W), lambda b, i: (b, i, 0))
    return pl.pallas_call(
        _dn_out_kernel,
        grid=(B, L // tm),
        in_specs=[spec, spec,
                  pl.BlockSpec((1, tm, W), lambda b, i: (b + b_off, i, 0)),
                  pl.BlockSpec((1, DN_DV), lambda b, i: (0, 0))],
        out_specs=spec,
        out_shape=jax.ShapeDtypeStruct((B, L, W), BF16),
        compiler_params=_params("parallel", "parallel"),
    )(o_f, o_b, z, g.reshape(1, DN_DV))


def gated_deltanet(qkv_raw, z, B, b_off, gates, lp, s0):
    qkv = dn_prep(qkv_raw, B, b_off, lp["dn_conv"])
    o_f, o_b, s_fin = dn_scan(qkv, gates, s0)
    return dn_output(o_f, o_b, z, b_off, lp["dn_out_norm"]), s_fin


def _axial_rope(x):
    L = x.shape[1]
    half = x.shape[-1] // 2
    t = jnp.arange(L)
    inv = ROPE_THETA ** (-jnp.arange(0, half, 2, dtype=F32) / half)
    out = []
    for pos, xa in ((t // GRID_W, x[..., :half]), (t % GRID_W, x[..., half:])):
        ang = pos.astype(F32)[:, None] * inv[None, :]
        cos, sin = jnp.cos(ang)[:, None, :], jnp.sin(ang)[:, None, :]
        x1, x2 = xa[..., : half // 2], xa[..., half // 2:]
        out += [x1 * cos - x2 * sin, x2 * cos + x1 * sin]
    return jnp.concatenate(out, axis=-1)


def _pad_heads(nope, rope):
    B, L, H, _ = nope.shape
    rope = jnp.broadcast_to(rope, (B, L, H, MLA_ROPE))
    pad = jnp.zeros((B, L, H, MLA_QK_PAD - MLA_NOPE - MLA_ROPE), nope.dtype)
    return jnp.concatenate([nope, rope, pad], axis=-1).reshape(B, L, H * MLA_QK_PAD).astype(BF16)


def _layer(x, lp, mods, dims, caches):
    Bc, Lc, Bl, Ll = dims
    Tc, Tl = Bc * Lc, Bl * Ll
    T = Tc + Tl
    tm = _pick(math.gcd(Tc, Ll), (1024, 512, 256, 128))

    def make_cidx(tile):
        nct, tpl = Tc // tile, Ll // tile
        return lambda i: jnp.where(i < nct, 0, 1 + (i - nct) // tpl)

    cidx = make_cidx(tm)
    h = modulate(x, lp["norm1_g"], mods, cidx, tm, 0, 1)

    w_in = lp["w_in"]
    cuts = np.cumsum((0,) + IN_SIZES)
    col = lambda a, b: w_in[:, cuts[a]:cuts[b]].astype(BF16)
    p_mla = matmul(h, col(0, 3))
    p_dn = matmul(h, col(3, 4), BF16)
    p_z = matmul(h, col(4, 5), BF16)
    p_ab = matmul(h, col(5, 7))
    p_hy = matmul(h, col(7, 8), BF16)
    p_na = matmul(h, col(8, 9), BF16)
    na_kv_ctx = matmul(h, w_in[:, cuts[8] + BRANCH_W:cuts[9]].astype(BF16), F32, rows=Tc)
    p_gate = matmul(h, col(9, 10), BF16)

    cq, ckv, krope = p_mla[:, :MLA_Q_RANK], p_mla[:, MLA_Q_RANK:MLA_Q_RANK + MLA_KV_RANK], p_mla[:, -MLA_ROPE:]

    cq_n = rmsnorm(cq, lp["mla_q_norm"], BF16)
    ckv_n = rmsnorm(ckv, lp["mla_kv_norm"], F32)
    q_all = matmul(cq_n, lp["mla_w_qb"].astype(BF16))
    w_kvb = lp["mla_w_kvb"].astype(BF16)
    kv_all = matmul(ckv_n.astype(BF16), w_kvb, BF16)
    ckv_ctx, krope_ctx, nak_ctx, nav_ctx, s_f0, s_b0 = caches
    kv_cache = matmul(ckv_ctx.reshape(-1, MLA_KV_RANK).astype(BF16), w_kvb, BF16)
    Lp = ckv_ctx.shape[1]

    def heads_q(t, B, L, rotate):
        t = t.reshape(B, L, MLA_HEADS, MLA_NOPE + MLA_ROPE)
        rope = t[..., MLA_NOPE:]
        if rotate:
            rope = _axial_rope(rope)
        return _pad_heads(t[..., :MLA_NOPE], rope)

    def heads_kv(kv, kr, B, L):
        kv = kv.reshape(B, L, MLA_HEADS, MLA_NOPE + MLA_V)
        return _pad_heads(kv[..., :MLA_NOPE], kr.reshape(B, L, 1, MLA_ROPE)), kv.reshape(B, L, -1)

    qc = heads_q(q_all[:Tc], Bc, Lc, False)
    kc, kvc = heads_kv(kv_all[:Tc], krope[:Tc], Bc, Lc)
    o_a_ctx = attention(qc, kc, kvc, MLA_HEADS, MLA_QK_PAD, MLA_V, MLA_SCALE, v_off=1, v_stride=2)
    ql = heads_q(q_all[Tc:], Bl, Ll, True)
    kl, kvl = heads_kv(kv_all[Tc:], _axial_rope(krope[Tc:].reshape(Bl, Ll, 1, MLA_ROPE)), Bl, Ll)
    kp, kvp = heads_kv(kv_cache, krope_ctx, Bl, Lp)
    o_a_lat = attention(ql, jnp.concatenate([kl, kp], axis=1), jnp.concatenate([kvl, kvp], axis=1),
                        MLA_HEADS, MLA_QK_PAD, MLA_V, MLA_SCALE, v_off=1, v_stride=2)
    o_a = jnp.concatenate([o_a_ctx.reshape(Tc, BRANCH_W), o_a_lat.reshape(Tl, BRANCH_W)])

    def seqs(p, L, start, B):
        if start % L == 0 and T % L == 0:
            return p.reshape(T // L, L, -1), start // L
        return p[start:start + B * L].reshape(B, L, -1), 0

    ctx_of = lambda p: seqs(p, Lc, 0, Bc)
    lat_of = lambda p: seqs(p, Ll, Tc, Bl)

    gates = dn_gates(p_ab, lp["dn_a_log"], lp["dn_dt_bias"])
    zero_state = jnp.zeros((Bc, 2, DN_HEADS, DN_DK, DN_DV), F32)
    (dn_c, off_c), (dn_l, off_l) = ctx_of(p_dn), lat_of(p_dn)
    o_b_ctx, s_ctx = gated_deltanet(dn_c, ctx_of(p_z)[0], Bc, off_c, gates[:Tc].reshape(Bc, Lc, -1), lp, zero_state)
    o_b_lat, _ = gated_deltanet(dn_l, lat_of(p_z)[0], Bl, off_l, gates[Tc:].reshape(Bl, Ll, -1), lp,
                                jnp.stack([s_f0, s_b0], axis=1))
    s_f, s_b = s_ctx[:, 0], s_ctx[:, 1]
    o_b = jnp.concatenate([o_b_ctx.reshape(Tc, BRANCH_W), o_b_lat.reshape(Tl, BRANCH_W)])

    (hy_c, off_c), (hy_l, off_l) = ctx_of(p_hy), lat_of(p_hy)
    o_c = jnp.concatenate([hyena(hy_c, Bc, off_c, lp).reshape(Tc, BRANCH_W),
                           hyena(hy_l, Bl, off_l, lp).reshape(Tl, BRANCH_W)])

    (na_c, off_c), (na_l, off_l) = ctx_of(p_na), lat_of(p_na)
    o_d_ctx = attention(na_c, na_c, na_c, NA_HEADS, NA_HD, NA_HD, NA_SCALE, k_off=NA_HEADS, v_off=2 * NA_HEADS,
                        batch=Bc, b_off=off_c)
    o_d_lat = neighbourhood_attention(na_l, Bl, off_l, nak_ctx.reshape(Bl, Lp, -1).astype(BF16),
                                      nav_ctx.reshape(Bl, Lp, -1).astype(BF16), lp["na_bias"])
    o_d = jnp.concatenate([o_d_ctx.reshape(Tc, BRANCH_W), o_d_lat.reshape(Tl, BRANCH_W)])

    merged = merge_branches((o_a, o_b, o_c, o_d), p_gate, lp["w_branch"].astype(BF16), tm)
    x = matmul_gated_residual(merged, lp["w_out"].astype(BF16), x, mods, cidx, tm, 2)

    h2 = modulate(x, lp["norm2_g"], mods, cidx, tm, 3, 4)
    pq = matmul(h2, lp["peer_wq"].astype(BF16))
    route = peer_route(pq, lp["peer_keys"])
    tt = _pick(math.gcd(Tc, Ll), (512, 256, 128))
    x = peer_dense(h2, lp["peer_u"].astype(BF16), lp["peer_v"].astype(BF16), route, x, mods, make_cidx(tt), tt, 5)

    ctx_out = (ckv_n[:Tc].reshape(Bc, Lc, MLA_KV_RANK), krope[:Tc].reshape(Bc, Lc, MLA_ROPE),
               na_kv_ctx[:, :BRANCH_W].reshape(Bc, Lc, NA_HEADS, NA_HD),
               na_kv_ctx[:, BRANCH_W:].reshape(Bc, Lc, NA_HEADS, NA_HD), s_f, s_b)
    return x, ctx_out


def kernel(x_prompt, x_sample, cache_mla_ckv, cache_mla_krope, cache_na_k, cache_na_v, state_dn_fwd, state_dn_bwd, c, c_ctx, norm1_g, w_ada, b_ada, w_in, mla_q_norm, mla_w_qb, mla_kv_norm, mla_w_kvb, dn_conv, dn_a_log, dn_dt_bias, dn_out_norm, hy_conv, hy_w1, hy_b1, hy_w2, hy_b2, hy_w3, hy_bias, na_rpb, w_branch, w_out, norm2_g, peer_wq, peer_keys, peer_u, peer_v, final_g):
    Bc, Lc, D = x_prompt.shape
    Bl, Ll, _ = x_sample.shape
    depth = w_in.shape[0]
    Tc = Bc * Lc
    x = jnp.concatenate([x_prompt.reshape(Tc, D), x_sample.reshape(Bl * Ll, D)])
    n_cond = 1 + Bl
    cond_rows = -(-n_cond // SUBLANES) * SUBLANES
    cond = jnp.concatenate([c_ctx[None], c, jnp.zeros((cond_rows - n_cond, D), F32)])

    names = ("norm1_g", "w_in", "mla_q_norm", "mla_w_qb", "mla_kv_norm", "mla_w_kvb", "dn_conv", "dn_a_log",
             "dn_dt_bias", "dn_out_norm", "hy_conv", "hy_w1", "hy_b1", "hy_w2", "hy_b2", "hy_w3", "hy_bias",
             "na_rpb", "w_branch", "w_out", "norm2_g", "peer_wq", "peer_keys", "peer_u", "peer_v")
    vals = (norm1_g, w_in, mla_q_norm, mla_w_qb, mla_kv_norm, mla_w_kvb, dn_conv, dn_a_log, dn_dt_bias,
            dn_out_norm, hy_conv, hy_w1, hy_b1, hy_w2, hy_b2, hy_w3, hy_bias, na_rpb, w_branch, w_out, norm2_g,
            peer_wq, peer_keys, peer_u, peer_v)
    dft = {L: _dft_matrices(L) for L in {Lc, Ll}}
    na_bias = _na_bias_tables(na_rpb.reshape((depth * NA_HEADS,) + na_rpb.shape[2:]), Ll // GRID_W)
    na_bias = na_bias.reshape((depth, NA_HEADS) + na_bias.shape[1:])
    ctx = []
    for l in range(depth):
        lp = {n: v[l] for n, v in zip(names, vals)}
        lp["dft"] = dft
        lp["na_bias"] = na_bias[l]
        mods = ada_modulation(cond, w_ada[l], b_ada[l]).reshape(cond_rows * 6, 1, D)
        caches = (cache_mla_ckv[:, l], cache_mla_krope[:, l], cache_na_k[:, l], cache_na_v[:, l],
                  state_dn_fwd[:, l], state_dn_bwd[:, l])
        x, ctx_out = _layer(x, lp, mods, (Bc, Lc, Bl, Ll), caches)
        ctx.append(ctx_out)
    y = rmsnorm(x, final_g)
    stack = lambda k: jnp.stack([t[k] for t in ctx], axis=1)
    return (y[:Tc].reshape(Bc, Lc, D), y[Tc:].reshape(Bl, Ll, D), stack(0), stack(1), stack(2), stack(3),
            stack(4), stack(5))
```

```python
import functools
import math

import numpy as np
import jax
import jax.numpy as jnp
from jax import lax
from jax.experimental import pallas as pl
from jax.experimental.pallas import tpu as pltpu

F32 = jnp.float32
BF16 = jnp.bfloat16

V7X_VMEM_BYTES = 64 * 1024 * 1024
VMEM_LIMIT = V7X_VMEM_BYTES - 8 * 1024 * 1024
LANES = 128
SUBLANES = 8

D_MODEL = 2048
GRID_W = 64
EPS = 1e-6
N_BRANCH = 4
BRANCH_W = D_MODEL // 2

MLA_HEADS = 8
MLA_NOPE = 128
MLA_ROPE = 64
MLA_V = BRANCH_W // MLA_HEADS
MLA_Q_RANK = D_MODEL // 4
MLA_KV_RANK = D_MODEL // 8
MLA_SCALE = (MLA_NOPE + MLA_ROPE) ** -0.5
ROPE_THETA = 10000.0

DN_HEADS = 8
DN_DK = 128
DN_DV = BRANCH_W // DN_HEADS
DN_CONV = 5
DN_CHUNK = 64
DN_CONV_CH = 2 * DN_HEADS * DN_DK + DN_HEADS * DN_DV

HY_WIDTH = BRANCH_W
HY_ORDER = 2
HY_SHORT = 3
HY_EMB = 33
HY_FAST_DECAY = 0.3
HY_SLOW_DECAY = 1.5
HY_DECAY_TARGET = 1e-2

NA_HEADS = 8
NA_HD = BRANCH_W // NA_HEADS
NA_KH = 8
NA_KW = 16
NA_SCALE = NA_HD ** -0.5
NA_QROWS = 4
NA_REGION = NA_KH + NA_QROWS
NEG_BIG = -1e30

PEER_HEADS = 8
PEER_NKEYS = 128
PEER_N = PEER_NKEYS * PEER_NKEYS
PEER_DKEY = 256
PEER_TOPK = 16

IN_SIZES = (MLA_Q_RANK, MLA_KV_RANK, MLA_ROPE, DN_CONV_CH, DN_HEADS * DN_DV, 2 * DN_HEADS, 2 * DN_HEADS,
            (HY_ORDER + 1) * HY_WIDTH, 3 * NA_HEADS * NA_HD, N_BRANCH * D_MODEL)


def _params(*sem):
    return pltpu.CompilerParams(dimension_semantics=sem, vmem_limit_bytes=VMEM_LIMIT)


def _pick(n, prefs):
    for p in prefs:
        if n % p == 0:
            return p
    return n


def _mm_kernel(x_ref, w_ref, o_ref):
    o_ref[...] = jnp.dot(x_ref[...], w_ref[...], preferred_element_type=F32).astype(o_ref.dtype)


def matmul(x, w, out_dtype=F32, tm=None, tn=None, rows=None):
    M, K = x.shape
    M = rows or M
    N = w.shape[1]
    tm = tm or _pick(M, (1024, 512, 256, 128))
    tn = tn or _pick(N, (512, 256, 128))
    return pl.pallas_call(
        _mm_kernel,
        grid=(M // tm, N // tn),
        in_specs=[pl.BlockSpec((tm, K), lambda i, j: (i, 0)),
                  pl.BlockSpec((K, tn), lambda i, j: (0, j))],
        out_specs=pl.BlockSpec((tm, tn), lambda i, j: (i, j)),
        out_shape=jax.ShapeDtypeStruct((M, N), out_dtype),
        compiler_params=_params("parallel", "parallel"),
    )(x, w)


def _ada_kernel(c_ref, w_ref, b_ref, o_ref):
    c = c_ref[...]
    o_ref[...] = jnp.dot(c * jax.nn.sigmoid(c), w_ref[...], preferred_element_type=F32) + b_ref[...]


def ada_modulation(cond, w_ada, b_ada):
    R, K = cond.shape
    N = w_ada.shape[1]
    tn = _pick(N, (1536, 1024, 512))
    return pl.pallas_call(
        _ada_kernel,
        grid=(N // tn,),
        in_specs=[pl.BlockSpec((R, K), lambda j: (0, 0)),
                  pl.BlockSpec((K, tn), lambda j: (0, j)),
                  pl.BlockSpec((1, tn), lambda j: (0, j))],
        out_specs=pl.BlockSpec((R, tn), lambda j: (0, j)),
        out_shape=jax.ShapeDtypeStruct((R, N), F32),
        compiler_params=_params("parallel"),
    )(cond, w_ada, b_ada.reshape(1, N))


def _rms(x, g):
    return x * lax.rsqrt(jnp.mean(x * x, axis=-1, keepdims=True) + EPS) * g


def _modulate_kernel(x_ref, g_ref, sh_ref, sc_ref, o_ref):
    y = _rms(x_ref[...], g_ref[...])
    o_ref[...] = (y * (1.0 + sc_ref[0]) + sh_ref[0]).astype(o_ref.dtype)


def modulate(x, g, mods, cidx, tm, shift_slot, scale_slot):
    T, D = x.shape
    return pl.pallas_call(
        _modulate_kernel,
        grid=(T // tm,),
        in_specs=[pl.BlockSpec((tm, D), lambda i: (i, 0)),
                  pl.BlockSpec((1, D), lambda i: (0, 0)),
                  pl.BlockSpec((1, 1, D), lambda i: (cidx(i) * 6 + shift_slot, 0, 0)),
                  pl.BlockSpec((1, 1, D), lambda i: (cidx(i) * 6 + scale_slot, 0, 0))],
        out_specs=pl.BlockSpec((tm, D), lambda i: (i, 0)),
        out_shape=jax.ShapeDtypeStruct((T, D), BF16),
        compiler_params=_params("parallel"),
    )(x, g.reshape(1, D), mods, mods)


def _rmsnorm_kernel(x_ref, g_ref, o_ref):
    o_ref[...] = _rms(x_ref[...].astype(F32), g_ref[...]).astype(o_ref.dtype)


def rmsnorm(x, g, out_dtype=F32):
    T, D = x.shape
    tm = _pick(T, (1024, 512, 256, 128))
    return pl.pallas_call(
        _rmsnorm_kernel,
        grid=(T // tm,),
        in_specs=[pl.BlockSpec((tm, D), lambda i: (i, 0)),
                  pl.BlockSpec((1, D), lambda i: (0, 0))],
        out_specs=pl.BlockSpec((tm, D), lambda i: (i, 0)),
        out_shape=jax.ShapeDtypeStruct((T, D), out_dtype),
        compiler_params=_params("parallel"),
    )(x, g.reshape(1, D))


def _merge_kernel(a_ref, b_ref, c_ref, d_ref, ga_ref, gb_ref, gc_ref, gd_ref, w_ref, o_ref):
    acc = None
    for n, (br, gl) in enumerate(((a_ref, ga_ref), (b_ref, gb_ref), (c_ref, gc_ref), (d_ref, gd_ref))):
        proj = jnp.dot(br[...], w_ref[n], preferred_element_type=F32)
        term = jax.nn.sigmoid(gl[...].astype(F32)) * proj
        acc = term if acc is None else acc + term
    o_ref[...] = acc.astype(o_ref.dtype)


def merge_branches(branches, gate_logits, w_branch, tm):
    T = branches[0].shape[0]
    tn = 512
    br_spec = pl.BlockSpec((tm, BRANCH_W), lambda i, j: (i, 0))
    nj = D_MODEL // tn
    gl_specs = [pl.BlockSpec((tm, tn), functools.partial(lambda i, j, n: (i, n * nj + j), n=n))
                for n in range(N_BRANCH)]
    return pl.pallas_call(
        _merge_kernel,
        grid=(T // tm, D_MODEL // tn),
        in_specs=[br_spec, br_spec, br_spec, br_spec, *gl_specs,
                  pl.BlockSpec((N_BRANCH, BRANCH_W, tn), lambda i, j: (0, 0, j))],
        out_specs=pl.BlockSpec((tm, tn), lambda i, j: (i, j)),
        out_shape=jax.ShapeDtypeStruct((T, D_MODEL), BF16),
        compiler_params=_params("parallel", "parallel"),
    )(*branches, gate_logits, gate_logits, gate_logits, gate_logits, w_branch)


def _mm_resid_kernel(m_ref, w_ref, x_ref, g_ref, o_ref):
    o_ref[...] = x_ref[...] + g_ref[0] * jnp.dot(m_ref[...], w_ref[...], preferred_element_type=F32)


def matmul_gated_residual(m, w, x, mods, cidx, tm, gate_slot):
    T, K = m.shape
    N = w.shape[1]
    tn = 512
    return pl.pallas_call(
        _mm_resid_kernel,
        grid=(T // tm, N // tn),
        in_specs=[pl.BlockSpec((tm, K), lambda i, j: (i, 0)),
                  pl.BlockSpec((K, tn), lambda i, j: (0, j)),
                  pl.BlockSpec((tm, tn), lambda i, j: (i, j)),
                  pl.BlockSpec((1, 1, tn), lambda i, j: (cidx(i) * 6 + gate_slot, 0, j))],
        out_specs=pl.BlockSpec((tm, tn), lambda i, j: (i, j)),
        out_shape=jax.ShapeDtypeStruct((T, N), F32),
        compiler_params=_params("parallel", "parallel"),
    )(m, w, x, mods)


def _attn_kernel(q_ref, k_ref, v_ref, o_ref, *, scale):
    s = lax.dot_general(q_ref[0], k_ref[0], (((1,), (1,)), ((), ())), preferred_element_type=F32) * scale
    p = jnp.exp(s - jnp.max(s, axis=-1, keepdims=True))
    l = jnp.sum(p, axis=-1, keepdims=True)
    o = jnp.dot(p.astype(BF16), v_ref[0], preferred_element_type=F32)
    o_ref[0] = (o / l).astype(o_ref.dtype)


def attention(q, k, v, heads, dqk, dv, scale, q_off=0, k_off=0, v_off=0, v_stride=1, k_stride=1, batch=None,
              b_off=0):
    B, Lq, _ = q.shape
    B = batch or B
    Lk = k.shape[1]
    tq = _pick(Lq, (256, 128))
    return pl.pallas_call(
        functools.partial(_attn_kernel, scale=scale),
        grid=(B, heads, Lq // tq),
        in_specs=[pl.BlockSpec((1, tq, dqk), lambda b, h, i: (b + b_off, i, q_off + h)),
                  pl.BlockSpec((1, Lk, dqk), lambda b, h, i: (b + b_off, 0, k_off + h * k_stride)),
                  pl.BlockSpec((1, Lk, dv), lambda b, h, i: (b + b_off, 0, v_off + h * v_stride))],
        out_specs=pl.BlockSpec((1, tq, dv), lambda b, h, i: (b, i, h)),
        out_shape=jax.ShapeDtypeStruct((B, Lq, heads * dv), BF16),
        compiler_params=_params("parallel", "parallel", "parallel"),
    )(q, k, v)


def _mla_attn_kernel(qn_ref, qr_ref, kn_ref, kr_ref, v_ref, o_ref):
    q = jnp.concatenate([qn_ref[0], qr_ref[0, 0]], axis=1)
    k = jnp.concatenate([kn_ref[0], kr_ref[0]], axis=1)
    s = lax.dot_general(q, k, (((1,), (1,)), ((), ())), preferred_element_type=F32) * MLA_SCALE
    p = jnp.exp(s - jnp.max(s, axis=-1, keepdims=True))
    l = jnp.sum(p, axis=-1, keepdims=True)
    o = jnp.dot(p.astype(BF16), v_ref[0], preferred_element_type=F32)
    o_ref[0] = (o / l).astype(o_ref.dtype)


def mla_attention(q_nope, q_rope, kv, k_rope, B, q_off, kv_off):
    _, Lq, _ = q_nope.shape
    Lk = kv.shape[1]
    H = MLA_HEADS
    tq = _pick(Lq, (256, 128))
    return pl.pallas_call(
        _mla_attn_kernel,
        grid=(B, H, Lq // tq),
        in_specs=[pl.BlockSpec((1, tq, MLA_NOPE), lambda b, h, i: (b + q_off, i, h)),
                  pl.BlockSpec((1, 1, tq, MLA_ROPE), lambda b, h, i: (b, h, i, 0)),
                  pl.BlockSpec((1, Lk, MLA_NOPE), lambda b, h, i: (b + kv_off, 0, 2 * h)),
                  pl.BlockSpec((1, Lk, MLA_ROPE), lambda b, h, i: (b, 0, 0)),
                  pl.BlockSpec((1, Lk, MLA_V), lambda b, h, i: (b + kv_off, 0, 2 * h + 1))],
        out_specs=pl.BlockSpec((1, tq, MLA_V), lambda b, h, i: (b, i, h)),
        out_shape=jax.ShapeDtypeStruct((B, Lq, H * MLA_V), BF16),
        compiler_params=_params("parallel", "parallel", "parallel"),
    )(q_nope, q_rope, kv, k_rope, kv)


def _na_kernel(q_ref, k_ref, v_ref, kc_ref, vc_ref, bias_ref, o_ref, *, rows):
    rb = pl.program_id(2)
    nkeys = NA_REGION * GRID_W
    start = jnp.clip(rb * NA_QROWS - NA_KH // 2, 0, rows - NA_REGION) * GRID_W
    start = pl.multiple_of(start, GRID_W)
    q = q_ref[0]
    kr = k_ref[0, pl.ds(start, nkeys), :]
    vr = v_ref[0, pl.ds(start, nkeys), :]
    nt = (((1,), (1,)), ((), ()))
    s_win = lax.dot_general(q, kr, nt, preferred_element_type=F32) * NA_SCALE + bias_ref[0, 0]
    s_ctx = lax.dot_general(q, kc_ref[0], nt, preferred_element_type=F32) * NA_SCALE
    m = jnp.maximum(jnp.max(s_win, axis=-1, keepdims=True), jnp.max(s_ctx, axis=-1, keepdims=True))
    p_win = jnp.exp(s_win - m)
    p_ctx = jnp.exp(s_ctx - m)
    l = jnp.sum(p_win, axis=-1, keepdims=True) + jnp.sum(p_ctx, axis=-1, keepdims=True)
    o = (jnp.dot(p_win.astype(BF16), vr, preferred_element_type=F32)
         + jnp.dot(p_ctx.astype(BF16), vc_ref[0], preferred_element_type=F32))
    o_ref[0] = (o / l).astype(o_ref.dtype)


def _na_bias_tables(rpb, rows):
    H = rpb.shape[0]
    n_dr, n_dc = 2 * NA_KH - 1, 2 * NA_KW - 1
    assert rows >= NA_REGION + NA_QROWS and rows % NA_QROWS == 0
    nblk = rows // NA_QROWS
    span = 2 * GRID_W
    lo = GRID_W - NA_KW
    v = jnp.pad(rpb.astype(F32), ((0, 0), (0, 0), (lo, span - lo - n_dc)), constant_values=NEG_BIG)
    skew = jnp.tile(v, (1, 1, GRID_W))[..., :GRID_W * (span - 1)].reshape(H, n_dr, GRID_W, span - 1)
    band = skew[..., GRID_W - 1:]
    c = np.arange(GRID_W)[:, None]
    kc = np.arange(GRID_W)[None, :]
    col0 = np.clip(c - NA_KW // 2, 0, GRID_W - NA_KW)
    col_ok = (kc >= col0) & (kc < col0 + NA_KW)
    band = jnp.where(col_ok, band, NEG_BIG)
    band = jnp.concatenate([band, jnp.full((H, 1, GRID_W, GRID_W), NEG_BIG, F32)], axis=1)
    tiles = []
    for rb in (0, 1, nblk - 1):
        start = int(np.clip(rb * NA_QROWS - NA_KH // 2, 0, rows - NA_REGION))
        for qr in range(NA_QROWS):
            r = rb * NA_QROWS + qr
            row0 = int(np.clip(r - NA_KH // 2, 0, rows - NA_KH))
            for j in range(NA_REGION):
                kr = start + j
                tiles.append(kr - r + NA_KH - 1 if row0 <= kr < row0 + NA_KH else n_dr)
    t = jnp.take(band, np.asarray(tiles, np.int32), axis=1).reshape(H, 3, NA_QROWS, NA_REGION, GRID_W, GRID_W)
    return t.transpose(0, 1, 2, 4, 3, 5).reshape(H, 3, NA_QROWS * GRID_W, NA_REGION * GRID_W)


def neighbourhood_attention(na_in, B, b_off, k_ctx, v_ctx, bias):
    _, L, _ = na_in.shape
    Lc = k_ctx.shape[1]
    rows = L // GRID_W
    nblk = rows // NA_QROWS
    tq = NA_QROWS * GRID_W
    nkeys = NA_REGION * GRID_W
    kind = lambda r: jnp.where(r == 0, 0, jnp.where(r == nblk - 1, 2, 1))
    H = NA_HEADS
    return pl.pallas_call(
        functools.partial(_na_kernel, rows=rows),
        grid=(B, H, nblk),
        in_specs=[pl.BlockSpec((1, tq, NA_HD), lambda b, h, r: (b + b_off, r, h)),
                  pl.BlockSpec((1, L, NA_HD), lambda b, h, r: (b + b_off, 0, H + h)),
                  pl.BlockSpec((1, L, NA_HD), lambda b, h, r: (b + b_off, 0, 2 * H + h)),
                  pl.BlockSpec((1, Lc, NA_HD), lambda b, h, r: (b, 0, h)),
                  pl.BlockSpec((1, Lc, NA_HD), lambda b, h, r: (b, 0, h)),
                  pl.BlockSpec((1, 1, tq, nkeys), lambda b, h, r: (h, kind(r), 0, 0))],
        out_specs=pl.BlockSpec((1, tq, NA_HD), lambda b, h, r: (b, r, h)),
        out_shape=jax.ShapeDtypeStruct((B, L, H * NA_HD), BF16),
        compiler_params=_params("parallel", "parallel", "parallel"),
    )(na_in, na_in, na_in, k_ctx, v_ctx, bias)


def _top_rows(works, k):
    outs = [[] for _ in works]
    for _ in range(k):
        mx = [jnp.max(w, axis=0, keepdims=True) for w in works]
        for out, m in zip(outs, mx):
            out.append(m)
        works = [jnp.where(w == m, -jnp.inf, w) for w, m in zip(works, mx)]
    return outs


_PEER_PAIRS = [(i, k) for i in range(PEER_TOPK) for k in range(PEER_TOPK) if (i + 1) * (k + 1) <= PEER_TOPK]
PEER_ROUTE_HEADS = 2


def _peer_route_kernel(q_ref, keys_ref, s1_ref, s2_ref, e1_ref, e2_ref, th_ref):
    half = PEER_DKEY // 2
    nt = (((1,), (1,)), ((), ()))
    heads = range(s1_ref.shape[0])
    scores = []
    for g in heads:
        q = q_ref[:, g * PEER_DKEY:(g + 1) * PEER_DKEY]
        scores.append(lax.dot_general(keys_ref[0], q[:, :half], nt, preferred_element_type=F32))
        scores.append(lax.dot_general(keys_ref[1], q[:, half:], nt, preferred_element_type=F32))
    tops = _top_rows(scores, PEER_TOPK)
    cands = [jnp.concatenate([tops[2 * g][i] + tops[2 * g + 1][k] for i, k in _PEER_PAIRS], axis=0) for g in heads]
    thetas = [t[-1] for t in _top_rows(cands, PEER_TOPK)]
    for g in heads:
        s1, s2 = scores[2 * g], scores[2 * g + 1]
        m1, m2 = tops[2 * g][0], tops[2 * g + 1][0]
        z = jnp.sum(jnp.where(cands[g] >= thetas[g], jnp.exp(cands[g] - (m1 + m2)), 0.0), axis=0, keepdims=True)
        s1_ref[g] = s1
        s2_ref[g] = s2
        e1_ref[g] = jnp.exp(s1 - m1) / z
        e2_ref[g] = jnp.exp(s2 - m2)
        th_ref[g] = thetas[g]


def peer_route(q, keys):
    T = q.shape[0]
    tt = _pick(T, (512, 256, 128))
    H = PEER_HEADS
    hp = PEER_ROUTE_HEADS
    big = jax.ShapeDtypeStruct((H, PEER_NKEYS, T), F32)
    big_spec = pl.BlockSpec((hp, PEER_NKEYS, tt), lambda i, h: (h, 0, i))
    return pl.pallas_call(
        _peer_route_kernel,
        name="peer_route",
        grid=(T // tt, H // hp),
        in_specs=[pl.BlockSpec((tt, hp * PEER_DKEY), lambda i, h: (i, h)),
                  pl.BlockSpec((2, PEER_NKEYS, PEER_DKEY // 2), lambda i, h: (0, 0, 0))],
        out_specs=[big_spec, big_spec, big_spec, big_spec,
                   pl.BlockSpec((hp, 1, tt), lambda i, h: (h, 0, i))],
        out_shape=[big, big, big, big, jax.ShapeDtypeStruct((H, 1, T), F32)],
        compiler_params=_params("parallel", "parallel"),
    )(q, keys)


PEER_TE = 8 * PEER_NKEYS


def _peer_dense_kernel(h_ref, u_ref, v_ref, s1_ref, s2_ref, e1_ref, e2_ref, th_ref, x_ref, g_ref, o_ref,
                       xu_ref, w_ref, acc_ref):
    j = pl.program_id(1)
    tt = h_ref.shape[0]

    @pl.when(j == 0)
    def _():
        acc_ref[...] = jnp.zeros_like(acc_ref)

    xu_ref[...] = lax.dot_general(u_ref[...], h_ref[...], (((1,), (1,)), ((), ())), preferred_element_type=F32)

    for aa in range(PEER_TE // PEER_NKEYS):
        rows = slice(aa * PEER_NKEYS, (aa + 1) * PEER_NKEYS)
        for tc in range(tt // LANES):
            cols = slice(tc * LANES, (tc + 1) * LANES)
            gate = jnp.zeros((PEER_NKEYS, LANES), F32)
            for h in range(PEER_HEADS):
                s1 = s1_ref[h, aa:aa + 1, cols]
                e1 = e1_ref[h, aa:aa + 1, cols]
                gate = gate + jnp.where(s1 + s2_ref[h, :, cols] >= th_ref[h, :, cols], e1 * e2_ref[h, :, cols], 0.0)
            pre = xu_ref[rows, cols]
            act = 0.5 * pre * (1.0 + lax.erf(pre * (2.0 ** -0.5)))
            w_ref[rows, cols] = (gate * act).astype(w_ref.dtype)

    acc_ref[...] += lax.dot_general(w_ref[...], v_ref[...], (((0,), (0,)), ((), ())), preferred_element_type=F32)

    @pl.when(j == pl.num_programs(1) - 1)
    def _():
        o_ref[...] = x_ref[...] + g_ref[0] * acc_ref[...]


def peer_dense(h, u, v, route, x, mods, cidx_tt, tt, gate_slot):
    T, D = h.shape
    s1, s2, e1, e2, th = route
    H = PEER_HEADS
    na = PEER_TE // PEER_NKEYS
    once = pl.Buffered(1)
    tok = pl.BlockSpec((H, PEER_NKEYS, tt), lambda i, j: (0, 0, i), pipeline_mode=once)
    sub = pl.BlockSpec((H, na, tt), lambda i, j: (0, j, i))
    return pl.pallas_call(
        _peer_dense_kernel,
        name="peer_dense",
        grid=(T // tt, PEER_N // PEER_TE),
        in_specs=[pl.BlockSpec((tt, D), lambda i, j: (i, 0), pipeline_mode=once),
                  pl.BlockSpec((PEER_TE, D), lambda i, j: (j, 0)),
                  pl.BlockSpec((PEER_TE, D), lambda i, j: (j, 0)),
                  sub, tok, sub, tok,
                  pl.BlockSpec((H, 1, tt), lambda i, j: (0, 0, i), pipeline_mode=once),
                  pl.BlockSpec((tt, D), lambda i, j: (i, 0), pipeline_mode=once),
                  pl.BlockSpec((1, 1, D), lambda i, j: (cidx_tt(i) * 6 + gate_slot, 0, 0))],
        out_specs=pl.BlockSpec((tt, D), lambda i, j: (i, 0)),
        out_shape=jax.ShapeDtypeStruct((T, D), F32),
        scratch_shapes=[pltpu.VMEM((PEER_TE, tt), F32),
                        pltpu.VMEM((PEER_TE, tt), BF16),
                        pltpu.VMEM((tt, D), F32)],
        compiler_params=_params("parallel", "arbitrary"),
    )(h, u, v, s1, s2, e1, e2, th, x, mods)


HY_HID = 64
HY_BANDS = (HY_EMB - 1) // 2
HY_FREQ_CHUNK = 512


def _hy_filter_kernel(w1t_ref, w1c_ref, w1s_ref, b1_ref, w2_ref, b2_ref, w3_ref, absd_ref, sum_ref, dif_ref, *, L):
    tl = sum_ref.shape[0]
    W = HY_WIDTH
    pos_i = pl.program_id(0) * tl + lax.broadcasted_iota(jnp.int32, (tl, 1), 0)
    pos = pos_i.astype(F32)
    t01 = pos * (1.0 / (L - 1))
    w = (2.0 * math.pi) * pos / L
    band = lax.broadcasted_iota(jnp.int32, (1, HY_BANDS), 1).astype(F32)
    f = 1e-4 + band * ((HY_BANDS - 1 - 1e-4) / (HY_BANDS - 1))
    fw = f * w
    pre = (t01 * w1t_ref[...] + jnp.dot(jnp.cos(fw), w1c_ref[...], preferred_element_type=F32)
           - jnp.dot(jnp.sin(fw), w1s_ref[...], preferred_element_type=F32) + b1_ref[...])
    h = jnp.sin(pre)
    h = jnp.sin(jnp.dot(h, w2_ref[...], preferred_element_type=F32) + b2_ref[...])
    h = jnp.dot(h, w3_ref[...], preferred_element_type=F32)
    window = jnp.exp(-t01 * absd_ref[...])
    for o in range(HY_ORDER):
        hf = h[:, (2 * o) * W:(2 * o + 1) * W] * window
        hb = jnp.where(pos_i == 0, 0.0, h[:, (2 * o + 1) * W:(2 * o + 2) * W] * window)
        sum_ref[:, o * W:(o + 1) * W] = (hf + hb).astype(sum_ref.dtype)
        dif_ref[:, o * W:(o + 1) * W] = (hf - hb).astype(dif_ref.dtype)


def hyena_filter_terms(L, lp):
    W = HY_WIDTH
    tl = _pick(L, (256, 128))
    max_decay = math.log(HY_DECAY_TARGET) / HY_FAST_DECAY
    min_decay = math.log(HY_DECAY_TARGET) / HY_SLOW_DECAY
    absd = jnp.abs(jnp.linspace(min_decay, max_decay, W, dtype=F32)).reshape(1, W)
    w1 = lp["hy_w1"]
    full = lambda a: pl.BlockSpec(a.shape, lambda i: (0,) * a.ndim)
    args = (w1[0:1], w1[1:1 + HY_BANDS], w1[1 + HY_BANDS:], lp["hy_b1"].reshape(1, -1), lp["hy_w2"],
            lp["hy_b2"].reshape(1, -1), lp["hy_w3"], absd)
    out = jax.ShapeDtypeStruct((L, HY_ORDER * W), BF16)
    return pl.pallas_call(
        functools.partial(_hy_filter_kernel, L=L),
        grid=(L // tl,),
        in_specs=[full(a) for a in args],
        out_specs=[pl.BlockSpec((tl, HY_ORDER * W), lambda i: (i, 0))] * 2,
        out_shape=[out, out],
        compiler_params=_params("parallel"),
    )(*args)


def _dft_matrices(L):
    k = jnp.arange(L, dtype=jnp.int32)
    m = (k[:, None] * k[None, :]) % (2 * L)
    ang = m.astype(F32) * (math.pi / L)
    return jnp.stack([jnp.cos(ang), -jnp.sin(ang)]).astype(BF16)


def _hy_conv_kernel(v_ref, x1_ref, x2_ref, cv_ref, c1_ref, c2_ref, f_ref, ka_ref, kb_ref, ks_ref, bias_ref, o_ref,
                    y_ref, yb_ref, conv_ref):
    L = v_ref.shape[1]
    N = 2 * L
    tw = o_ref.shape[2]
    row = lax.broadcasted_iota(jnp.int32, (L, 1), 0)
    nyq = jnp.where(row % 2 == 0, 1.0, -1.0)
    fchunk = min(L, HY_FREQ_CHUNK)

    def short_conv(x_ref, c_ref):
        x = x_ref[0].astype(F32)
        prev = jnp.where(row >= 1, pltpu.roll(x, 1, 0), 0.0)
        nxt = jnp.where(row <= L - 2, pltpu.roll(x, L - 1, 0), 0.0)
        return prev * c_ref[0:1, :] + x * c_ref[1:2, :] + nxt * c_ref[2:3, :]

    y_ref[...] = short_conv(v_ref, cv_ref)
    for o, (x_ref, c_ref) in enumerate(((x1_ref, c1_ref), (x2_ref, c2_ref))):
        cols = slice(o * tw, (o + 1) * tw)
        y = y_ref[...]
        k_nyq = jnp.sum(nyq * ks_ref[:, cols].astype(F32), axis=0, keepdims=True)
        u_nyq = jnp.sum(nyq * y, axis=0, keepdims=True)
        conv_ref[...] = nyq * (u_nyq * k_nyq * (1.0 / N)) + bias_ref[o:o + 1, :] * y
        yb_ref[...] = y.astype(BF16)

        def freq_chunk(i, carry):
            f0 = pl.multiple_of(i * fchunk, fchunk)
            fr = pl.ds(f0, fchunk)
            yb = yb_ref[...]
            ka = ka_ref[fr, cols].astype(F32)
            kb = kb_ref[fr, cols].astype(F32)
            ua = jnp.dot(f_ref[0, fr, :], yb, preferred_element_type=F32)
            ub = jnp.dot(f_ref[1, fr, :], yb, preferred_element_type=F32)
            k_idx = f0 + lax.broadcasted_iota(jnp.int32, (fchunk, 1), 0)
            sc = jnp.where(k_idx == 0, 1.0 / N, 2.0 / N)
            ya = (sc * (ua * ka - ub * kb)).astype(BF16)
            yb2 = (sc * (ua * kb + ub * ka)).astype(BF16)
            conv_ref[...] += (jnp.dot(f_ref[0, :, fr], ya, preferred_element_type=F32)
                              + jnp.dot(f_ref[1, :, fr], yb2, preferred_element_type=F32))
            return carry

        lax.fori_loop(0, L // fchunk, freq_chunk, 0)
        y_ref[...] = short_conv(x_ref, c_ref) * conv_ref[...]
    o_ref[0] = y_ref[...].astype(o_ref.dtype)


def hyena(hy_raw, B, b_off, lp):
    _, L, _ = hy_raw.shape
    W = HY_WIDTH
    tw = 256
    nw = W // tw
    fsum, fdif = hyena_filter_terms(L, lp)
    F = lp["dft"][L]
    ka = matmul(F[0], fsum, BF16)
    kb = matmul(F[1], fdif, BF16)
    regroup = lambda a: a.reshape(L, HY_ORDER, nw, tw).transpose(0, 2, 1, 3).reshape(L, nw * HY_ORDER * tw)
    ka, kb, ks = regroup(ka), regroup(kb), regroup(fsum)
    cw = lp["hy_conv"]
    xspec = lambda g: pl.BlockSpec((1, L, tw), lambda j, b: (b + b_off, 0, g * nw + j))
    cspec = lambda g: pl.BlockSpec((HY_SHORT, tw), lambda j, b: (0, g * nw + j))
    kspec = pl.BlockSpec((L, HY_ORDER * tw), lambda j, b: (0, j), pipeline_mode=pl.Buffered(1))
    return pl.pallas_call(
        _hy_conv_kernel,
        name="hy_conv",
        grid=(nw, B),
        in_specs=[xspec(0), xspec(1), xspec(2), cspec(0), cspec(1), cspec(2),
                  pl.BlockSpec((2, L, L), lambda j, b: (0, 0, 0), pipeline_mode=pl.Buffered(1)),
                  kspec, kspec, kspec,
                  pl.BlockSpec((HY_ORDER, tw), lambda j, b: (0, j))],
        out_specs=pl.BlockSpec((1, L, tw), lambda j, b: (b, 0, j)),
        out_shape=jax.ShapeDtypeStruct((B, L, W), BF16),
        scratch_shapes=[pltpu.VMEM((L, tw), F32), pltpu.VMEM((L, tw), BF16), pltpu.VMEM((L, tw), F32)],
        compiler_params=_params("parallel", "parallel"),
    )(hy_raw, hy_raw, hy_raw, cw, cw, cw, F, ka, kb, ks, lp["hy_bias"])


def _dn_prep_kernel(x_ref, w_ref, o_ref, *, n_q_tiles):
    L = x_ref.shape[1]
    tc = x_ref.shape[2]
    x = x_ref[0].astype(F32)
    row = lax.broadcasted_iota(jnp.int32, (L, 1), 0)
    half = DN_CONV // 2
    acc = x * w_ref[half:half + 1, :]
    for d in range(-half, half + 1):
        if d == 0:
            continue
        shifted = pltpu.roll(x, (-d) % L, 0)
        valid = jnp.logical_and(row + d >= 0, row + d <= L - 1)
        acc = acc + jnp.where(valid, shifted, 0.0) * w_ref[half + d:half + d + 1, :]
    y = acc * jax.nn.sigmoid(acc)
    j = pl.program_id(1)
    for g in range(tc // DN_DK):
        yg = y[:, g * DN_DK:(g + 1) * DN_DK]
        inv = lax.rsqrt(jnp.sum(yg * yg, axis=-1, keepdims=True) + EPS)
        fac = jnp.where(j < n_q_tiles, inv * DN_DK ** -0.5, jnp.where(j < 2 * n_q_tiles, inv, 1.0))
        o_ref[0, :, g * DN_DK:(g + 1) * DN_DK] = yg * fac


def dn_prep(qkv_raw, B, b_off, conv_w):
    _, L, CH = qkv_raw.shape
    tc = 256
    return pl.pallas_call(
        functools.partial(_dn_prep_kernel, n_q_tiles=DN_HEADS * DN_DK // tc),
        grid=(B, CH // tc),
        in_specs=[pl.BlockSpec((1, L, tc), lambda b, j: (b + b_off, 0, j)),
                  pl.BlockSpec((DN_CONV, tc), lambda b, j: (0, j))],
        out_specs=pl.BlockSpec((1, L, tc), lambda b, j: (b, 0, j)),
        out_shape=jax.ShapeDtypeStruct((B, L, CH), F32),
        compiler_params=_params("parallel", "parallel"),
    )(qkv_raw, conv_w)


def _dn_gate_kernel(ab_ref, alog_ref, dt_ref, o_ref):
    n = 2 * DN_HEADS
    a = ab_ref[:, :n] + dt_ref[...]
    softplus = jnp.maximum(a, 0.0) + jnp.log1p(jnp.exp(-jnp.abs(a)))
    o_ref[:, :n] = -jnp.exp(alog_ref[...]) * softplus
    o_ref[:, n:] = jax.nn.sigmoid(ab_ref[:, n:])


def dn_gates(p_ab, a_log, dt_bias):
    T, n2 = p_ab.shape
    tm = _pick(T, (2048, 1024, 512, 256, 128))
    n = 2 * DN_HEADS
    return pl.pallas_call(
        _dn_gate_kernel,
        grid=(T // tm,),
        in_specs=[pl.BlockSpec((tm, n2), lambda i: (i, 0)),
                  pl.BlockSpec((1, n), lambda i: (0, 0)),
                  pl.BlockSpec((1, n), lambda i: (0, 0))],
        out_specs=pl.BlockSpec((tm, n2), lambda i: (i, 0)),
        out_shape=jax.ShapeDtypeStruct((T, n2), F32),
        compiler_params=_params("parallel"),
    )(p_ab, a_log.reshape(1, n), dt_bias.reshape(1, n))


def _split3(x):
    hi = x.astype(BF16)
    r1 = x - hi.astype(F32)
    mid = r1.astype(BF16)
    lo = (r1 - mid.astype(F32)).astype(BF16)
    return hi, mid, lo


def _dn_chunk_kernel(qf_ref, qb_ref, gcf_ref, gcb_ref, grf_ref, grb_ref, s0_ref, of_ref, ob_ref, sout_ref, s_ref):
    n = pl.program_id(1)
    C = DN_CHUNK
    H = DN_HEADS
    nt = (((1,), (1,)), ((), ()))

    @pl.when(n == 0)
    def _():
        s_ref[...] = s0_ref[0]

    r = lax.broadcasted_iota(jnp.int32, (C, C), 0)
    c = lax.broadcasted_iota(jnp.int32, (C, C), 1)
    eye = (r == c).astype(F32)

    def mm(x, y, dims=None):
        x, y = x.astype(BF16), y.astype(BF16)
        if dims is None:
            return jnp.dot(x, y, preferred_element_type=F32)
        return lax.dot_general(x, y, dims, preferred_element_type=F32)

    incl, strict, gcol, gc_col, gc_row, g_tot = [], [], [], [], [], []
    for d, (gc_ref, gr_ref) in enumerate(((gcf_ref, grf_ref), (gcb_ref, grb_ref))):
        lag = (r - c) if d == 0 else (c - r)
        incl.append(lag >= 0)
        strict.append(lag > 0)
        tri = jnp.where(incl[d], 1.0, 0.0).astype(BF16)
        g_c = gc_ref[0, 0]
        g_r = gr_ref[0, 0, 0]
        gcol.append(g_c)
        gc_col.append(sum(jnp.dot(tri, p, preferred_element_type=F32) for p in _split3(g_c)))
        row = sum(lax.dot_general(p, tri, nt, preferred_element_type=F32) for p in _split3(g_r))
        gc_row.append(row)
        g_tot.append(row[:, C - 1:C] if d == 0 else row[:, 0:1])

    chains = [(d, h) for d in range(2) for h in range(H)]
    idx = range(len(chains))
    qkv = (qf_ref, qb_ref)
    q = [qkv[d][0, :, h * DN_DK:(h + 1) * DN_DK] for d, h in chains]
    k = [qkv[d][0, :, (H + h) * DN_DK:(H + h + 1) * DN_DK] for d, h in chains]
    v = [qkv[d][0, :, 2 * H * DN_DK + h * DN_DV:2 * H * DN_DK + (h + 1) * DN_DV] for d, h in chains]
    gc = [gc_col[d][:, h:h + 1] for d, h in chains]
    beta = [gcol[d][:, H + h:H + h + 1] for d, h in chains]
    g_last = [g_tot[d][h:h + 1, :] for d, h in chains]
    eg = [jnp.exp(gc[i]) for i in idx]
    decay = [jnp.exp(jnp.where(incl[d], gc[i] - gc_row[d][h:h + 1, :], NEG_BIG)) for i, (d, h) in enumerate(chains)]
    kk = [mm(k[i], k[i], nt) for i in idx]
    qk = [mm(q[i], k[i], nt) for i in idx]
    a = [jnp.where(strict[d], beta[i] * kk[i] * decay[i], 0.0) for i, (d, h) in enumerate(chains)]
    qk = [jnp.where(incl[d], qk[i] * decay[i], 0.0) for i, (d, h) in enumerate(chains)]
    inv = None
    for l in range(C.bit_length() - 1):
        couple = jnp.logical_and((r >> (l + 1)) == (c >> (l + 1)), (r >> l) != (c >> l))
        a_l = [jnp.where(couple, a[i], 0.0) for i in idx]
        if inv is None:
            inv = [eye - a_l[i] for i in idx]
        else:
            t = [mm(inv[i], a_l[i]) for i in idx]
            inv = [inv[i] - mm(t[i], inv[i]) for i in idx]
    rhs = [jnp.concatenate([v[i] * beta[i], k[i] * (beta[i] * eg[i])], axis=1) for i in idx]
    sol = [mm(inv[i], rhs[i]) for i in idx]
    S = [s_ref[d, h] for d, h in chains]
    v_new = [sol[i][:, :DN_DV] - mm(sol[i][:, DN_DV:], S[i]) for i in idx]
    o = [mm(q[i] * eg[i], S[i]) + mm(qk[i], v_new[i]) for i in idx]
    upd = [mm(k[i] * jnp.exp(g_last[i] - gc[i]), v_new[i], (((0,), (0,)), ((), ()))) for i in idx]
    o_refs = (of_ref, ob_ref)
    for i, (d, h) in enumerate(chains):
        o_refs[d][0, :, h * DN_DV:(h + 1) * DN_DV] = o[i]
        s_ref[d, h] = S[i] * jnp.exp(g_last[i]) + upd[i]

    @pl.when(n == pl.num_programs(1) - 1)
    def _():
        sout_ref[0] = s_ref[...]


def dn_scan(qkv, gates, s0):
    B, L, CH = qkv.shape
    C, H = DN_CHUNK, DN_HEADS
    N = L // C
    g4 = gates.reshape(B, L, 2, 2, H)
    gcol = g4.transpose(0, 3, 1, 2, 4).reshape(B, 2, L, 2 * H)
    grow = gcol.reshape(B, 2, N, C, 2 * H).transpose(0, 1, 2, 4, 3)
    rev = lambda n: N - 1 - n
    out = jax.ShapeDtypeStruct((B, L, H * DN_DV), F32)
    return pl.pallas_call(
        _dn_chunk_kernel,
        name="dn_chunk",
        grid=(B, N),
        in_specs=[pl.BlockSpec((1, C, CH), lambda b, n: (b, n, 0)),
                  pl.BlockSpec((1, C, CH), lambda b, n: (b, rev(n), 0)),
                  pl.BlockSpec((1, 1, C, 2 * H), lambda b, n: (b, 0, n, 0)),
                  pl.BlockSpec((1, 1, C, 2 * H), lambda b, n: (b, 1, rev(n), 0)),
                  pl.BlockSpec((1, 1, 1, 2 * H, C), lambda b, n: (b, 0, n, 0, 0)),
                  pl.BlockSpec((1, 1, 1, 2 * H, C), lambda b, n: (b, 1, rev(n), 0, 0)),
                  pl.BlockSpec((1, 2, H, DN_DK, DN_DV), lambda b, n: (b, 0, 0, 0, 0))],
        out_specs=[pl.BlockSpec((1, C, H * DN_DV), lambda b, n: (b, n, 0)),
                   pl.BlockSpec((1, C, H * DN_DV), lambda b, n: (b, rev(n), 0)),
                   pl.BlockSpec((1, 2, H, DN_DK, DN_DV), lambda b, n: (b, 0, 0, 0, 0))],
        out_shape=[out, out, jax.ShapeDtypeStruct((B, 2, H, DN_DK, DN_DV), F32)],
        scratch_shapes=[pltpu.VMEM((2, H, DN_DK, DN_DV), F32)],
        compiler_params=_params("parallel", "arbitrary"),
    )(qkv, qkv, gcol, gcol, grow, grow, s0)


def _dn_out_kernel(of_ref, ob_ref, z_ref, g_ref, o_ref):
    for h in range(DN_HEADS):
        cols = slice(h * DN_DV, (h + 1) * DN_DV)
        o = of_ref[0, :, cols] + ob_ref[0, :, cols]
        z = z_ref[0, :, cols].astype(F32)
        o_ref[0, :, cols] = (_rms(o, g_ref[...]) * (z * jax.nn.sigmoid(z))).astype(o_ref.dtype)


def dn_output(o_f, o_b, z, b_off, g):
    B, L, W = o_f.shape
    tm = _pick(L, (512, 256, 128))
    spec = pl.BlockSpec((1, tm, W), lambda b, i: (b, i, 0))
    return pl.pallas_call(
        _dn_out_kernel,
        grid=(B, L // tm),
        in_specs=[spec, spec,
                  pl.BlockSpec((1, tm, W), lambda b, i: (b + b_off, i, 0)),
                  pl.BlockSpec((1, DN_DV), lambda b, i: (0, 0))],
        out_specs=spec,
        out_shape=jax.ShapeDtypeStruct((B, L, W), BF16),
        compiler_params=_params("parallel", "parallel"),
    )(o_f, o_b, z, g.reshape(1, DN_DV))


def gated_deltanet(qkv_raw, z, B, b_off, gates, lp, s0):
    qkv = dn_prep(qkv_raw, B, b_off, lp["dn_conv"])
    o_f, o_b, s_fin = dn_scan(qkv, gates, s0)
    return dn_output(o_f, o_b, z, b_off, lp["dn_out_norm"]), s_fin


def _axial_rope(x):
    L = x.shape[1]
    half = x.shape[-1] // 2
    t = jnp.arange(L)
    inv = ROPE_THETA ** (-jnp.arange(0, half, 2, dtype=F32) / half)
    out = []
    for pos, xa in ((t // GRID_W, x[..., :half]), (t % GRID_W, x[..., half:])):
        ang = pos.astype(F32)[:, None] * inv[None, :]
        cos, sin = jnp.cos(ang)[:, None, :], jnp.sin(ang)[:, None, :]
        x1, x2 = xa[..., : half // 2], xa[..., half // 2:]
        out += [x1 * cos - x2 * sin, x2 * cos + x1 * sin]
    return jnp.concatenate(out, axis=-1)


def _layer(x, lp, mods, dims, caches):
    Bc, Lc, Bl, Ll = dims
    Tc, Tl = Bc * Lc, Bl * Ll
    T = Tc + Tl
    tm = _pick(math.gcd(Tc, Ll), (1024, 512, 256, 128))

    def make_cidx(tile):
        nct, tpl = Tc // tile, Ll // tile
        return lambda i: jnp.where(i < nct, 0, 1 + (i - nct) // tpl)

    cidx = make_cidx(tm)
    h = modulate(x, lp["norm1_g"], mods, cidx, tm, 0, 1)

    w_in = lp["w_in"]
    cuts = np.cumsum((0,) + IN_SIZES)
    col = lambda a, b: w_in[:, cuts[a]:cuts[b]].astype(BF16)
    p_mla = matmul(h, col(0, 3))
    p_dn = matmul(h, col(3, 4), BF16)
    p_z = matmul(h, col(4, 5), BF16)
    p_ab = matmul(h, col(5, 7))
    p_hy = matmul(h, col(7, 8), BF16)
    p_na = matmul(h, col(8, 9), BF16)
    na_kv_ctx = matmul(h, w_in[:, cuts[8] + BRANCH_W:cuts[9]].astype(BF16), F32, rows=Tc)
    p_gate = matmul(h, col(9, 10), BF16)

    cq, ckv, krope = p_mla[:, :MLA_Q_RANK], p_mla[:, MLA_Q_RANK:MLA_Q_RANK + MLA_KV_RANK], p_mla[:, -MLA_ROPE:]

    cq_n = rmsnorm(cq, lp["mla_q_norm"], BF16)
    ckv_n = rmsnorm(ckv, lp["mla_kv_norm"], F32)
    def seqs(p, L, start, B):
        if start % L == 0 and T % L == 0:
            return p.reshape(T // L, L, -1), start // L
        return p[start:start + B * L].reshape(B, L, -1), 0

    ctx_of = lambda p: seqs(p, Lc, 0, Bc)
    lat_of = lambda p: seqs(p, Ll, Tc, Bl)

    w_qb = lp["mla_w_qb"].reshape(MLA_Q_RANK, MLA_HEADS, MLA_NOPE + MLA_ROPE)
    q_nope = matmul(cq_n, w_qb[:, :, :MLA_NOPE].reshape(MLA_Q_RANK, -1).astype(BF16), BF16)
    q_rope = matmul(cq_n, w_qb[:, :, MLA_NOPE:].reshape(MLA_Q_RANK, -1).astype(BF16))
    w_kvb = lp["mla_w_kvb"].astype(BF16)
    kv_all = matmul(ckv_n.astype(BF16), w_kvb, BF16)
    ckv_ctx, krope_ctx, nak_ctx, nav_ctx, s_f0, s_b0 = caches
    kv_cache = matmul(ckv_ctx.reshape(-1, MLA_KV_RANK).astype(BF16), w_kvb, BF16)
    Lp = ckv_ctx.shape[1]
    head_major = lambda t: t.transpose(0, 2, 1, 3).astype(BF16)

    (qn_c, qoff_c), (kv_c, koff_c) = ctx_of(q_nope), ctx_of(kv_all)
    o_a_ctx = mla_attention(qn_c, head_major(q_rope[:Tc].reshape(Bc, Lc, MLA_HEADS, MLA_ROPE)), kv_c,
                            krope[:Tc].reshape(Bc, Lc, MLA_ROPE).astype(BF16), Bc, qoff_c, koff_c)
    qn_l, qoff_l = lat_of(q_nope)
    qr_l = head_major(_axial_rope(q_rope[Tc:].reshape(Bl, Ll, MLA_HEADS, MLA_ROPE)))
    kr_l = _axial_rope(krope[Tc:].reshape(Bl, Ll, 1, MLA_ROPE)).reshape(Bl, Ll, MLA_ROPE)
    kr_l = jnp.concatenate([kr_l, krope_ctx], axis=1).astype(BF16)
    kv_l = jnp.concatenate([kv_all[Tc:].reshape(Bl, Ll, -1), kv_cache.reshape(Bl, Lp, -1)], axis=1)
    o_a_lat = mla_attention(qn_l, qr_l, kv_l, kr_l, Bl, qoff_l, 0)
    o_a = jnp.concatenate([o_a_ctx.reshape(Tc, BRANCH_W), o_a_lat.reshape(Tl, BRANCH_W)])

    gates = dn_gates(p_ab, lp["dn_a_log"], lp["dn_dt_bias"])
    zero_state = jnp.zeros((Bc, 2, DN_HEADS, DN_DK, DN_DV), F32)
    (dn_c, off_c), (dn_l, off_l) = ctx_of(p_dn), lat_of(p_dn)
    o_b_ctx, s_ctx = gated_deltanet(dn_c, ctx_of(p_z)[0], Bc, off_c, gates[:Tc].reshape(Bc, Lc, -1), lp, zero_state)
    o_b_lat, _ = gated_deltanet(dn_l, lat_of(p_z)[0], Bl, off_l, gates[Tc:].reshape(Bl, Ll, -1), lp,
                                jnp.stack([s_f0, s_b0], axis=1))
    s_f, s_b = s_ctx[:, 0], s_ctx[:, 1]
    o_b = jnp.concatenate([o_b_ctx.reshape(Tc, BRANCH_W), o_b_lat.reshape(Tl, BRANCH_W)])

    (hy_c, off_c), (hy_l, off_l) = ctx_of(p_hy), lat_of(p_hy)
    o_c = jnp.concatenate([hyena(hy_c, Bc, off_c, lp).reshape(Tc, BRANCH_W),
                           hyena(hy_l, Bl, off_l, lp).reshape(Tl, BRANCH_W)])

    (na_c, off_c), (na_l, off_l) = ctx_of(p_na), lat_of(p_na)
    o_d_ctx = attention(na_c, na_c, na_c, NA_HEADS, NA_HD, NA_HD, NA_SCALE, k_off=NA_HEADS, v_off=2 * NA_HEADS,
                        batch=Bc, b_off=off_c)
    o_d_lat = neighbourhood_attention(na_l, Bl, off_l, nak_ctx.reshape(Bl, Lp, -1).astype(BF16),
                                      nav_ctx.reshape(Bl, Lp, -1).astype(BF16), lp["na_bias"])
    o_d = jnp.concatenate([o_d_ctx.reshape(Tc, BRANCH_W), o_d_lat.reshape(Tl, BRANCH_W)])

    merged = merge_branches((o_a, o_b, o_c, o_d), p_gate, lp["w_branch"].astype(BF16), tm)
    x = matmul_gated_residual(merged, lp["w_out"].astype(BF16), x, mods, cidx, tm, 2)

    h2 = modulate(x, lp["norm2_g"], mods, cidx, tm, 3, 4)
    pq = matmul(h2, lp["peer_wq"].astype(BF16))
    route = peer_route(pq, lp["peer_keys"])
    tt = _pick(math.gcd(Tc, Ll), (512, 256, 128))
    x = peer_dense(h2, lp["peer_u"].astype(BF16), lp["peer_v"].astype(BF16), route, x, mods, make_cidx(tt), tt, 5)

    ctx_out = (ckv_n[:Tc].reshape(Bc, Lc, MLA_KV_RANK), krope[:Tc].reshape(Bc, Lc, MLA_ROPE),
               na_kv_ctx[:, :BRANCH_W].reshape(Bc, Lc, NA_HEADS, NA_HD),
               na_kv_ctx[:, BRANCH_W:].reshape(Bc, Lc, NA_HEADS, NA_HD), s_f, s_b)
    return x, ctx_out


def kernel(x_prompt, x_sample, cache_mla_ckv, cache_mla_krope, cache_na_k, cache_na_v, state_dn_fwd, state_dn_bwd, c, c_ctx, norm1_g, w_ada, b_ada, w_in, mla_q_norm, mla_w_qb, mla_kv_norm, mla_w_kvb, dn_conv, dn_a_log, dn_dt_bias, dn_out_norm, hy_conv, hy_w1, hy_b1, hy_w2, hy_b2, hy_w3, hy_bias, na_rpb, w_branch, w_out, norm2_g, peer_wq, peer_keys, peer_u, peer_v, final_g):
    Bc, Lc, D = x_prompt.shape
    Bl, Ll, _ = x_sample.shape
    depth = w_in.shape[0]
    Tc = Bc * Lc
    x = jnp.concatenate([x_prompt.reshape(Tc, D), x_sample.reshape(Bl * Ll, D)])
    n_cond = 1 + Bl
    cond_rows = -(-n_cond // SUBLANES) * SUBLANES
    cond = jnp.concatenate([c_ctx[None], c, jnp.zeros((cond_rows - n_cond, D), F32)])

    names = ("norm1_g", "w_in", "mla_q_norm", "mla_w_qb", "mla_kv_norm", "mla_w_kvb", "dn_conv", "dn_a_log",
             "dn_dt_bias", "dn_out_norm", "hy_conv", "hy_w1", "hy_b1", "hy_w2", "hy_b2", "hy_w3", "hy_bias",
             "na_rpb", "w_branch", "w_out", "norm2_g", "peer_wq", "peer_keys", "peer_u", "peer_v")
    vals = (norm1_g, w_in, mla_q_norm, mla_w_qb, mla_kv_norm, mla_w_kvb, dn_conv, dn_a_log, dn_dt_bias,
            dn_out_norm, hy_conv, hy_w1, hy_b1, hy_w2, hy_b2, hy_w3, hy_bias, na_rpb, w_branch, w_out, norm2_g,
            peer_wq, peer_keys, peer_u, peer_v)
    dft = {L: _dft_matrices(L) for L in {Lc, Ll}}
    na_bias = _na_bias_tables(na_rpb.reshape((depth * NA_HEADS,) + na_rpb.shape[2:]), Ll // GRID_W)
    na_bias = na_bias.reshape((depth, NA_HEADS) + na_bias.shape[1:])
    ctx = []
    for l in range(depth):
        lp = {n: v[l] for n, v in zip(names, vals)}
        lp["dft"] = dft
        lp["na_bias"] = na_bias[l]
        mods = ada_modulation(cond, w_ada[l], b_ada[l]).reshape(cond_rows * 6, 1, D)
        caches = (cache_mla_ckv[:, l], cache_mla_krope[:, l], cache_na_k[:, l], cache_na_v[:, l],
                  state_dn_fwd[:, l], state_dn_bwd[:, l])
        x, ctx_out = _layer(x, lp, mods, (Bc, Lc, Bl, Ll), caches)
        ctx.append(ctx_out)
    y = rmsnorm(x, final_g)
    stack = lambda k: jnp.stack([t[k] for t in ctx], axis=1)
    return (y[:Tc].reshape(Bc, Lc, D), y[Tc:].reshape(Bl, Ll, D), stack(0), stack(1), stack(2), stack(3),
            stack(4), stack(5))
```

```python
import functools
import math

import numpy as np
import jax
import jax.numpy as jnp
from jax import lax
from jax.experimental import pallas as pl
from jax.experimental.pallas import tpu as pltpu

F32 = jnp.float32
BF16 = jnp.bfloat16

V7X_VMEM_BYTES = 64 * 1024 * 1024
VMEM_LIMIT = V7X_VMEM_BYTES - 8 * 1024 * 1024
LANES = 128
SUBLANES = 8

D_MODEL = 2048
GRID_W = 64
EPS = 1e-6
N_BRANCH = 4
BRANCH_W = D_MODEL // 2

MLA_HEADS = 8
MLA_NOPE = 128
MLA_ROPE = 64
MLA_V = BRANCH_W // MLA_HEADS
MLA_Q_RANK = D_MODEL // 4
MLA_KV_RANK = D_MODEL // 8
MLA_SCALE = (MLA_NOPE + MLA_ROPE) ** -0.5
ROPE_THETA = 10000.0

DN_HEADS = 8
DN_DK = 128
DN_DV = BRANCH_W // DN_HEADS
DN_CONV = 5
DN_CHUNK = 64
DN_CONV_CH = 2 * DN_HEADS * DN_DK + DN_HEADS * DN_DV

HY_WIDTH = BRANCH_W
HY_ORDER = 2
HY_SHORT = 3
HY_EMB = 33
HY_FAST_DECAY = 0.3
HY_SLOW_DECAY = 1.5
HY_DECAY_TARGET = 1e-2

NA_HEADS = 8
NA_HD = BRANCH_W // NA_HEADS
NA_KH = 8
NA_KW = 16
NA_SCALE = NA_HD ** -0.5
NA_QROWS = 4
NA_REGION = NA_KH + NA_QROWS
NEG_BIG = -1e30

PEER_HEADS = 8
PEER_NKEYS = 128
PEER_N = PEER_NKEYS * PEER_NKEYS
PEER_DKEY = 256
PEER_TOPK = 16

IN_SIZES = (MLA_Q_RANK, MLA_KV_RANK, MLA_ROPE, DN_CONV_CH, DN_HEADS * DN_DV, 2 * DN_HEADS, 2 * DN_HEADS,
            (HY_ORDER + 1) * HY_WIDTH, 3 * NA_HEADS * NA_HD, N_BRANCH * D_MODEL)


def _params(*sem):
    return pltpu.CompilerParams(dimension_semantics=sem, vmem_limit_bytes=VMEM_LIMIT)


def _pick(n, prefs):
    for p in prefs:
        if n % p == 0:
            return p
    return n


def _mm_kernel(x_ref, w_ref, o_ref):
    o_ref[...] = jnp.dot(x_ref[...], w_ref[...], preferred_element_type=F32).astype(o_ref.dtype)


def matmul(x, w, out_dtype=F32, tm=None, tn=None, rows=None):
    M, K = x.shape
    M = rows or M
    N = w.shape[1]
    tm = tm or _pick(M, (1024, 512, 256, 128))
    tn = tn or _pick(N, (512, 256, 128))
    return pl.pallas_call(
        _mm_kernel,
        grid=(M // tm, N // tn),
        in_specs=[pl.BlockSpec((tm, K), lambda i, j: (i, 0)),
                  pl.BlockSpec((K, tn), lambda i, j: (0, j))],
        out_specs=pl.BlockSpec((tm, tn), lambda i, j: (i, j)),
        out_shape=jax.ShapeDtypeStruct((M, N), out_dtype),
        compiler_params=_params("parallel", "parallel"),
    )(x, w)


def _ada_kernel(c_ref, w_ref, b_ref, o_ref):
    c = c_ref[...]
    o_ref[...] = jnp.dot(c * jax.nn.sigmoid(c), w_ref[...], preferred_element_type=F32) + b_ref[...]


def ada_modulation(cond, w_ada, b_ada):
    R, K = cond.shape
    N = w_ada.shape[1]
    tn = _pick(N, (1536, 1024, 512))
    return pl.pallas_call(
        _ada_kernel,
        grid=(N // tn,),
        in_specs=[pl.BlockSpec((R, K), lambda j: (0, 0)),
                  pl.BlockSpec((K, tn), lambda j: (0, j)),
                  pl.BlockSpec((1, tn), lambda j: (0, j))],
        out_specs=pl.BlockSpec((R, tn), lambda j: (0, j)),
        out_shape=jax.ShapeDtypeStruct((R, N), F32),
        compiler_params=_params("parallel"),
    )(cond, w_ada, b_ada.reshape(1, N))


def _rms(x, g):
    return x * lax.rsqrt(jnp.mean(x * x, axis=-1, keepdims=True) + EPS) * g


def _modulate_kernel(x_ref, g_ref, sh_ref, sc_ref, o_ref):
    y = _rms(x_ref[...], g_ref[...])
    o_ref[...] = (y * (1.0 + sc_ref[0]) + sh_ref[0]).astype(o_ref.dtype)


def modulate(x, g, mods, cidx, tm, shift_slot, scale_slot):
    T, D = x.shape
    return pl.pallas_call(
        _modulate_kernel,
        grid=(T // tm,),
        in_specs=[pl.BlockSpec((tm, D), lambda i: (i, 0)),
                  pl.BlockSpec((1, D), lambda i: (0, 0)),
                  pl.BlockSpec((1, 1, D), lambda i: (cidx(i) * 6 + shift_slot, 0, 0)),
                  pl.BlockSpec((1, 1, D), lambda i: (cidx(i) * 6 + scale_slot, 0, 0))],
        out_specs=pl.BlockSpec((tm, D), lambda i: (i, 0)),
        out_shape=jax.ShapeDtypeStruct((T, D), BF16),
        compiler_params=_params("parallel"),
    )(x, g.reshape(1, D), mods, mods)


def _rmsnorm_kernel(x_ref, g_ref, o_ref):
    o_ref[...] = _rms(x_ref[...].astype(F32), g_ref[...]).astype(o_ref.dtype)


def rmsnorm(x, g, out_dtype=F32, start=0, rows=None):
    T, D = x.shape
    T = rows or T
    tm = _pick(math.gcd(T, start) if start else T, (1024, 512, 256, 128))
    first = start // tm
    return pl.pallas_call(
        _rmsnorm_kernel,
        grid=(T // tm,),
        in_specs=[pl.BlockSpec((tm, D), lambda i: (i + first, 0)),
                  pl.BlockSpec((1, D), lambda i: (0, 0))],
        out_specs=pl.BlockSpec((tm, D), lambda i: (i, 0)),
        out_shape=jax.ShapeDtypeStruct((T, D), out_dtype),
        compiler_params=_params("parallel"),
    )(x, g.reshape(1, D))


def _merge_kernel(a_ref, b_ref, c_ref, d_ref, ga_ref, gb_ref, gc_ref, gd_ref, w_ref, o_ref):
    acc = None
    for n, (br, gl) in enumerate(((a_ref, ga_ref), (b_ref, gb_ref), (c_ref, gc_ref), (d_ref, gd_ref))):
        proj = jnp.dot(br[...], w_ref[n], preferred_element_type=F32)
        term = jax.nn.sigmoid(gl[...].astype(F32)) * proj
        acc = term if acc is None else acc + term
    o_ref[...] = acc.astype(o_ref.dtype)


def merge_branches(branches, gate_logits, w_branch, tm):
    T = branches[0].shape[0]
    tn = 512
    br_spec = pl.BlockSpec((tm, BRANCH_W), lambda i, j: (i, 0))
    nj = D_MODEL // tn
    gl_specs = [pl.BlockSpec((tm, tn), functools.partial(lambda i, j, n: (i, n * nj + j), n=n))
                for n in range(N_BRANCH)]
    return pl.pallas_call(
        _merge_kernel,
        grid=(T // tm, D_MODEL // tn),
        in_specs=[br_spec, br_spec, br_spec, br_spec, *gl_specs,
                  pl.BlockSpec((N_BRANCH, BRANCH_W, tn), lambda i, j: (0, 0, j))],
        out_specs=pl.BlockSpec((tm, tn), lambda i, j: (i, j)),
        out_shape=jax.ShapeDtypeStruct((T, D_MODEL), BF16),
        compiler_params=_params("parallel", "parallel"),
    )(*branches, gate_logits, gate_logits, gate_logits, gate_logits, w_branch)


def _mm_resid_kernel(m_ref, w_ref, x_ref, g_ref, o_ref):
    o_ref[...] = x_ref[...] + g_ref[0] * jnp.dot(m_ref[...], w_ref[...], preferred_element_type=F32)


def matmul_gated_residual(m, w, x, mods, cidx, tm, gate_slot):
    T, K = m.shape
    N = w.shape[1]
    tn = 512
    return pl.pallas_call(
        _mm_resid_kernel,
        grid=(T // tm, N // tn),
        in_specs=[pl.BlockSpec((tm, K), lambda i, j: (i, 0)),
                  pl.BlockSpec((K, tn), lambda i, j: (0, j)),
                  pl.BlockSpec((tm, tn), lambda i, j: (i, j)),
                  pl.BlockSpec((1, 1, tn), lambda i, j: (cidx(i) * 6 + gate_slot, 0, j))],
        out_specs=pl.BlockSpec((tm, tn), lambda i, j: (i, j)),
        out_shape=jax.ShapeDtypeStruct((T, N), F32),
        compiler_params=_params("parallel", "parallel"),
    )(m, w, x, mods)


def _attn_kernel(q_ref, k_ref, v_ref, o_ref, *, scale):
    s = lax.dot_general(q_ref[0], k_ref[0], (((1,), (1,)), ((), ())), preferred_element_type=F32) * scale
    p = jnp.exp(s - jnp.max(s, axis=-1, keepdims=True))
    l = jnp.sum(p, axis=-1, keepdims=True)
    o = jnp.dot(p.astype(BF16), v_ref[0], preferred_element_type=F32)
    o_ref[0] = (o / l).astype(o_ref.dtype)


def attention(q, k, v, heads, dqk, dv, scale, q_off=0, k_off=0, v_off=0, v_stride=1, k_stride=1, batch=None,
              b_off=0):
    B, Lq, _ = q.shape
    B = batch or B
    Lk = k.shape[1]
    tq = _pick(Lq, (256, 128))
    return pl.pallas_call(
        functools.partial(_attn_kernel, scale=scale),
        grid=(B, heads, Lq // tq),
        in_specs=[pl.BlockSpec((1, tq, dqk), lambda b, h, i: (b + b_off, i, q_off + h)),
                  pl.BlockSpec((1, Lk, dqk), lambda b, h, i: (b + b_off, 0, k_off + h * k_stride)),
                  pl.BlockSpec((1, Lk, dv), lambda b, h, i: (b + b_off, 0, v_off + h * v_stride))],
        out_specs=pl.BlockSpec((1, tq, dv), lambda b, h, i: (b, i, h)),
        out_shape=jax.ShapeDtypeStruct((B, Lq, heads * dv), BF16),
        compiler_params=_params("parallel", "parallel", "parallel"),
    )(q, k, v)


def _mla_attn_kernel(qn_ref, qr_ref, kn_ref, kr_ref, v_ref, o_ref):
    q = jnp.concatenate([qn_ref[0], qr_ref[0, 0]], axis=1)
    k = jnp.concatenate([kn_ref[0], kr_ref[0]], axis=1)
    s = lax.dot_general(q, k, (((1,), (1,)), ((), ())), preferred_element_type=F32) * MLA_SCALE
    p = jnp.exp(s - jnp.max(s, axis=-1, keepdims=True))
    l = jnp.sum(p, axis=-1, keepdims=True)
    o = jnp.dot(p.astype(BF16), v_ref[0], preferred_element_type=F32)
    o_ref[0] = (o / l).astype(o_ref.dtype)


def mla_attention(q_nope, q_rope, kv, k_rope, B, q_off, kv_off):
    _, Lq, _ = q_nope.shape
    Lk = kv.shape[1]
    H = MLA_HEADS
    tq = _pick(Lq, (256, 128))
    return pl.pallas_call(
        _mla_attn_kernel,
        grid=(B, H, Lq // tq),
        in_specs=[pl.BlockSpec((1, tq, MLA_NOPE), lambda b, h, i: (b + q_off, i, h)),
                  pl.BlockSpec((1, 1, tq, MLA_ROPE), lambda b, h, i: (b, h, i, 0)),
                  pl.BlockSpec((1, Lk, MLA_NOPE), lambda b, h, i: (b + kv_off, 0, 2 * h)),
                  pl.BlockSpec((1, Lk, MLA_ROPE), lambda b, h, i: (b, 0, 0)),
                  pl.BlockSpec((1, Lk, MLA_V), lambda b, h, i: (b + kv_off, 0, 2 * h + 1))],
        out_specs=pl.BlockSpec((1, tq, MLA_V), lambda b, h, i: (b, i, h)),
        out_shape=jax.ShapeDtypeStruct((B, Lq, H * MLA_V), BF16),
        compiler_params=_params("parallel", "parallel", "parallel"),
    )(q_nope, q_rope, kv, k_rope, kv)


def _na_kernel(q_ref, k_ref, v_ref, kc_ref, vc_ref, bias_ref, o_ref, *, rows):
    nblk = rows // NA_QROWS
    tq = NA_QROWS * GRID_W
    nkeys = NA_REGION * GRID_W
    nt = (((1,), (1,)), ((), ()))
    kc = kc_ref[0]
    vc = vc_ref[0]
    for rb in range(nblk):
        start = min(max(rb * NA_QROWS - NA_KH // 2, 0), rows - NA_REGION) * GRID_W
        kind = 0 if rb == 0 else (2 if rb == nblk - 1 else 1)
        q = q_ref[0, rb * tq:(rb + 1) * tq, :]
        kr = k_ref[0, start:start + nkeys, :]
        vr = v_ref[0, start:start + nkeys, :]
        s_win = lax.dot_general(q, kr, nt, preferred_element_type=F32) * NA_SCALE + bias_ref[0, kind]
        s_ctx = lax.dot_general(q, kc, nt, preferred_element_type=F32) * NA_SCALE
        m = jnp.maximum(jnp.max(s_win, axis=-1, keepdims=True), jnp.max(s_ctx, axis=-1, keepdims=True))
        p_win = jnp.exp(s_win - m)
        p_ctx = jnp.exp(s_ctx - m)
        l = jnp.sum(p_win, axis=-1, keepdims=True) + jnp.sum(p_ctx, axis=-1, keepdims=True)
        o = (jnp.dot(p_win.astype(BF16), vr, preferred_element_type=F32)
             + jnp.dot(p_ctx.astype(BF16), vc, preferred_element_type=F32))
        o_ref[0, rb * tq:(rb + 1) * tq, :] = (o / l).astype(o_ref.dtype)


def _na_bias_tables(rpb, rows):
    H = rpb.shape[0]
    n_dr, n_dc = 2 * NA_KH - 1, 2 * NA_KW - 1
    assert rows >= NA_REGION + NA_QROWS and rows % NA_QROWS == 0
    nblk = rows // NA_QROWS
    span = 2 * GRID_W
    lo = GRID_W - NA_KW
    v = jnp.pad(rpb.astype(F32), ((0, 0), (0, 0), (lo, span - lo - n_dc)), constant_values=NEG_BIG)
    skew = jnp.tile(v, (1, 1, GRID_W))[..., :GRID_W * (span - 1)].reshape(H, n_dr, GRID_W, span - 1)
    band = skew[..., GRID_W - 1:]
    c = np.arange(GRID_W)[:, None]
    kc = np.arange(GRID_W)[None, :]
    col0 = np.clip(c - NA_KW // 2, 0, GRID_W - NA_KW)
    col_ok = (kc >= col0) & (kc < col0 + NA_KW)
    band = jnp.where(col_ok, band, NEG_BIG)
    band = jnp.concatenate([band, jnp.full((H, 1, GRID_W, GRID_W), NEG_BIG, F32)], axis=1)
    tiles = []
    for rb in (0, 1, nblk - 1):
        start = int(np.clip(rb * NA_QROWS - NA_KH // 2, 0, rows - NA_REGION))
        for qr in range(NA_QROWS):
            r = rb * NA_QROWS + qr
            row0 = int(np.clip(r - NA_KH // 2, 0, rows - NA_KH))
            for j in range(NA_REGION):
                kr = start + j
                tiles.append(kr - r + NA_KH - 1 if row0 <= kr < row0 + NA_KH else n_dr)
    t = jnp.take(band, np.asarray(tiles, np.int32), axis=1).reshape(H, 3, NA_QROWS, NA_REGION, GRID_W, GRID_W)
    return t.transpose(0, 1, 2, 4, 3, 5).reshape(H, 3, NA_QROWS * GRID_W, NA_REGION * GRID_W)


def neighbourhood_attention(na_in, B, b_off, k_ctx, v_ctx, bias):
    _, L, _ = na_in.shape
    Lc = k_ctx.shape[1]
    rows = L // GRID_W
    tq = NA_QROWS * GRID_W
    nkeys = NA_REGION * GRID_W
    H = NA_HEADS
    seq = lambda col: pl.BlockSpec((1, L, NA_HD), lambda h, b: (b + b_off, 0, col * H + h))
    ctx = pl.BlockSpec((1, Lc, NA_HD), lambda h, b: (b, 0, h))
    return pl.pallas_call(
        functools.partial(_na_kernel, rows=rows),
        grid=(H, B),
        in_specs=[seq(0), seq(1), seq(2), ctx, ctx,
                  pl.BlockSpec((1, 3, tq, nkeys), lambda h, b: (h, 0, 0, 0))],
        out_specs=pl.BlockSpec((1, L, NA_HD), lambda h, b: (b, 0, h)),
        out_shape=jax.ShapeDtypeStruct((B, L, H * NA_HD), BF16),
        compiler_params=_params("parallel", "parallel"),
    )(na_in, na_in, na_in, k_ctx, v_ctx, bias)


def _top_rows(works, k):
    outs = [[] for _ in works]
    for _ in range(k):
        mx = [jnp.max(w, axis=0, keepdims=True) for w in works]
        for out, m in zip(outs, mx):
            out.append(m)
        works = [jnp.where(w == m, -jnp.inf, w) for w, m in zip(works, mx)]
    return outs


_PEER_PAIRS = [(i, k) for i in range(PEER_TOPK) for k in range(PEER_TOPK) if (i + 1) * (k + 1) <= PEER_TOPK]
PEER_ROUTE_HEADS = 2


def _peer_route_kernel(q_ref, keys_ref, s1_ref, s2_ref, e1_ref, e2_ref, th_ref):
    half = PEER_DKEY // 2
    nt = (((1,), (1,)), ((), ()))
    heads = range(s1_ref.shape[0])
    scores = []
    for g in heads:
        q = q_ref[:, g * PEER_DKEY:(g + 1) * PEER_DKEY]
        scores.append(lax.dot_general(keys_ref[0], q[:, :half], nt, preferred_element_type=F32))
        scores.append(lax.dot_general(keys_ref[1], q[:, half:], nt, preferred_element_type=F32))
    tops = _top_rows(scores, PEER_TOPK)
    cands = [jnp.concatenate([tops[2 * g][i] + tops[2 * g + 1][k] for i, k in _PEER_PAIRS], axis=0) for g in heads]
    thetas = [t[-1] for t in _top_rows(cands, PEER_TOPK)]
    for g in heads:
        s1, s2 = scores[2 * g], scores[2 * g + 1]
        m1, m2 = tops[2 * g][0], tops[2 * g + 1][0]
        z = jnp.sum(jnp.where(cands[g] >= thetas[g], jnp.exp(cands[g] - (m1 + m2)), 0.0), axis=0, keepdims=True)
        s1_ref[g] = s1
        s2_ref[g] = s2
        e1_ref[g] = jnp.exp(s1 - m1) / z
        e2_ref[g] = jnp.exp(s2 - m2)
        th_ref[g] = thetas[g]


def peer_route(q, keys):
    T = q.shape[0]
    tt = _pick(T, (512, 256, 128))
    H = PEER_HEADS
    hp = PEER_ROUTE_HEADS
    big = jax.ShapeDtypeStruct((H, PEER_NKEYS, T), F32)
    big_spec = pl.BlockSpec((hp, PEER_NKEYS, tt), lambda i, h: (h, 0, i))
    return pl.pallas_call(
        _peer_route_kernel,
        name="peer_route",
        grid=(T // tt, H // hp),
        in_specs=[pl.BlockSpec((tt, hp * PEER_DKEY), lambda i, h: (i, h)),
                  pl.BlockSpec((2, PEER_NKEYS, PEER_DKEY // 2), lambda i, h: (0, 0, 0))],
        out_specs=[big_spec, big_spec, big_spec, big_spec,
                   pl.BlockSpec((hp, 1, tt), lambda i, h: (h, 0, i))],
        out_shape=[big, big, big, big, jax.ShapeDtypeStruct((H, 1, T), F32)],
        compiler_params=_params("parallel", "parallel"),
    )(q, keys)


PEER_TE = 8 * PEER_NKEYS


def _peer_dense_kernel(h_ref, u_ref, v_ref, s1_ref, s2_ref, e1_ref, e2_ref, th_ref, x_ref, g_ref, o_ref,
                       xu_ref, w_ref, acc_ref):
    j = pl.program_id(1)
    tt = h_ref.shape[0]

    @pl.when(j == 0)
    def _():
        acc_ref[...] = jnp.zeros_like(acc_ref)

    xu_ref[...] = lax.dot_general(u_ref[...], h_ref[...], (((1,), (1,)), ((), ())), preferred_element_type=F32)

    for aa in range(PEER_TE // PEER_NKEYS):
        rows = slice(aa * PEER_NKEYS, (aa + 1) * PEER_NKEYS)
        for tc in range(tt // LANES):
            cols = slice(tc * LANES, (tc + 1) * LANES)
            gate = jnp.zeros((PEER_NKEYS, LANES), F32)
            for h in range(PEER_HEADS):
                s1 = s1_ref[h, aa:aa + 1, cols]
                e1 = e1_ref[h, aa:aa + 1, cols]
                gate = gate + jnp.where(s1 + s2_ref[h, :, cols] >= th_ref[h, :, cols], e1 * e2_ref[h, :, cols], 0.0)
            pre = xu_ref[rows, cols]
            act = 0.5 * pre * (1.0 + lax.erf(pre * (2.0 ** -0.5)))
            w_ref[rows, cols] = (gate * act).astype(w_ref.dtype)

    acc_ref[...] += lax.dot_general(w_ref[...], v_ref[...], (((0,), (0,)), ((), ())), preferred_element_type=F32)

    @pl.when(j == pl.num_programs(1) - 1)
    def _():
        o_ref[...] = x_ref[...] + g_ref[0] * acc_ref[...]


def peer_dense(h, u, v, route, x, mods, cidx_tt, tt, gate_slot):
    T, D = h.shape
    s1, s2, e1, e2, th = route
    H = PEER_HEADS
    na = PEER_TE // PEER_NKEYS
    once = pl.Buffered(1)
    tok = pl.BlockSpec((H, PEER_NKEYS, tt), lambda i, j: (0, 0, i), pipeline_mode=once)
    sub = pl.BlockSpec((H, na, tt), lambda i, j: (0, j, i))
    return pl.pallas_call(
        _peer_dense_kernel,
        name="peer_dense",
        grid=(T // tt, PEER_N // PEER_TE),
        in_specs=[pl.BlockSpec((tt, D), lambda i, j: (i, 0), pipeline_mode=once),
                  pl.BlockSpec((PEER_TE, D), lambda i, j: (j, 0)),
                  pl.BlockSpec((PEER_TE, D), lambda i, j: (j, 0)),
                  sub, tok, sub, tok,
                  pl.BlockSpec((H, 1, tt), lambda i, j: (0, 0, i), pipeline_mode=once),
                  pl.BlockSpec((tt, D), lambda i, j: (i, 0), pipeline_mode=once),
                  pl.BlockSpec((1, 1, D), lambda i, j: (cidx_tt(i) * 6 + gate_slot, 0, 0))],
        out_specs=pl.BlockSpec((tt, D), lambda i, j: (i, 0)),
        out_shape=jax.ShapeDtypeStruct((T, D), F32),
        scratch_shapes=[pltpu.VMEM((PEER_TE, tt), F32),
                        pltpu.VMEM((PEER_TE, tt), BF16),
                        pltpu.VMEM((tt, D), F32)],
        compiler_params=_params("parallel", "arbitrary"),
    )(h, u, v, s1, s2, e1, e2, th, x, mods)


HY_HID = 64
HY_BANDS = (HY_EMB - 1) // 2
HY_FREQ_CHUNK = 512


def _hy_filter_kernel(w1t_ref, w1c_ref, w1s_ref, b1_ref, w2_ref, b2_ref, w3_ref, absd_ref, sum_ref, dif_ref, *, L):
    tl = sum_ref.shape[0]
    W = HY_WIDTH
    pos_i = pl.program_id(0) * tl + lax.broadcasted_iota(jnp.int32, (tl, 1), 0)
    pos = pos_i.astype(F32)
    t01 = pos * (1.0 / (L - 1))
    w = (2.0 * math.pi) * pos / L
    band = lax.broadcasted_iota(jnp.int32, (1, HY_BANDS), 1).astype(F32)
    f = 1e-4 + band * ((HY_BANDS - 1 - 1e-4) / (HY_BANDS - 1))
    fw = f * w
    pre = (t01 * w1t_ref[...] + jnp.dot(jnp.cos(fw), w1c_ref[...], preferred_element_type=F32)
           - jnp.dot(jnp.sin(fw), w1s_ref[...], preferred_element_type=F32) + b1_ref[...])
    h = jnp.sin(pre)
    h = jnp.sin(jnp.dot(h, w2_ref[...], preferred_element_type=F32) + b2_ref[...])
    h = jnp.dot(h, w3_ref[...], preferred_element_type=F32)
    window = jnp.exp(-t01 * absd_ref[...])
    for o in range(HY_ORDER):
        hf = h[:, (2 * o) * W:(2 * o + 1) * W] * window
        hb = jnp.where(pos_i == 0, 0.0, h[:, (2 * o + 1) * W:(2 * o + 2) * W] * window)
        sum_ref[:, o * W:(o + 1) * W] = (hf + hb).astype(sum_ref.dtype)
        dif_ref[:, o * W:(o + 1) * W] = (hf - hb).astype(dif_ref.dtype)


def hyena_filter_terms(L, lp):
    W = HY_WIDTH
    tl = _pick(L, (256, 128))
    max_decay = math.log(HY_DECAY_TARGET) / HY_FAST_DECAY
    min_decay = math.log(HY_DECAY_TARGET) / HY_SLOW_DECAY
    absd = jnp.abs(jnp.linspace(min_decay, max_decay, W, dtype=F32)).reshape(1, W)
    w1 = lp["hy_w1"]
    full = lambda a: pl.BlockSpec(a.shape, lambda i: (0,) * a.ndim)
    args = (w1[0:1], w1[1:1 + HY_BANDS], w1[1 + HY_BANDS:], lp["hy_b1"].reshape(1, -1), lp["hy_w2"],
            lp["hy_b2"].reshape(1, -1), lp["hy_w3"], absd)
    out = jax.ShapeDtypeStruct((L, HY_ORDER * W), BF16)
    return pl.pallas_call(
        functools.partial(_hy_filter_kernel, L=L),
        grid=(L // tl,),
        in_specs=[full(a) for a in args],
        out_specs=[pl.BlockSpec((tl, HY_ORDER * W), lambda i: (i, 0))] * 2,
        out_shape=[out, out],
        compiler_params=_params("parallel"),
    )(*args)


def _dft_matrices(L):
    k = jnp.arange(L, dtype=jnp.int32)
    m = (k[:, None] * k[None, :]) % (2 * L)
    ang = m.astype(F32) * (math.pi / L)
    return jnp.stack([jnp.cos(ang), -jnp.sin(ang)]).astype(BF16)


def _hy_conv_kernel(v_ref, x1_ref, x2_ref, cv_ref, c1_ref, c2_ref, f_ref, ka_ref, kb_ref, ks_ref, bias_ref, o_ref,
                    y_ref, yb_ref, conv_ref):
    L = v_ref.shape[1]
    N = 2 * L
    tw = o_ref.shape[2]
    row = lax.broadcasted_iota(jnp.int32, (L, 1), 0)
    nyq = jnp.where(row % 2 == 0, 1.0, -1.0)
    fchunk = min(L, HY_FREQ_CHUNK)

    def short_conv(x_ref, c_ref):
        x = x_ref[0].astype(F32)
        prev = jnp.where(row >= 1, pltpu.roll(x, 1, 0), 0.0)
        nxt = jnp.where(row <= L - 2, pltpu.roll(x, L - 1, 0), 0.0)
        return prev * c_ref[0:1, :] + x * c_ref[1:2, :] + nxt * c_ref[2:3, :]

    y_ref[...] = short_conv(v_ref, cv_ref)
    for o, (x_ref, c_ref) in enumerate(((x1_ref, c1_ref), (x2_ref, c2_ref))):
        cols = slice(o * tw, (o + 1) * tw)
        y = y_ref[...]
        k_nyq = jnp.sum(nyq * ks_ref[:, cols].astype(F32), axis=0, keepdims=True)
        u_nyq = jnp.sum(nyq * y, axis=0, keepdims=True)
        conv_ref[...] = nyq * (u_nyq * k_nyq * (1.0 / N)) + bias_ref[o:o + 1, :] * y
        yb_ref[...] = y.astype(BF16)

        def freq_chunk(i, carry):
            f0 = pl.multiple_of(i * fchunk, fchunk)
            fr = pl.ds(f0, fchunk)
            yb = yb_ref[...]
            ka = ka_ref[fr, cols].astype(F32)
            kb = kb_ref[fr, cols].astype(F32)
            ua = jnp.dot(f_ref[0, fr, :], yb, preferred_element_type=F32)
            ub = jnp.dot(f_ref[1, fr, :], yb, preferred_element_type=F32)
            k_idx = f0 + lax.broadcasted_iota(jnp.int32, (fchunk, 1), 0)
            sc = jnp.where(k_idx == 0, 1.0 / N, 2.0 / N)
            ya = (sc * (ua * ka - ub * kb)).astype(BF16)
            yb2 = (sc * (ua * kb + ub * ka)).astype(BF16)
            conv_ref[...] += (jnp.dot(f_ref[0, :, fr], ya, preferred_element_type=F32)
                              + jnp.dot(f_ref[1, :, fr], yb2, preferred_element_type=F32))
            return carry

        lax.fori_loop(0, L // fchunk, freq_chunk, 0)
        y_ref[...] = short_conv(x_ref, c_ref) * conv_ref[...]
    o_ref[0] = y_ref[...].astype(o_ref.dtype)


def hyena(hy_raw, B, b_off, lp):
    _, L, _ = hy_raw.shape
    W = HY_WIDTH
    tw = 256
    nw = W // tw
    fsum, fdif = hyena_filter_terms(L, lp)
    F = lp["dft"][L]
    ka = matmul(F[0], fsum, BF16)
    kb = matmul(F[1], fdif, BF16)
    regroup = lambda a: a.reshape(L, HY_ORDER, nw, tw).transpose(0, 2, 1, 3).reshape(L, nw * HY_ORDER * tw)
    ka, kb, ks = regroup(ka), regroup(kb), regroup(fsum)
    cw = lp["hy_conv"]
    xspec = lambda g: pl.BlockSpec((1, L, tw), lambda j, b: (b + b_off, 0, g * nw + j))
    cspec = lambda g: pl.BlockSpec((HY_SHORT, tw), lambda j, b: (0, g * nw + j))
    kspec = pl.BlockSpec((L, HY_ORDER * tw), lambda j, b: (0, j), pipeline_mode=pl.Buffered(1))
    return pl.pallas_call(
        _hy_conv_kernel,
        name="hy_conv",
        grid=(nw, B),
        in_specs=[xspec(0), xspec(1), xspec(2), cspec(0), cspec(1), cspec(2),
                  pl.BlockSpec((2, L, L), lambda j, b: (0, 0, 0), pipeline_mode=pl.Buffered(1)),
                  kspec, kspec, kspec,
                  pl.BlockSpec((HY_ORDER, tw), lambda j, b: (0, j))],
        out_specs=pl.BlockSpec((1, L, tw), lambda j, b: (b, 0, j)),
        out_shape=jax.ShapeDtypeStruct((B, L, W), BF16),
        scratch_shapes=[pltpu.VMEM((L, tw), F32), pltpu.VMEM((L, tw), BF16), pltpu.VMEM((L, tw), F32)],
        compiler_params=_params("parallel", "parallel"),
    )(hy_raw, hy_raw, hy_raw, cw, cw, cw, F, ka, kb, ks, lp["hy_bias"])


def _dn_prep_kernel(x_ref, w_ref, o_ref, *, n_q_tiles):
    L = x_ref.shape[1]
    tc = x_ref.shape[2]
    x = x_ref[0].astype(F32)
    row = lax.broadcasted_iota(jnp.int32, (L, 1), 0)
    half = DN_CONV // 2
    acc = x * w_ref[half:half + 1, :]
    for d in range(-half, half + 1):
        if d == 0:
            continue
        shifted = pltpu.roll(x, (-d) % L, 0)
        valid = jnp.logical_and(row + d >= 0, row + d <= L - 1)
        acc = acc + jnp.where(valid, shifted, 0.0) * w_ref[half + d:half + d + 1, :]
    y = acc * jax.nn.sigmoid(acc)
    j = pl.program_id(1)
    for g in range(tc // DN_DK):
        yg = y[:, g * DN_DK:(g + 1) * DN_DK]
        inv = lax.rsqrt(jnp.sum(yg * yg, axis=-1, keepdims=True) + EPS)
        fac = jnp.where(j < n_q_tiles, inv * DN_DK ** -0.5, jnp.where(j < 2 * n_q_tiles, inv, 1.0))
        o_ref[0, :, g * DN_DK:(g + 1) * DN_DK] = yg * fac


def dn_prep(qkv_raw, B, b_off, conv_w):
    _, L, CH = qkv_raw.shape
    tc = 256
    return pl.pallas_call(
        functools.partial(_dn_prep_kernel, n_q_tiles=DN_HEADS * DN_DK // tc),
        grid=(B, CH // tc),
        in_specs=[pl.BlockSpec((1, L, tc), lambda b, j: (b + b_off, 0, j)),
                  pl.BlockSpec((DN_CONV, tc), lambda b, j: (0, j))],
        out_specs=pl.BlockSpec((1, L, tc), lambda b, j: (b, 0, j)),
        out_shape=jax.ShapeDtypeStruct((B, L, CH), F32),
        compiler_params=_params("parallel", "parallel"),
    )(qkv_raw, conv_w)


def _dn_gate_kernel(ab_ref, alog_ref, dt_ref, o_ref):
    n = 2 * DN_HEADS
    a = ab_ref[:, :n] + dt_ref[...]
    softplus = jnp.maximum(a, 0.0) + jnp.log1p(jnp.exp(-jnp.abs(a)))
    o_ref[:, :n] = -jnp.exp(alog_ref[...]) * softplus
    o_ref[:, n:] = jax.nn.sigmoid(ab_ref[:, n:])


def dn_gates(p_ab, a_log, dt_bias):
    T, n2 = p_ab.shape
    tm = _pick(T, (2048, 1024, 512, 256, 128))
    n = 2 * DN_HEADS
    return pl.pallas_call(
        _dn_gate_kernel,
        grid=(T // tm,),
        in_specs=[pl.BlockSpec((tm, n2), lambda i: (i, 0)),
                  pl.BlockSpec((1, n), lambda i: (0, 0)),
                  pl.BlockSpec((1, n), lambda i: (0, 0))],
        out_specs=pl.BlockSpec((tm, n2), lambda i: (i, 0)),
        out_shape=jax.ShapeDtypeStruct((T, n2), F32),
        compiler_params=_params("parallel"),
    )(p_ab, a_log.reshape(1, n), dt_bias.reshape(1, n))


def _split3(x):
    hi = x.astype(BF16)
    r1 = x - hi.astype(F32)
    mid = r1.astype(BF16)
    lo = (r1 - mid.astype(F32)).astype(BF16)
    return hi, mid, lo


def _dn_chunk_kernel(qf_ref, qb_ref, gcf_ref, gcb_ref, grf_ref, grb_ref, s0_ref, of_ref, ob_ref, sout_ref, s_ref):
    n = pl.program_id(1)
    C = DN_CHUNK
    H = DN_HEADS
    nt = (((1,), (1,)), ((), ()))

    @pl.when(n == 0)
    def _():
        s_ref[...] = s0_ref[0]

    r = lax.broadcasted_iota(jnp.int32, (C, C), 0)
    c = lax.broadcasted_iota(jnp.int32, (C, C), 1)
    eye = (r == c).astype(F32)

    def mm(x, y, dims=None):
        x, y = x.astype(BF16), y.astype(BF16)
        if dims is None:
            return jnp.dot(x, y, preferred_element_type=F32)
        return lax.dot_general(x, y, dims, preferred_element_type=F32)

    incl, strict, gcol, gc_col, gc_row, g_tot = [], [], [], [], [], []
    for d, (gc_ref, gr_ref) in enumerate(((gcf_ref, grf_ref), (gcb_ref, grb_ref))):
        lag = (r - c) if d == 0 else (c - r)
        incl.append(lag >= 0)
        strict.append(lag > 0)
        tri = jnp.where(incl[d], 1.0, 0.0).astype(BF16)
        g_c = gc_ref[0, 0]
        g_r = gr_ref[0, 0, 0]
        gcol.append(g_c)
        gc_col.append(sum(jnp.dot(tri, p, preferred_element_type=F32) for p in _split3(g_c)))
        row = sum(lax.dot_general(p, tri, nt, preferred_element_type=F32) for p in _split3(g_r))
        gc_row.append(row)
        g_tot.append(row[:, C - 1:C] if d == 0 else row[:, 0:1])

    chains = [(d, h) for d in range(2) for h in range(H)]
    idx = range(len(chains))
    qkv = (qf_ref, qb_ref)
    q = [qkv[d][0, :, h * DN_DK:(h + 1) * DN_DK] for d, h in chains]
    k = [qkv[d][0, :, (H + h) * DN_DK:(H + h + 1) * DN_DK] for d, h in chains]
    v = [qkv[d][0, :, 2 * H * DN_DK + h * DN_DV:2 * H * DN_DK + (h + 1) * DN_DV] for d, h in chains]
    gc = [gc_col[d][:, h:h + 1] for d, h in chains]
    beta = [gcol[d][:, H + h:H + h + 1] for d, h in chains]
    g_last = [g_tot[d][h:h + 1, :] for d, h in chains]
    eg = [jnp.exp(gc[i]) for i in idx]
    decay = [jnp.exp(jnp.where(incl[d], gc[i] - gc_row[d][h:h + 1, :], NEG_BIG)) for i, (d, h) in enumerate(chains)]
    kk = [mm(k[i], k[i], nt) for i in idx]
    qk = [mm(q[i], k[i], nt) for i in idx]
    a = [jnp.where(strict[d], beta[i] * kk[i] * decay[i], 0.0) for i, (d, h) in enumerate(chains)]
    qk = [jnp.where(incl[d], qk[i] * decay[i], 0.0) for i, (d, h) in enumerate(chains)]
    inv = None
    for l in range(C.bit_length() - 1):
        couple = jnp.logical_and((r >> (l + 1)) == (c >> (l + 1)), (r >> l) != (c >> l))
        a_l = [jnp.where(couple, a[i], 0.0) for i in idx]
        if inv is None:
            inv = [eye - a_l[i] for i in idx]
        else:
            t = [mm(inv[i], a_l[i]) for i in idx]
            inv = [inv[i] - mm(t[i], inv[i]) for i in idx]
    rhs = [jnp.concatenate([v[i] * beta[i], k[i] * (beta[i] * eg[i])], axis=1) for i in idx]
    sol = [mm(inv[i], rhs[i]) for i in idx]
    S = [s_ref[d, h] for d, h in chains]
    v_new = [sol[i][:, :DN_DV] - mm(sol[i][:, DN_DV:], S[i]) for i in idx]
    o = [mm(q[i] * eg[i], S[i]) + mm(qk[i], v_new[i]) for i in idx]
    upd = [mm(k[i] * jnp.exp(g_last[i] - gc[i]), v_new[i], (((0,), (0,)), ((), ()))) for i in idx]
    o_refs = (of_ref, ob_ref)
    for i, (d, h) in enumerate(chains):
        o_refs[d][0, :, h * DN_DV:(h + 1) * DN_DV] = o[i]
        s_ref[d, h] = S[i] * jnp.exp(g_last[i]) + upd[i]

    @pl.when(n == pl.num_programs(1) - 1)
    def _():
        sout_ref[0] = s_ref[...]


def dn_scan(qkv, gates, s0):
    B, L, CH = qkv.shape
    C, H = DN_CHUNK, DN_HEADS
    N = L // C
    g4 = gates.reshape(B, L, 2, 2, H)
    gcol = g4.transpose(0, 3, 1, 2, 4).reshape(B, 2, L, 2 * H)
    grow = gcol.reshape(B, 2, N, C, 2 * H).transpose(0, 1, 2, 4, 3)
    rev = lambda n: N - 1 - n
    out = jax.ShapeDtypeStruct((B, L, H * DN_DV), F32)
    return pl.pallas_call(
        _dn_chunk_kernel,
        name="dn_chunk",
        grid=(B, N),
        in_specs=[pl.BlockSpec((1, C, CH), lambda b, n: (b, n, 0)),
                  pl.BlockSpec((1, C, CH), lambda b, n: (b, rev(n), 0)),
                  pl.BlockSpec((1, 1, C, 2 * H), lambda b, n: (b, 0, n, 0)),
                  pl.BlockSpec((1, 1, C, 2 * H), lambda b, n: (b, 1, rev(n), 0)),
                  pl.BlockSpec((1, 1, 1, 2 * H, C), lambda b, n: (b, 0, n, 0, 0)),
                  pl.BlockSpec((1, 1, 1, 2 * H, C), lambda b, n: (b, 1, rev(n), 0, 0)),
                  pl.BlockSpec((1, 2, H, DN_DK, DN_DV), lambda b, n: (b, 0, 0, 0, 0))],
        out_specs=[pl.BlockSpec((1, C, H * DN_DV), lambda b, n: (b, n, 0)),
                   pl.BlockSpec((1, C, H * DN_DV), lambda b, n: (b, rev(n), 0)),
                   pl.BlockSpec((1, 2, H, DN_DK, DN_DV), lambda b, n: (b, 0, 0, 0, 0))],
        out_shape=[out, out, jax.ShapeDtypeStruct((B, 2, H, DN_DK, DN_DV), F32)],
        scratch_shapes=[pltpu.VMEM((2, H, DN_DK, DN_DV), F32)],
        compiler_params=_params("parallel", "arbitrary"),
    )(qkv, qkv, gcol, gcol, grow, grow, s0)


def _dn_out_kernel(of_ref, ob_ref, z_ref, g_ref, o_ref):
    for h in range(DN_HEADS):
        cols = slice(h * DN_DV, (h + 1) * DN_DV)
        o = of_ref[0, :, cols] + ob_ref[0, :, cols]
        z = z_ref[0, :, cols].astype(F32)
        o_ref[0, :, cols] = (_rms(o, g_ref[...]) * (z * jax.nn.sigmoid(z))).astype(o_ref.dtype)


def dn_output(o_f, o_b, z, b_off, g):
    B, L, W = o_f.shape
    tm = _pick(L, (512, 256, 128))
    spec = pl.BlockSpec((1, tm, W), lambda b, i: (b, i, 0))
    return pl.pallas_call(
        _dn_out_kernel,
        grid=(B, L // tm),
        in_specs=[spec, spec,
                  pl.BlockSpec((1, tm, W), lambda b, i: (b + b_off, i, 0)),
                  pl.BlockSpec((1, DN_DV), lambda b, i: (0, 0))],
        out_specs=spec,
        out_shape=jax.ShapeDtypeStruct((B, L, W), BF16),
        compiler_params=_params("parallel", "parallel"),
    )(o_f, o_b, z, g.reshape(1, DN_DV))


def gated_deltanet(qkv_raw, z, B, b_off, gates, lp, s0):
    qkv = dn_prep(qkv_raw, B, b_off, lp["dn_conv"])
    o_f, o_b, s_fin = dn_scan(qkv, gates, s0)
    return dn_output(o_f, o_b, z, b_off, lp["dn_out_norm"]), s_fin


def _axial_rope(x):
    L = x.shape[1]
    half = x.shape[-1] // 2
    t = jnp.arange(L)
    inv = ROPE_THETA ** (-jnp.arange(0, half, 2, dtype=F32) / half)
    out = []
    for pos, xa in ((t // GRID_W, x[..., :half]), (t % GRID_W, x[..., half:])):
        ang = pos.astype(F32)[:, None] * inv[None, :]
        cos, sin = jnp.cos(ang)[:, None, :], jnp.sin(ang)[:, None, :]
        x1, x2 = xa[..., : half // 2], xa[..., half // 2:]
        out += [x1 * cos - x2 * sin, x2 * cos + x1 * sin]
    return jnp.concatenate(out, axis=-1)


def _layer(x, lp, mods, dims, caches):
    Bc, Lc, Bl, Ll = dims
    Tc, Tl = Bc * Lc, Bl * Ll
    T = Tc + Tl
    tm = _pick(math.gcd(Tc, Ll), (1024, 512, 256, 128))

    def make_cidx(tile):
        nct, tpl = Tc // tile, Ll // tile
        return lambda i: jnp.where(i < nct, 0, 1 + (i - nct) // tpl)

    cidx = make_cidx(tm)
    h = modulate(x, lp["norm1_g"], mods, cidx, tm, 0, 1)

    w_in = lp["w_in"]
    cuts = np.cumsum((0,) + IN_SIZES)
    col = lambda a, b: w_in[:, cuts[a]:cuts[b]].astype(BF16)
    p_mla = matmul(h, col(0, 3))
    p_dn = matmul(h, col(3, 4), BF16)
    p_z = matmul(h, col(4, 5), BF16)
    p_ab = matmul(h, col(5, 7))
    p_hy = matmul(h, col(7, 8), BF16)
    p_na = matmul(h, col(8, 9), BF16)
    na_kv_ctx = matmul(h, w_in[:, cuts[8] + BRANCH_W:cuts[9]].astype(BF16), F32, rows=Tc)
    p_gate = matmul(h, col(9, 10), BF16)

    cq, ckv, krope = p_mla[:, :MLA_Q_RANK], p_mla[:, MLA_Q_RANK:MLA_Q_RANK + MLA_KV_RANK], p_mla[:, -MLA_ROPE:]

    cq_n = rmsnorm(cq, lp["mla_q_norm"], BF16)
    ckv_n = rmsnorm(ckv, lp["mla_kv_norm"], F32)
    def seqs(p, L, start, B):
        if start % L == 0 and T % L == 0:
            return p.reshape(T // L, L, -1), start // L
        return p[start:start + B * L].reshape(B, L, -1), 0

    ctx_of = lambda p: seqs(p, Lc, 0, Bc)
    lat_of = lambda p: seqs(p, Ll, Tc, Bl)

    w_qb = lp["mla_w_qb"].reshape(MLA_Q_RANK, MLA_HEADS, MLA_NOPE + MLA_ROPE)
    q_nope = matmul(cq_n, w_qb[:, :, :MLA_NOPE].reshape(MLA_Q_RANK, -1).astype(BF16), BF16)
    q_rope = matmul(cq_n, w_qb[:, :, MLA_NOPE:].reshape(MLA_Q_RANK, -1).astype(BF16))
    w_kvb = lp["mla_w_kvb"].astype(BF16)
    kv_all = matmul(ckv_n.astype(BF16), w_kvb, BF16)
    ckv_ctx, krope_ctx, nak_ctx, nav_ctx, s_f0, s_b0 = caches
    kv_cache = matmul(ckv_ctx.reshape(-1, MLA_KV_RANK).astype(BF16), w_kvb, BF16)
    Lp = ckv_ctx.shape[1]
    head_major = lambda t: t.transpose(0, 2, 1, 3).astype(BF16)

    (qn_c, qoff_c), (kv_c, koff_c) = ctx_of(q_nope), ctx_of(kv_all)
    o_a_ctx = mla_attention(qn_c, head_major(q_rope[:Tc].reshape(Bc, Lc, MLA_HEADS, MLA_ROPE)), kv_c,
                            krope[:Tc].reshape(Bc, Lc, MLA_ROPE).astype(BF16), Bc, qoff_c, koff_c)
    qn_l, qoff_l = lat_of(q_nope)
    qr_l = head_major(_axial_rope(q_rope[Tc:].reshape(Bl, Ll, MLA_HEADS, MLA_ROPE)))
    kr_l = _axial_rope(krope[Tc:].reshape(Bl, Ll, 1, MLA_ROPE)).reshape(Bl, Ll, MLA_ROPE)
    kr_l = jnp.concatenate([kr_l, krope_ctx], axis=1).astype(BF16)
    kv_l = jnp.concatenate([kv_all[Tc:].reshape(Bl, Ll, -1), kv_cache.reshape(Bl, Lp, -1)], axis=1)
    o_a_lat = mla_attention(qn_l, qr_l, kv_l, kr_l, Bl, qoff_l, 0)
    o_a = jnp.concatenate([o_a_ctx.reshape(Tc, BRANCH_W), o_a_lat.reshape(Tl, BRANCH_W)])

    gates = dn_gates(p_ab, lp["dn_a_log"], lp["dn_dt_bias"])
    zero_state = jnp.zeros((Bc, 2, DN_HEADS, DN_DK, DN_DV), F32)
    (dn_c, off_c), (dn_l, off_l) = ctx_of(p_dn), lat_of(p_dn)
    o_b_ctx, s_ctx = gated_deltanet(dn_c, ctx_of(p_z)[0], Bc, off_c, gates[:Tc].reshape(Bc, Lc, -1), lp, zero_state)
    o_b_lat, _ = gated_deltanet(dn_l, lat_of(p_z)[0], Bl, off_l, gates[Tc:].reshape(Bl, Ll, -1), lp,
                                jnp.stack([s_f0, s_b0], axis=1))
    s_f, s_b = s_ctx[:, 0], s_ctx[:, 1]
    o_b = jnp.concatenate([o_b_ctx.reshape(Tc, BRANCH_W), o_b_lat.reshape(Tl, BRANCH_W)])

    (hy_c, off_c), (hy_l, off_l) = ctx_of(p_hy), lat_of(p_hy)
    o_c = jnp.concatenate([hyena(hy_c, Bc, off_c, lp).reshape(Tc, BRANCH_W),
                           hyena(hy_l, Bl, off_l, lp).reshape(Tl, BRANCH_W)])

    (na_c, off_c), (na_l, off_l) = ctx_of(p_na), lat_of(p_na)
    o_d_ctx = attention(na_c, na_c, na_c, NA_HEADS, NA_HD, NA_HD, NA_SCALE, k_off=NA_HEADS, v_off=2 * NA_HEADS,
                        batch=Bc, b_off=off_c)
    o_d_lat = neighbourhood_attention(na_l, Bl, off_l, nak_ctx.reshape(Bl, Lp, -1).astype(BF16),
                                      nav_ctx.reshape(Bl, Lp, -1).astype(BF16), lp["na_bias"])
    o_d = jnp.concatenate([o_d_ctx.reshape(Tc, BRANCH_W), o_d_lat.reshape(Tl, BRANCH_W)])

    merged = merge_branches((o_a, o_b, o_c, o_d), p_gate, lp["w_branch"].astype(BF16), tm)
    x = matmul_gated_residual(merged, lp["w_out"].astype(BF16), x, mods, cidx, tm, 2)

    h2 = modulate(x, lp["norm2_g"], mods, cidx, tm, 3, 4)
    pq = matmul(h2, lp["peer_wq"].astype(BF16))
    route = peer_route(pq, lp["peer_keys"])
    tt = _pick(math.gcd(Tc, Ll), (512, 256, 128))
    x = peer_dense(h2, lp["peer_u"].astype(BF16), lp["peer_v"].astype(BF16), route, x, mods, make_cidx(tt), tt, 5)

    ctx_out = (ckv_n[:Tc].reshape(Bc, Lc, MLA_KV_RANK), krope[:Tc].reshape(Bc, Lc, MLA_ROPE),
               na_kv_ctx[:, :BRANCH_W].reshape(Bc, Lc, NA_HEADS, NA_HD),
               na_kv_ctx[:, BRANCH_W:].reshape(Bc, Lc, NA_HEADS, NA_HD), s_f, s_b)
    return x, ctx_out


def kernel(x_prompt, x_sample, cache_mla_ckv, cache_mla_krope, cache_na_k, cache_na_v, state_dn_fwd, state_dn_bwd, c, c_ctx, norm1_g, w_ada, b_ada, w_in, mla_q_norm, mla_w_qb, mla_kv_norm, mla_w_kvb, dn_conv, dn_a_log, dn_dt_bias, dn_out_norm, hy_conv, hy_w1, hy_b1, hy_w2, hy_b2, hy_w3, hy_bias, na_rpb, w_branch, w_out, norm2_g, peer_wq, peer_keys, peer_u, peer_v, final_g):
    Bc, Lc, D = x_prompt.shape
    Bl, Ll, _ = x_sample.shape
    depth = w_in.shape[0]
    Tc = Bc * Lc
    x = jnp.concatenate([x_prompt.reshape(Tc, D), x_sample.reshape(Bl * Ll, D)])
    n_cond = 1 + Bl
    cond_rows = -(-n_cond // SUBLANES) * SUBLANES
    cond = jnp.concatenate([c_ctx[None], c, jnp.zeros((cond_rows - n_cond, D), F32)])

    names = ("norm1_g", "w_in", "mla_q_norm", "mla_w_qb", "mla_kv_norm", "mla_w_kvb", "dn_conv", "dn_a_log",
             "dn_dt_bias", "dn_out_norm", "hy_conv", "hy_w1", "hy_b1", "hy_w2", "hy_b2", "hy_w3", "hy_bias",
             "na_rpb", "w_branch", "w_out", "norm2_g", "peer_wq", "peer_keys", "peer_u", "peer_v")
    vals = (norm1_g, w_in, mla_q_norm, mla_w_qb, mla_kv_norm, mla_w_kvb, dn_conv, dn_a_log, dn_dt_bias,
            dn_out_norm, hy_conv, hy_w1, hy_b1, hy_w2, hy_b2, hy_w3, hy_bias, na_rpb, w_branch, w_out, norm2_g,
            peer_wq, peer_keys, peer_u, peer_v)
    dft = {L: _dft_matrices(L) for L in {Lc, Ll}}
    na_bias = _na_bias_tables(na_rpb.reshape((depth * NA_HEADS,) + na_rpb.shape[2:]), Ll // GRID_W)
    na_bias = na_bias.reshape((depth, NA_HEADS) + na_bias.shape[1:])
    ctx = []
    for l in range(depth):
        lp = {n: v[l] for n, v in zip(names, vals)}
        lp["dft"] = dft
        lp["na_bias"] = na_bias[l]
        mods = ada_modulation(cond, w_ada[l], b_ada[l]).reshape(cond_rows * 6, 1, D)
        caches = (cache_mla_ckv[:, l], cache_mla_krope[:, l], cache_na_k[:, l], cache_na_v[:, l],
                  state_dn_fwd[:, l], state_dn_bwd[:, l])
        x, ctx_out = _layer(x, lp, mods, (Bc, Lc, Bl, Ll), caches)
        ctx.append(ctx_out)
    y_ctx = rmsnorm(x, final_g, rows=Tc)
    y_lat = rmsnorm(x, final_g, start=Tc, rows=Bl * Ll)
    stack = lambda k: jnp.stack([t[k] for t in ctx], axis=1)
    return (y_ctx.reshape(Bc, Lc, D), y_lat.reshape(Bl, Ll, D), stack(0), stack(1), stack(2), stack(3),
            stack(4), stack(5))
```

```python
import functools
import math

import numpy as np
import jax
import jax.numpy as jnp
from jax import lax
from jax.experimental import pallas as pl
from jax.experimental.pallas import tpu as pltpu

F32 = jnp.float32
BF16 = jnp.bfloat16

V7X_VMEM_BYTES = 64 * 1024 * 1024
VMEM_LIMIT = V7X_VMEM_BYTES - 8 * 1024 * 1024
LANES = 128
SUBLANES = 8

D_MODEL = 2048
GRID_W = 64
EPS = 1e-6
N_BRANCH = 4
BRANCH_W = D_MODEL // 2

MLA_HEADS = 8
MLA_NOPE = 128
MLA_ROPE = 64
MLA_V = BRANCH_W // MLA_HEADS
MLA_Q_RANK = D_MODEL // 4
MLA_KV_RANK = D_MODEL // 8
MLA_SCALE = (MLA_NOPE + MLA_ROPE) ** -0.5
ROPE_THETA = 10000.0

DN_HEADS = 8
DN_DK = 128
DN_DV = BRANCH_W // DN_HEADS
DN_CONV = 5
DN_CHUNK = 64
DN_CONV_CH = 2 * DN_HEADS * DN_DK + DN_HEADS * DN_DV

HY_WIDTH = BRANCH_W
HY_ORDER = 2
HY_SHORT = 3
HY_EMB = 33
HY_FAST_DECAY = 0.3
HY_SLOW_DECAY = 1.5
HY_DECAY_TARGET = 1e-2

NA_HEADS = 8
NA_HD = BRANCH_W // NA_HEADS
NA_KH = 8
NA_KW = 16
NA_SCALE = NA_HD ** -0.5
NA_QROWS = 4
NA_REGION = NA_KH + NA_QROWS
NEG_BIG = -1e30

PEER_HEADS = 8
PEER_NKEYS = 128
PEER_N = PEER_NKEYS * PEER_NKEYS
PEER_DKEY = 256
PEER_TOPK = 16

IN_SIZES = (MLA_Q_RANK, MLA_KV_RANK, MLA_ROPE, DN_CONV_CH, DN_HEADS * DN_DV, 2 * DN_HEADS, 2 * DN_HEADS,
            (HY_ORDER + 1) * HY_WIDTH, 3 * NA_HEADS * NA_HD, N_BRANCH * D_MODEL)


def _params(*sem):
    return pltpu.CompilerParams(dimension_semantics=sem, vmem_limit_bytes=VMEM_LIMIT)


def _pick(n, prefs):
    for p in prefs:
        if n % p == 0:
            return p
    return n


def _mm_kernel(x_ref, w_ref, o_ref):
    o_ref[...] = jnp.dot(x_ref[...], w_ref[...], preferred_element_type=F32).astype(o_ref.dtype)


def matmul(x, w, out_dtype=F32, tm=None, tn=None, rows=None):
    M, K = x.shape
    M = rows or M
    N = w.shape[1]
    tm = tm or _pick(M, (1024, 512, 256, 128))
    tn = tn or _pick(N, (512, 256, 128))
    return pl.pallas_call(
        _mm_kernel,
        grid=(M // tm, N // tn),
        in_specs=[pl.BlockSpec((tm, K), lambda i, j: (i, 0)),
                  pl.BlockSpec((K, tn), lambda i, j: (0, j))],
        out_specs=pl.BlockSpec((tm, tn), lambda i, j: (i, j)),
        out_shape=jax.ShapeDtypeStruct((M, N), out_dtype),
        compiler_params=_params("parallel", "parallel"),
    )(x, w)


def _ada_kernel(c_ref, w_ref, b_ref, o_ref):
    c = c_ref[...]
    o_ref[...] = jnp.dot(c * jax.nn.sigmoid(c), w_ref[...], preferred_element_type=F32) + b_ref[...]


def ada_modulation(cond, w_ada, b_ada):
    R, K = cond.shape
    N = w_ada.shape[1]
    tn = _pick(N, (1536, 1024, 512))
    return pl.pallas_call(
        _ada_kernel,
        grid=(N // tn,),
        in_specs=[pl.BlockSpec((R, K), lambda j: (0, 0)),
                  pl.BlockSpec((K, tn), lambda j: (0, j)),
                  pl.BlockSpec((1, tn), lambda j: (0, j))],
        out_specs=pl.BlockSpec((R, tn), lambda j: (0, j)),
        out_shape=jax.ShapeDtypeStruct((R, N), F32),
        compiler_params=_params("parallel"),
    )(cond, w_ada, b_ada.reshape(1, N))


def _rms(x, g):
    return x * lax.rsqrt(jnp.mean(x * x, axis=-1, keepdims=True) + EPS) * g


def _modulate_kernel(x_ref, g_ref, sh_ref, sc_ref, o_ref):
    y = _rms(x_ref[...], g_ref[...])
    o_ref[...] = (y * (1.0 + sc_ref[0]) + sh_ref[0]).astype(o_ref.dtype)


def modulate(x, g, mods, cidx, tm, shift_slot, scale_slot):
    T, D = x.shape
    return pl.pallas_call(
        _modulate_kernel,
        grid=(T // tm,),
        in_specs=[pl.BlockSpec((tm, D), lambda i: (i, 0)),
                  pl.BlockSpec((1, D), lambda i: (0, 0)),
                  pl.BlockSpec((1, 1, D), lambda i: (cidx(i) * 6 + shift_slot, 0, 0)),
                  pl.BlockSpec((1, 1, D), lambda i: (cidx(i) * 6 + scale_slot, 0, 0))],
        out_specs=pl.BlockSpec((tm, D), lambda i: (i, 0)),
        out_shape=jax.ShapeDtypeStruct((T, D), BF16),
        compiler_params=_params("parallel"),
    )(x, g.reshape(1, D), mods, mods)


def _rmsnorm_kernel(x_ref, g_ref, o_ref):
    o_ref[...] = _rms(x_ref[...].astype(F32), g_ref[...]).astype(o_ref.dtype)


def rmsnorm(x, g, out_dtype=F32, start=0, rows=None):
    T, D = x.shape
    T = rows or T
    tm = _pick(math.gcd(T, start) if start else T, (1024, 512, 256, 128))
    first = start // tm
    return pl.pallas_call(
        _rmsnorm_kernel,
        grid=(T // tm,),
        in_specs=[pl.BlockSpec((tm, D), lambda i: (i + first, 0)),
                  pl.BlockSpec((1, D), lambda i: (0, 0))],
        out_specs=pl.BlockSpec((tm, D), lambda i: (i, 0)),
        out_shape=jax.ShapeDtypeStruct((T, D), out_dtype),
        compiler_params=_params("parallel"),
    )(x, g.reshape(1, D))


def _merge_kernel(*refs, n_ctx_tiles):
    ctx_refs, lat_refs = refs[:N_BRANCH], refs[N_BRANCH:2 * N_BRANCH]
    gate_refs = refs[2 * N_BRANCH:3 * N_BRANCH]
    w_ref, o_ref = refs[3 * N_BRANCH:]

    def merge(branch_refs):
        acc = None
        for n, (br, gl) in enumerate(zip(branch_refs, gate_refs)):
            proj = jnp.dot(br[...], w_ref[n], preferred_element_type=F32)
            term = jax.nn.sigmoid(gl[...].astype(F32)) * proj
            acc = term if acc is None else acc + term
        o_ref[...] = acc.astype(o_ref.dtype)

    is_ctx = pl.program_id(0) < n_ctx_tiles
    pl.when(is_ctx)(functools.partial(merge, ctx_refs))
    pl.when(jnp.logical_not(is_ctx))(functools.partial(merge, lat_refs))


def merge_branches(branches_ctx, branches_lat, gate_logits, w_branch, tm):
    Tc, Tl = branches_ctx[0].shape[0], branches_lat[0].shape[0]
    nct, nlt = Tc // tm, Tl // tm
    tn = 512
    ctx_spec = pl.BlockSpec((tm, BRANCH_W), lambda i, j: (jnp.minimum(i, nct - 1), 0))
    lat_spec = pl.BlockSpec((tm, BRANCH_W), lambda i, j: (jnp.maximum(i - nct, 0), 0))
    nj = D_MODEL // tn
    gl_specs = [pl.BlockSpec((tm, tn), functools.partial(lambda i, j, n: (i, n * nj + j), n=n))
                for n in range(N_BRANCH)]
    return pl.pallas_call(
        functools.partial(_merge_kernel, n_ctx_tiles=nct),
        grid=(nct + nlt, D_MODEL // tn),
        in_specs=[ctx_spec] * N_BRANCH + [lat_spec] * N_BRANCH + gl_specs
                 + [pl.BlockSpec((N_BRANCH, BRANCH_W, tn), lambda i, j: (0, 0, j))],
        out_specs=pl.BlockSpec((tm, tn), lambda i, j: (i, j)),
        out_shape=jax.ShapeDtypeStruct((Tc + Tl, D_MODEL), BF16),
        compiler_params=_params("parallel", "parallel"),
    )(*branches_ctx, *branches_lat, gate_logits, gate_logits, gate_logits, gate_logits, w_branch)


def _mm_resid_kernel(m_ref, w_ref, x_ref, g_ref, o_ref):
    o_ref[...] = x_ref[...] + g_ref[0] * jnp.dot(m_ref[...], w_ref[...], preferred_element_type=F32)


def matmul_gated_residual(m, w, x, mods, cidx, tm, gate_slot):
    T, K = m.shape
    N = w.shape[1]
    tn = 512
    return pl.pallas_call(
        _mm_resid_kernel,
        grid=(T // tm, N // tn),
        in_specs=[pl.BlockSpec((tm, K), lambda i, j: (i, 0)),
                  pl.BlockSpec((K, tn), lambda i, j: (0, j)),
                  pl.BlockSpec((tm, tn), lambda i, j: (i, j)),
                  pl.BlockSpec((1, 1, tn), lambda i, j: (cidx(i) * 6 + gate_slot, 0, j))],
        out_specs=pl.BlockSpec((tm, tn), lambda i, j: (i, j)),
        out_shape=jax.ShapeDtypeStruct((T, N), F32),
        compiler_params=_params("parallel", "parallel"),
    )(m, w, x, mods)


def _attn_kernel(q_ref, k_ref, v_ref, o_ref, *, scale):
    s = lax.dot_general(q_ref[0], k_ref[0], (((1,), (1,)), ((), ())), preferred_element_type=F32) * scale
    p = jnp.exp(s - jnp.max(s, axis=-1, keepdims=True))
    l = jnp.sum(p, axis=-1, keepdims=True)
    o = jnp.dot(p.astype(BF16), v_ref[0], preferred_element_type=F32)
    o_ref[0] = (o / l).astype(o_ref.dtype)


def attention(q, k, v, heads, dqk, dv, scale, q_off=0, k_off=0, v_off=0, v_stride=1, k_stride=1, batch=None,
              b_off=0):
    B, Lq, _ = q.shape
    B = batch or B
    Lk = k.shape[1]
    tq = _pick(Lq, (256, 128))
    return pl.pallas_call(
        functools.partial(_attn_kernel, scale=scale),
        grid=(B, heads, Lq // tq),
        in_specs=[pl.BlockSpec((1, tq, dqk), lambda b, h, i: (b + b_off, i, q_off + h)),
                  pl.BlockSpec((1, Lk, dqk), lambda b, h, i: (b + b_off, 0, k_off + h * k_stride)),
                  pl.BlockSpec((1, Lk, dv), lambda b, h, i: (b + b_off, 0, v_off + h * v_stride))],
        out_specs=pl.BlockSpec((1, tq, dv), lambda b, h, i: (b, i, h)),
        out_shape=jax.ShapeDtypeStruct((B, Lq, heads * dv), BF16),
        compiler_params=_params("parallel", "parallel", "parallel"),
    )(q, k, v)


def _mla_attn_kernel(qn_ref, qr_ref, kn_ref, kr_ref, v_ref, o_ref):
    q = jnp.concatenate([qn_ref[0], qr_ref[0, 0]], axis=1)
    k = jnp.concatenate([kn_ref[0], kr_ref[0]], axis=1)
    s = lax.dot_general(q, k, (((1,), (1,)), ((), ())), preferred_element_type=F32) * MLA_SCALE
    p = jnp.exp(s - jnp.max(s, axis=-1, keepdims=True))
    l = jnp.sum(p, axis=-1, keepdims=True)
    o = jnp.dot(p.astype(BF16), v_ref[0], preferred_element_type=F32)
    o_ref[0] = (o / l).astype(o_ref.dtype)


def mla_attention(q_nope, q_rope, kv, k_rope, B, q_off, kv_off):
    _, Lq, _ = q_nope.shape
    Lk = kv.shape[1]
    H = MLA_HEADS
    tq = _pick(Lq, (256, 128))
    return pl.pallas_call(
        _mla_attn_kernel,
        grid=(B, H, Lq // tq),
        in_specs=[pl.BlockSpec((1, tq, MLA_NOPE), lambda b, h, i: (b + q_off, i, h)),
                  pl.BlockSpec((1, 1, tq, MLA_ROPE), lambda b, h, i: (b, h, i, 0)),
                  pl.BlockSpec((1, Lk, MLA_NOPE), lambda b, h, i: (b + kv_off, 0, 2 * h)),
                  pl.BlockSpec((1, Lk, MLA_ROPE), lambda b, h, i: (b, 0, 0)),
                  pl.BlockSpec((1, Lk, MLA_V), lambda b, h, i: (b + kv_off, 0, 2 * h + 1))],
        out_specs=pl.BlockSpec((1, tq, MLA_V), lambda b, h, i: (b, i, h)),
        out_shape=jax.ShapeDtypeStruct((B, Lq, H * MLA_V), BF16),
        compiler_params=_params("parallel", "parallel", "parallel"),
    )(q_nope, q_rope, kv, k_rope, kv)


def _na_kernel(q_ref, k_ref, v_ref, kc_ref, vc_ref, bias_ref, o_ref, *, rows):
    nblk = rows // NA_QROWS
    tq = NA_QROWS * GRID_W
    nkeys = NA_REGION * GRID_W
    nt = (((1,), (1,)), ((), ()))
    kc = kc_ref[0]
    vc = vc_ref[0]
    for rb in range(nblk):
        start = min(max(rb * NA_QROWS - NA_KH // 2, 0), rows - NA_REGION) * GRID_W
        kind = 0 if rb == 0 else (2 if rb == nblk - 1 else 1)
        q = q_ref[0, rb * tq:(rb + 1) * tq, :]
        kr = k_ref[0, start:start + nkeys, :]
        vr = v_ref[0, start:start + nkeys, :]
        s_win = lax.dot_general(q, kr, nt, preferred_element_type=F32) * NA_SCALE + bias_ref[0, kind]
        s_ctx = lax.dot_general(q, kc, nt, preferred_element_type=F32) * NA_SCALE
        m = jnp.maximum(jnp.max(s_win, axis=-1, keepdims=True), jnp.max(s_ctx, axis=-1, keepdims=True))
        p_win = jnp.exp(s_win - m)
        p_ctx = jnp.exp(s_ctx - m)
        l = jnp.sum(p_win, axis=-1, keepdims=True) + jnp.sum(p_ctx, axis=-1, keepdims=True)
        o = (jnp.dot(p_win.astype(BF16), vr, preferred_element_type=F32)
             + jnp.dot(p_ctx.astype(BF16), vc, preferred_element_type=F32))
        o_ref[0, rb * tq:(rb + 1) * tq, :] = (o / l).astype(o_ref.dtype)


def _na_bias_tables(rpb, rows):
    H = rpb.shape[0]
    n_dr, n_dc = 2 * NA_KH - 1, 2 * NA_KW - 1
    assert rows >= NA_REGION + NA_QROWS and rows % NA_QROWS == 0
    nblk = rows // NA_QROWS
    span = 2 * GRID_W
    lo = GRID_W - NA_KW
    v = jnp.pad(rpb.astype(F32), ((0, 0), (0, 0), (lo, span - lo - n_dc)), constant_values=NEG_BIG)
    skew = jnp.tile(v, (1, 1, GRID_W))[..., :GRID_W * (span - 1)].reshape(H, n_dr, GRID_W, span - 1)
    band = skew[..., GRID_W - 1:]
    c = np.arange(GRID_W)[:, None]
    kc = np.arange(GRID_W)[None, :]
    col0 = np.clip(c - NA_KW // 2, 0, GRID_W - NA_KW)
    col_ok = (kc >= col0) & (kc < col0 + NA_KW)
    band = jnp.where(col_ok, band, NEG_BIG)
    band = jnp.concatenate([band, jnp.full((H, 1, GRID_W, GRID_W), NEG_BIG, F32)], axis=1)
    tiles = []
    for rb in (0, 1, nblk - 1):
        start = int(np.clip(rb * NA_QROWS - NA_KH // 2, 0, rows - NA_REGION))
        for qr in range(NA_QROWS):
            r = rb * NA_QROWS + qr
            row0 = int(np.clip(r - NA_KH // 2, 0, rows - NA_KH))
            for j in range(NA_REGION):
                kr = start + j
                tiles.append(kr - r + NA_KH - 1 if row0 <= kr < row0 + NA_KH else n_dr)
    t = jnp.take(band, np.asarray(tiles, np.int32), axis=1).reshape(H, 3, NA_QROWS, NA_REGION, GRID_W, GRID_W)
    return t.transpose(0, 1, 2, 4, 3, 5).reshape(H, 3, NA_QROWS * GRID_W, NA_REGION * GRID_W)


def neighbourhood_attention(na_in, B, b_off, k_ctx, v_ctx, bias):
    _, L, _ = na_in.shape
    Lc = k_ctx.shape[1]
    rows = L // GRID_W
    tq = NA_QROWS * GRID_W
    nkeys = NA_REGION * GRID_W
    H = NA_HEADS
    seq = lambda col: pl.BlockSpec((1, L, NA_HD), lambda h, b: (b + b_off, 0, col * H + h))
    ctx = pl.BlockSpec((1, Lc, NA_HD), lambda h, b: (b, 0, h))
    return pl.pallas_call(
        functools.partial(_na_kernel, rows=rows),
        grid=(H, B),
        in_specs=[seq(0), seq(1), seq(2), ctx, ctx,
                  pl.BlockSpec((1, 3, tq, nkeys), lambda h, b: (h, 0, 0, 0))],
        out_specs=pl.BlockSpec((1, L, NA_HD), lambda h, b: (b, 0, h)),
        out_shape=jax.ShapeDtypeStruct((B, L, H * NA_HD), BF16),
        compiler_params=_params("parallel", "parallel"),
    )(na_in, na_in, na_in, k_ctx, v_ctx, bias)


def _top_rows(works, k):
    outs = [[] for _ in works]
    for _ in range(k):
        mx = [jnp.max(w, axis=0, keepdims=True) for w in works]
        for out, m in zip(outs, mx):
            out.append(m)
        works = [jnp.where(w == m, -jnp.inf, w) for w, m in zip(works, mx)]
    return outs


_PEER_PAIRS = [(i, k) for i in range(PEER_TOPK) for k in range(PEER_TOPK) if (i + 1) * (k + 1) <= PEER_TOPK]
PEER_ROUTE_HEADS = 2


def _peer_route_kernel(q_ref, keys_ref, s1_ref, s2_ref, e1_ref, e2_ref, th_ref):
    half = PEER_DKEY // 2
    nt = (((1,), (1,)), ((), ()))
    heads = range(s1_ref.shape[0])
    scores = []
    for g in heads:
        q = q_ref[:, g * PEER_DKEY:(g + 1) * PEER_DKEY]
        scores.append(lax.dot_general(keys_ref[0], q[:, :half], nt, preferred_element_type=F32))
        scores.append(lax.dot_general(keys_ref[1], q[:, half:], nt, preferred_element_type=F32))
    tops = _top_rows(scores, PEER_TOPK)
    cands = [jnp.concatenate([tops[2 * g][i] + tops[2 * g + 1][k] for i, k in _PEER_PAIRS], axis=0) for g in heads]
    thetas = [t[-1] for t in _top_rows(cands, PEER_TOPK)]
    for g in heads:
        s1, s2 = scores[2 * g], scores[2 * g + 1]
        m1, m2 = tops[2 * g][0], tops[2 * g + 1][0]
        z = jnp.sum(jnp.where(cands[g] >= thetas[g], jnp.exp(cands[g] - (m1 + m2)), 0.0), axis=0, keepdims=True)
        s1_ref[g] = s1
        s2_ref[g] = s2
        e1_ref[g] = jnp.exp(s1 - m1) / z
        e2_ref[g] = jnp.exp(s2 - m2)
        th_ref[g] = thetas[g]


def peer_route(q, keys):
    T = q.shape[0]
    tt = _pick(T, (512, 256, 128))
    H = PEER_HEADS
    hp = PEER_ROUTE_HEADS
    big = jax.ShapeDtypeStruct((H, PEER_NKEYS, T), F32)
    big_spec = pl.BlockSpec((hp, PEER_NKEYS, tt), lambda i, h: (h, 0, i))
    return pl.pallas_call(
        _peer_route_kernel,
        name="peer_route",
        grid=(T // tt, H // hp),
        in_specs=[pl.BlockSpec((tt, hp * PEER_DKEY), lambda i, h: (i, h)),
                  pl.BlockSpec((2, PEER_NKEYS, PEER_DKEY // 2), lambda i, h: (0, 0, 0))],
        out_specs=[big_spec, big_spec, big_spec, big_spec,
                   pl.BlockSpec((hp, 1, tt), lambda i, h: (h, 0, i))],
        out_shape=[big, big, big, big, jax.ShapeDtypeStruct((H, 1, T), F32)],
        compiler_params=_params("parallel", "parallel"),
    )(q, keys)


PEER_TE = 8 * PEER_NKEYS


def _peer_dense_kernel(h_ref, u_ref, v_ref, s1_ref, s2_ref, e1_ref, e2_ref, th_ref, x_ref, g_ref, o_ref,
                       xu_ref, w_ref, acc_ref):
    j = pl.program_id(1)
    tt = h_ref.shape[0]

    @pl.when(j == 0)
    def _():
        acc_ref[...] = jnp.zeros_like(acc_ref)

    xu_ref[...] = lax.dot_general(u_ref[...], h_ref[...], (((1,), (1,)), ((), ())), preferred_element_type=F32)

    for aa in range(PEER_TE // PEER_NKEYS):
        rows = slice(aa * PEER_NKEYS, (aa + 1) * PEER_NKEYS)
        for tc in range(tt // LANES):
            cols = slice(tc * LANES, (tc + 1) * LANES)
            gate = jnp.zeros((PEER_NKEYS, LANES), F32)
            for h in range(PEER_HEADS):
                s1 = s1_ref[h, aa:aa + 1, cols]
                e1 = e1_ref[h, aa:aa + 1, cols]
                gate = gate + jnp.where(s1 + s2_ref[h, :, cols] >= th_ref[h, :, cols], e1 * e2_ref[h, :, cols], 0.0)
            pre = xu_ref[rows, cols]
            act = 0.5 * pre * (1.0 + lax.erf(pre * (2.0 ** -0.5)))
            w_ref[rows, cols] = (gate * act).astype(w_ref.dtype)

    acc_ref[...] += lax.dot_general(w_ref[...], v_ref[...], (((0,), (0,)), ((), ())), preferred_element_type=F32)

    @pl.when(j == pl.num_programs(1) - 1)
    def _():
        o_ref[...] = x_ref[...] + g_ref[0] * acc_ref[...]


def peer_dense(h, u, v, route, x, mods, cidx_tt, tt, gate_slot):
    T, D = h.shape
    s1, s2, e1, e2, th = route
    H = PEER_HEADS
    na = PEER_TE // PEER_NKEYS
    once = pl.Buffered(1)
    tok = pl.BlockSpec((H, PEER_NKEYS, tt), lambda i, j: (0, 0, i), pipeline_mode=once)
    sub = pl.BlockSpec((H, na, tt), lambda i, j: (0, j, i))
    return pl.pallas_call(
        _peer_dense_kernel,
        name="peer_dense",
        grid=(T // tt, PEER_N // PEER_TE),
        in_specs=[pl.BlockSpec((tt, D), lambda i, j: (i, 0), pipeline_mode=once),
                  pl.BlockSpec((PEER_TE, D), lambda i, j: (j, 0)),
                  pl.BlockSpec((PEER_TE, D), lambda i, j: (j, 0)),
                  sub, tok, sub, tok,
                  pl.BlockSpec((H, 1, tt), lambda i, j: (0, 0, i), pipeline_mode=once),
                  pl.BlockSpec((tt, D), lambda i, j: (i, 0), pipeline_mode=once),
                  pl.BlockSpec((1, 1, D), lambda i, j: (cidx_tt(i) * 6 + gate_slot, 0, 0))],
        out_specs=pl.BlockSpec((tt, D), lambda i, j: (i, 0)),
        out_shape=jax.ShapeDtypeStruct((T, D), F32),
        scratch_shapes=[pltpu.VMEM((PEER_TE, tt), F32),
                        pltpu.VMEM((PEER_TE, tt), BF16),
                        pltpu.VMEM((tt, D), F32)],
        compiler_params=_params("parallel", "arbitrary"),
    )(h, u, v, s1, s2, e1, e2, th, x, mods)


HY_HID = 64
HY_BANDS = (HY_EMB - 1) // 2
HY_FREQ_CHUNK = 512


def _hy_filter_kernel(w1t_ref, w1c_ref, w1s_ref, b1_ref, w2_ref, b2_ref, w3_ref, absd_ref, sum_ref, dif_ref, *, L):
    tl = sum_ref.shape[0]
    W = HY_WIDTH
    pos_i = pl.program_id(0) * tl + lax.broadcasted_iota(jnp.int32, (tl, 1), 0)
    pos = pos_i.astype(F32)
    t01 = pos * (1.0 / (L - 1))
    w = (2.0 * math.pi) * pos / L
    band = lax.broadcasted_iota(jnp.int32, (1, HY_BANDS), 1).astype(F32)
    f = 1e-4 + band * ((HY_BANDS - 1 - 1e-4) / (HY_BANDS - 1))
    fw = f * w
    pre = (t01 * w1t_ref[...] + jnp.dot(jnp.cos(fw), w1c_ref[...], preferred_element_type=F32)
           - jnp.dot(jnp.sin(fw), w1s_ref[...], preferred_element_type=F32) + b1_ref[...])
    h = jnp.sin(pre)
    h = jnp.sin(jnp.dot(h, w2_ref[...], preferred_element_type=F32) + b2_ref[...])
    h = jnp.dot(h, w3_ref[...], preferred_element_type=F32)
    window = jnp.exp(-t01 * absd_ref[...])
    for o in range(HY_ORDER):
        hf = h[:, (2 * o) * W:(2 * o + 1) * W] * window
        hb = jnp.where(pos_i == 0, 0.0, h[:, (2 * o + 1) * W:(2 * o + 2) * W] * window)
        sum_ref[:, o * W:(o + 1) * W] = (hf + hb).astype(sum_ref.dtype)
        dif_ref[:, o * W:(o + 1) * W] = (hf - hb).astype(dif_ref.dtype)


def hyena_filter_terms(L, lp):
    W = HY_WIDTH
    tl = _pick(L, (256, 128))
    max_decay = math.log(HY_DECAY_TARGET) / HY_FAST_DECAY
    min_decay = math.log(HY_DECAY_TARGET) / HY_SLOW_DECAY
    absd = jnp.abs(jnp.linspace(min_decay, max_decay, W, dtype=F32)).reshape(1, W)
    w1 = lp["hy_w1"]
    full = lambda a: pl.BlockSpec(a.shape, lambda i: (0,) * a.ndim)
    args = (w1[0:1], w1[1:1 + HY_BANDS], w1[1 + HY_BANDS:], lp["hy_b1"].reshape(1, -1), lp["hy_w2"],
            lp["hy_b2"].reshape(1, -1), lp["hy_w3"], absd)
    out = jax.ShapeDtypeStruct((L, HY_ORDER * W), BF16)
    return pl.pallas_call(
        functools.partial(_hy_filter_kernel, L=L),
        grid=(L // tl,),
        in_specs=[full(a) for a in args],
        out_specs=[pl.BlockSpec((tl, HY_ORDER * W), lambda i: (i, 0))] * 2,
        out_shape=[out, out],
        compiler_params=_params("parallel"),
    )(*args)


def _dft_matrices(L):
    k = jnp.arange(L, dtype=jnp.int32)
    m = (k[:, None] * k[None, :]) % (2 * L)
    ang = m.astype(F32) * (math.pi / L)
    return jnp.stack([jnp.cos(ang), -jnp.sin(ang)]).astype(BF16)


def _hy_conv_kernel(v_ref, x1_ref, x2_ref, cv_ref, c1_ref, c2_ref, f_ref, ka_ref, kb_ref, ks_ref, bias_ref, o_ref,
                    y_ref, yb_ref, conv_ref):
    L = v_ref.shape[1]
    N = 2 * L
    tw = o_ref.shape[2]
    row = lax.broadcasted_iota(jnp.int32, (L, 1), 0)
    nyq = jnp.where(row % 2 == 0, 1.0, -1.0)
    fchunk = min(L, HY_FREQ_CHUNK)

    def short_conv(x_ref, c_ref):
        x = x_ref[0].astype(F32)
        prev = jnp.where(row >= 1, pltpu.roll(x, 1, 0), 0.0)
        nxt = jnp.where(row <= L - 2, pltpu.roll(x, L - 1, 0), 0.0)
        return prev * c_ref[0:1, :] + x * c_ref[1:2, :] + nxt * c_ref[2:3, :]

    y_ref[...] = short_conv(v_ref, cv_ref)
    for o, (x_ref, c_ref) in enumerate(((x1_ref, c1_ref), (x2_ref, c2_ref))):
        cols = slice(o * tw, (o + 1) * tw)
        y = y_ref[...]
        k_nyq = jnp.sum(nyq * ks_ref[:, cols].astype(F32), axis=0, keepdims=True)
        u_nyq = jnp.sum(nyq * y, axis=0, keepdims=True)
        conv_ref[...] = nyq * (u_nyq * k_nyq * (1.0 / N)) + bias_ref[o:o + 1, :] * y
        yb_ref[...] = y.astype(BF16)

        def freq_chunk(i, carry):
            f0 = pl.multiple_of(i * fchunk, fchunk)
            fr = pl.ds(f0, fchunk)
            yb = yb_ref[...]
            ka = ka_ref[fr, cols].astype(F32)
            kb = kb_ref[fr, cols].astype(F32)
            ua = jnp.dot(f_ref[0, fr, :], yb, preferred_element_type=F32)
            ub = jnp.dot(f_ref[1, fr, :], yb, preferred_element_type=F32)
            k_idx = f0 + lax.broadcasted_iota(jnp.int32, (fchunk, 1), 0)
            sc = jnp.where(k_idx == 0, 1.0 / N, 2.0 / N)
            ya = (sc * (ua * ka - ub * kb)).astype(BF16)
            yb2 = (sc * (ua * kb + ub * ka)).astype(BF16)
            conv_ref[...] += (jnp.dot(f_ref[0, :, fr], ya, preferred_element_type=F32)
                              + jnp.dot(f_ref[1, :, fr], yb2, preferred_element_type=F32))
            return carry

        lax.fori_loop(0, L // fchunk, freq_chunk, 0)
        y_ref[...] = short_conv(x_ref, c_ref) * conv_ref[...]
    o_ref[0] = y_ref[...].astype(o_ref.dtype)


def hyena(hy_raw, B, b_off, lp):
    _, L, _ = hy_raw.shape
    W = HY_WIDTH
    tw = 256
    nw = W // tw
    fsum, fdif = hyena_filter_terms(L, lp)
    F = lp["dft"][L]
    ka = matmul(F[0], fsum, BF16)
    kb = matmul(F[1], fdif, BF16)
    regroup = lambda a: a.reshape(L, HY_ORDER, nw, tw).transpose(0, 2, 1, 3).reshape(L, nw * HY_ORDER * tw)
    ka, kb, ks = regroup(ka), regroup(kb), regroup(fsum)
    cw = lp["hy_conv"]
    xspec = lambda g: pl.BlockSpec((1, L, tw), lambda j, b: (b + b_off, 0, g * nw + j))
    cspec = lambda g: pl.BlockSpec((HY_SHORT, tw), lambda j, b: (0, g * nw + j))
    kspec = pl.BlockSpec((L, HY_ORDER * tw), lambda j, b: (0, j), pipeline_mode=pl.Buffered(1))
    return pl.pallas_call(
        _hy_conv_kernel,
        name="hy_conv",
        grid=(nw, B),
        in_specs=[xspec(0), xspec(1), xspec(2), cspec(0), cspec(1), cspec(2),
                  pl.BlockSpec((2, L, L), lambda j, b: (0, 0, 0), pipeline_mode=pl.Buffered(1)),
                  kspec, kspec, kspec,
                  pl.BlockSpec((HY_ORDER, tw), lambda j, b: (0, j))],
        out_specs=pl.BlockSpec((1, L, tw), lambda j, b: (b, 0, j)),
        out_shape=jax.ShapeDtypeStruct((B, L, W), BF16),
        scratch_shapes=[pltpu.VMEM((L, tw), F32), pltpu.VMEM((L, tw), BF16), pltpu.VMEM((L, tw), F32)],
        compiler_params=_params("parallel", "parallel"),
    )(hy_raw, hy_raw, hy_raw, cw, cw, cw, F, ka, kb, ks, lp["hy_bias"])


def _dn_prep_kernel(x_ref, w_ref, o_ref, *, n_q_tiles):
    L = x_ref.shape[1]
    tc = x_ref.shape[2]
    x = x_ref[0].astype(F32)
    row = lax.broadcasted_iota(jnp.int32, (L, 1), 0)
    half = DN_CONV // 2
    acc = x * w_ref[half:half + 1, :]
    for d in range(-half, half + 1):
        if d == 0:
            continue
        shifted = pltpu.roll(x, (-d) % L, 0)
        valid = jnp.logical_and(row + d >= 0, row + d <= L - 1)
        acc = acc + jnp.where(valid, shifted, 0.0) * w_ref[half + d:half + d + 1, :]
    y = acc * jax.nn.sigmoid(acc)
    j = pl.program_id(1)
    for g in range(tc // DN_DK):
        yg = y[:, g * DN_DK:(g + 1) * DN_DK]
        inv = lax.rsqrt(jnp.sum(yg * yg, axis=-1, keepdims=True) + EPS)
        fac = jnp.where(j < n_q_tiles, inv * DN_DK ** -0.5, jnp.where(j < 2 * n_q_tiles, inv, 1.0))
        o_ref[0, :, g * DN_DK:(g + 1) * DN_DK] = yg * fac


def dn_prep(qkv_raw, B, b_off, conv_w):
    _, L, CH = qkv_raw.shape
    tc = 256
    return pl.pallas_call(
        functools.partial(_dn_prep_kernel, n_q_tiles=DN_HEADS * DN_DK // tc),
        grid=(B, CH // tc),
        in_specs=[pl.BlockSpec((1, L, tc), lambda b, j: (b + b_off, 0, j)),
                  pl.BlockSpec((DN_CONV, tc), lambda b, j: (0, j))],
        out_specs=pl.BlockSpec((1, L, tc), lambda b, j: (b, 0, j)),
        out_shape=jax.ShapeDtypeStruct((B, L, CH), F32),
        compiler_params=_params("parallel", "parallel"),
    )(qkv_raw, conv_w)


def _dn_gate_kernel(ab_ref, alog_ref, dt_ref, o_ref):
    n = 2 * DN_HEADS
    a = ab_ref[:, :n] + dt_ref[...]
    softplus = jnp.maximum(a, 0.0) + jnp.log1p(jnp.exp(-jnp.abs(a)))
    o_ref[:, :n] = -jnp.exp(alog_ref[...]) * softplus
    o_ref[:, n:] = jax.nn.sigmoid(ab_ref[:, n:])


def dn_gates(p_ab, a_log, dt_bias):
    T, n2 = p_ab.shape
    tm = _pick(T, (2048, 1024, 512, 256, 128))
    n = 2 * DN_HEADS
    return pl.pallas_call(
        _dn_gate_kernel,
        grid=(T // tm,),
        in_specs=[pl.BlockSpec((tm, n2), lambda i: (i, 0)),
                  pl.BlockSpec((1, n), lambda i: (0, 0)),
                  pl.BlockSpec((1, n), lambda i: (0, 0))],
        out_specs=pl.BlockSpec((tm, n2), lambda i: (i, 0)),
        out_shape=jax.ShapeDtypeStruct((T, n2), F32),
        compiler_params=_params("parallel"),
    )(p_ab, a_log.reshape(1, n), dt_bias.reshape(1, n))


def _split3(x):
    hi = x.astype(BF16)
    r1 = x - hi.astype(F32)
    mid = r1.astype(BF16)
    lo = (r1 - mid.astype(F32)).astype(BF16)
    return hi, mid, lo


def _dn_chunk_kernel(qf_ref, qb_ref, gcf_ref, gcb_ref, grf_ref, grb_ref, s0_ref, of_ref, ob_ref, sout_ref, s_ref):
    n = pl.program_id(1)
    C = DN_CHUNK
    H = DN_HEADS
    nt = (((1,), (1,)), ((), ()))

    @pl.when(n == 0)
    def _():
        s_ref[...] = s0_ref[0]

    r = lax.broadcasted_iota(jnp.int32, (C, C), 0)
    c = lax.broadcasted_iota(jnp.int32, (C, C), 1)
    eye = (r == c).astype(F32)

    def mm(x, y, dims=None):
        x, y = x.astype(BF16), y.astype(BF16)
        if dims is None:
            return jnp.dot(x, y, preferred_element_type=F32)
        return lax.dot_general(x, y, dims, preferred_element_type=F32)

    incl, strict, gcol, gc_col, gc_row, g_tot = [], [], [], [], [], []
    for d, (gc_ref, gr_ref) in enumerate(((gcf_ref, grf_ref), (gcb_ref, grb_ref))):
        lag = (r - c) if d == 0 else (c - r)
        incl.append(lag >= 0)
        strict.append(lag > 0)
        tri = jnp.where(incl[d], 1.0, 0.0).astype(BF16)
        g_c = gc_ref[0, 0]
        g_r = gr_ref[0, 0, 0]
        gcol.append(g_c)
        gc_col.append(sum(jnp.dot(tri, p, preferred_element_type=F32) for p in _split3(g_c)))
        row = sum(lax.dot_general(p, tri, nt, preferred_element_type=F32) for p in _split3(g_r))
        gc_row.append(row)
        g_tot.append(row[:, C - 1:C] if d == 0 else row[:, 0:1])

    chains = [(d, h) for d in range(2) for h in range(H)]
    idx = range(len(chains))
    qkv = (qf_ref, qb_ref)
    q = [qkv[d][0, :, h * DN_DK:(h + 1) * DN_DK] for d, h in chains]
    k = [qkv[d][0, :, (H + h) * DN_DK:(H + h + 1) * DN_DK] for d, h in chains]
    v = [qkv[d][0, :, 2 * H * DN_DK + h * DN_DV:2 * H * DN_DK + (h + 1) * DN_DV] for d, h in chains]
    gc = [gc_col[d][:, h:h + 1] for d, h in chains]
    beta = [gcol[d][:, H + h:H + h + 1] for d, h in chains]
    g_last = [g_tot[d][h:h + 1, :] for d, h in chains]
    eg = [jnp.exp(gc[i]) for i in idx]
    decay = [jnp.exp(jnp.where(incl[d], gc[i] - gc_row[d][h:h + 1, :], NEG_BIG)) for i, (d, h) in enumerate(chains)]
    kk = [mm(k[i], k[i], nt) for i in idx]
    qk = [mm(q[i], k[i], nt) for i in idx]
    a = [jnp.where(strict[d], beta[i] * kk[i] * decay[i], 0.0) for i, (d, h) in enumerate(chains)]
    qk = [jnp.where(incl[d], qk[i] * decay[i], 0.0) for i, (d, h) in enumerate(chains)]
    inv = None
    for l in range(C.bit_length() - 1):
        couple = jnp.logical_and((r >> (l + 1)) == (c >> (l + 1)), (r >> l) != (c >> l))
        a_l = [jnp.where(couple, a[i], 0.0) for i in idx]
        if inv is None:
            inv = [eye - a_l[i] for i in idx]
        else:
            t = [mm(inv[i], a_l[i]) for i in idx]
            inv = [inv[i] - mm(t[i], inv[i]) for i in idx]
    rhs = [jnp.concatenate([v[i] * beta[i], k[i] * (beta[i] * eg[i])], axis=1) for i in idx]
    sol = [mm(inv[i], rhs[i]) for i in idx]
    S = [s_ref[d, h] for d, h in chains]
    v_new = [sol[i][:, :DN_DV] - mm(sol[i][:, DN_DV:], S[i]) for i in idx]
    o = [mm(q[i] * eg[i], S[i]) + mm(qk[i], v_new[i]) for i in idx]
    upd = [mm(k[i] * jnp.exp(g_last[i] - gc[i]), v_new[i], (((0,), (0,)), ((), ()))) for i in idx]
    o_refs = (of_ref, ob_ref)
    for i, (d, h) in enumerate(chains):
        o_refs[d][0, :, h * DN_DV:(h + 1) * DN_DV] = o[i]
        s_ref[d, h] = S[i] * jnp.exp(g_last[i]) + upd[i]

    @pl.when(n == pl.num_programs(1) - 1)
    def _():
        sout_ref[0] = s_ref[...]


def dn_scan(qkv, gates, s0):
    B, L, CH = qkv.shape
    C, H = DN_CHUNK, DN_HEADS
    N = L // C
    g4 = gates.reshape(B, L, 2, 2, H)
    gcol = g4.transpose(0, 3, 1, 2, 4).reshape(B, 2, L, 2 * H)
    grow = gcol.reshape(B, 2, N, C, 2 * H).transpose(0, 1, 2, 4, 3)
    rev = lambda n: N - 1 - n
    out = jax.ShapeDtypeStruct((B, L, H * DN_DV), F32)
    return pl.pallas_call(
        _dn_chunk_kernel,
        name="dn_chunk",
        grid=(B, N),
        in_specs=[pl.BlockSpec((1, C, CH), lambda b, n: (b, n, 0)),
                  pl.BlockSpec((1, C, CH), lambda b, n: (b, rev(n), 0)),
                  pl.BlockSpec((1, 1, C, 2 * H), lambda b, n: (b, 0, n, 0)),
                  pl.BlockSpec((1, 1, C, 2 * H), lambda b, n: (b, 1, rev(n), 0)),
                  pl.BlockSpec((1, 1, 1, 2 * H, C), lambda b, n: (b, 0, n, 0, 0)),
                  pl.BlockSpec((1, 1, 1, 2 * H, C), lambda b, n: (b, 1, rev(n), 0, 0)),
                  pl.BlockSpec((1, 2, H, DN_DK, DN_DV), lambda b, n: (b, 0, 0, 0, 0))],
        out_specs=[pl.BlockSpec((1, C, H * DN_DV), lambda b, n: (b, n, 0)),
                   pl.BlockSpec((1, C, H * DN_DV), lambda b, n: (b, rev(n), 0)),
                   pl.BlockSpec((1, 2, H, DN_DK, DN_DV), lambda b, n: (b, 0, 0, 0, 0))],
        out_shape=[out, out, jax.ShapeDtypeStruct((B, 2, H, DN_DK, DN_DV), F32)],
        scratch_shapes=[pltpu.VMEM((2, H, DN_DK, DN_DV), F32)],
        compiler_params=_params("parallel", "arbitrary"),
    )(qkv, qkv, gcol, gcol, grow, grow, s0)


def _dn_out_kernel(of_ref, ob_ref, z_ref, g_ref, o_ref):
    for h in range(DN_HEADS):
        cols = slice(h * DN_DV, (h + 1) * DN_DV)
        o = of_ref[0, :, cols] + ob_ref[0, :, cols]
        z = z_ref[0, :, cols].astype(F32)
        o_ref[0, :, cols] = (_rms(o, g_ref[...]) * (z * jax.nn.sigmoid(z))).astype(o_ref.dtype)


def dn_output(o_f, o_b, z, b_off, g):
    B, L, W = o_f.shape
    tm = _pick(L, (512, 256, 128))
    spec = pl.BlockSpec((1, tm, W), lambda b, i: (b, i, 0))
    return pl.pallas_call(
        _dn_out_kernel,
        grid=(B, L // tm),
        in_specs=[spec, spec,
                  pl.BlockSpec((1, tm, W), lambda b, i: (b + b_off, i, 0)),
                  pl.BlockSpec((1, DN_DV), lambda b, i: (0, 0))],
        out_specs=spec,
        out_shape=jax.ShapeDtypeStruct((B, L, W), BF16),
        compiler_params=_params("parallel", "parallel"),
    )(o_f, o_b, z, g.reshape(1, DN_DV))


def gated_deltanet(qkv_raw, z, B, b_off, gates, lp, s0):
    qkv = dn_prep(qkv_raw, B, b_off, lp["dn_conv"])
    o_f, o_b, s_fin = dn_scan(qkv, gates, s0)
    return dn_output(o_f, o_b, z, b_off, lp["dn_out_norm"]), s_fin


def _axial_rope(x):
    L = x.shape[1]
    half = x.shape[-1] // 2
    t = jnp.arange(L)
    inv = ROPE_THETA ** (-jnp.arange(0, half, 2, dtype=F32) / half)
    out = []
    for pos, xa in ((t // GRID_W, x[..., :half]), (t % GRID_W, x[..., half:])):
        ang = pos.astype(F32)[:, None] * inv[None, :]
        cos, sin = jnp.cos(ang)[:, None, :], jnp.sin(ang)[:, None, :]
        x1, x2 = xa[..., : half // 2], xa[..., half // 2:]
        out += [x1 * cos - x2 * sin, x2 * cos + x1 * sin]
    return jnp.concatenate(out, axis=-1)


def _layer(x, lp, mods, dims, caches):
    Bc, Lc, Bl, Ll = dims
    Tc, Tl = Bc * Lc, Bl * Ll
    T = Tc + Tl
    tm = _pick(math.gcd(Tc, Ll), (1024, 512, 256, 128))

    def make_cidx(tile):
        nct, tpl = Tc // tile, Ll // tile
        return lambda i: jnp.where(i < nct, 0, 1 + (i - nct) // tpl)

    cidx = make_cidx(tm)
    h = modulate(x, lp["norm1_g"], mods, cidx, tm, 0, 1)

    w_in = lp["w_in"]
    cuts = np.cumsum((0,) + IN_SIZES)
    col = lambda a, b: w_in[:, cuts[a]:cuts[b]].astype(BF16)
    p_mla = matmul(h, col(0, 3))
    p_dn = matmul(h, col(3, 4), BF16)
    p_z = matmul(h, col(4, 5), BF16)
    p_ab = matmul(h, col(5, 7))
    p_hy = matmul(h, col(7, 8), BF16)
    p_na = matmul(h, col(8, 9), BF16)
    na_kv_ctx = matmul(h, w_in[:, cuts[8] + BRANCH_W:cuts[9]].astype(BF16), F32, rows=Tc)
    p_gate = matmul(h, col(9, 10), BF16)

    cq, ckv, krope = p_mla[:, :MLA_Q_RANK], p_mla[:, MLA_Q_RANK:MLA_Q_RANK + MLA_KV_RANK], p_mla[:, -MLA_ROPE:]

    cq_n = rmsnorm(cq, lp["mla_q_norm"], BF16)
    ckv_n = rmsnorm(ckv, lp["mla_kv_norm"], F32)
    def seqs(p, L, start, B):
        if start % L == 0 and T % L == 0:
            return p.reshape(T // L, L, -1), start // L
        return p[start:start + B * L].reshape(B, L, -1), 0

    ctx_of = lambda p: seqs(p, Lc, 0, Bc)
    lat_of = lambda p: seqs(p, Ll, Tc, Bl)

    w_qb = lp["mla_w_qb"].reshape(MLA_Q_RANK, MLA_HEADS, MLA_NOPE + MLA_ROPE)
    q_nope = matmul(cq_n, w_qb[:, :, :MLA_NOPE].reshape(MLA_Q_RANK, -1).astype(BF16), BF16)
    q_rope = matmul(cq_n, w_qb[:, :, MLA_NOPE:].reshape(MLA_Q_RANK, -1).astype(BF16))
    w_kvb = lp["mla_w_kvb"].astype(BF16)
    kv_all = matmul(ckv_n.astype(BF16), w_kvb, BF16)
    ckv_ctx, krope_ctx, nak_ctx, nav_ctx, s_f0, s_b0 = caches
    kv_cache = matmul(ckv_ctx.reshape(-1, MLA_KV_RANK).astype(BF16), w_kvb, BF16)
    Lp = ckv_ctx.shape[1]
    head_major = lambda t: t.transpose(0, 2, 1, 3).astype(BF16)

    (qn_c, qoff_c), (kv_c, koff_c) = ctx_of(q_nope), ctx_of(kv_all)
    o_a_ctx = mla_attention(qn_c, head_major(q_rope[:Tc].reshape(Bc, Lc, MLA_HEADS, MLA_ROPE)), kv_c,
                            krope[:Tc].reshape(Bc, Lc, MLA_ROPE).astype(BF16), Bc, qoff_c, koff_c)
    qn_l, qoff_l = lat_of(q_nope)
    qr_l = head_major(_axial_rope(q_rope[Tc:].reshape(Bl, Ll, MLA_HEADS, MLA_ROPE)))
    kr_l = _axial_rope(krope[Tc:].reshape(Bl, Ll, 1, MLA_ROPE)).reshape(Bl, Ll, MLA_ROPE)
    kr_l = jnp.concatenate([kr_l, krope_ctx], axis=1).astype(BF16)
    kv_l = jnp.concatenate([kv_all[Tc:].reshape(Bl, Ll, -1), kv_cache.reshape(Bl, Lp, -1)], axis=1)
    o_a_lat = mla_attention(qn_l, qr_l, kv_l, kr_l, Bl, qoff_l, 0)
    o_a = (o_a_ctx.reshape(Tc, BRANCH_W), o_a_lat.reshape(Tl, BRANCH_W))

    gates = dn_gates(p_ab, lp["dn_a_log"], lp["dn_dt_bias"])
    zero_state = jnp.zeros((Bc, 2, DN_HEADS, DN_DK, DN_DV), F32)
    (dn_c, off_c), (dn_l, off_l) = ctx_of(p_dn), lat_of(p_dn)
    o_b_ctx, s_ctx = gated_deltanet(dn_c, ctx_of(p_z)[0], Bc, off_c, gates[:Tc].reshape(Bc, Lc, -1), lp, zero_state)
    o_b_lat, _ = gated_deltanet(dn_l, lat_of(p_z)[0], Bl, off_l, gates[Tc:].reshape(Bl, Ll, -1), lp,
                                jnp.stack([s_f0, s_b0], axis=1))
    s_f, s_b = s_ctx[:, 0], s_ctx[:, 1]
    o_b = (o_b_ctx.reshape(Tc, BRANCH_W), o_b_lat.reshape(Tl, BRANCH_W))

    (hy_c, off_c), (hy_l, off_l) = ctx_of(p_hy), lat_of(p_hy)
    o_c = (hyena(hy_c, Bc, off_c, lp).reshape(Tc, BRANCH_W), hyena(hy_l, Bl, off_l, lp).reshape(Tl, BRANCH_W))

    (na_c, off_c), (na_l, off_l) = ctx_of(p_na), lat_of(p_na)
    o_d_ctx = attention(na_c, na_c, na_c, NA_HEADS, NA_HD, NA_HD, NA_SCALE, k_off=NA_HEADS, v_off=2 * NA_HEADS,
                        batch=Bc, b_off=off_c)
    o_d_lat = neighbourhood_attention(na_l, Bl, off_l, nak_ctx.reshape(Bl, Lp, -1).astype(BF16),
                                      nav_ctx.reshape(Bl, Lp, -1).astype(BF16), lp["na_bias"])
    o_d = (o_d_ctx.reshape(Tc, BRANCH_W), o_d_lat.reshape(Tl, BRANCH_W))

    outs = (o_a, o_b, o_c, o_d)
    merged = merge_branches([o[0] for o in outs], [o[1] for o in outs], p_gate, lp["w_branch"].astype(BF16), tm)
    x = matmul_gated_residual(merged, lp["w_out"].astype(BF16), x, mods, cidx, tm, 2)

    h2 = modulate(x, lp["norm2_g"], mods, cidx, tm, 3, 4)
    pq = matmul(h2, lp["peer_wq"].astype(BF16))
    route = peer_route(pq, lp["peer_keys"])
    tt = _pick(math.gcd(Tc, Ll), (512, 256, 128))
    x = peer_dense(h2, lp["peer_u"].astype(BF16), lp["peer_v"].astype(BF16), route, x, mods, make_cidx(tt), tt, 5)

    ctx_out = (ckv_n[:Tc].reshape(Bc, Lc, MLA_KV_RANK), krope[:Tc].reshape(Bc, Lc, MLA_ROPE),
               na_kv_ctx[:, :BRANCH_W].reshape(Bc, Lc, NA_HEADS, NA_HD),
               na_kv_ctx[:, BRANCH_W:].reshape(Bc, Lc, NA_HEADS, NA_HD), s_f, s_b)
    return x, ctx_out


def kernel(x_prompt, x_sample, cache_mla_ckv, cache_mla_krope, cache_na_k, cache_na_v, state_dn_fwd, state_dn_bwd, c, c_ctx, norm1_g, w_ada, b_ada, w_in, mla_q_norm, mla_w_qb, mla_kv_norm, mla_w_kvb, dn_conv, dn_a_log, dn_dt_bias, dn_out_norm, hy_conv, hy_w1, hy_b1, hy_w2, hy_b2, hy_w3, hy_bias, na_rpb, w_branch, w_out, norm2_g, peer_wq, peer_keys, peer_u, peer_v, final_g):
    Bc, Lc, D = x_prompt.shape
    Bl, Ll, _ = x_sample.shape
    depth = w_in.shape[0]
    Tc = Bc * Lc
    x = jnp.concatenate([x_prompt.reshape(Tc, D), x_sample.reshape(Bl * Ll, D)])
    n_cond = 1 + Bl
    cond_rows = -(-n_cond // SUBLANES) * SUBLANES
    cond = jnp.concatenate([c_ctx[None], c, jnp.zeros((cond_rows - n_cond, D), F32)])

    names = ("norm1_g", "w_in", "mla_q_norm", "mla_w_qb", "mla_kv_norm", "mla_w_kvb", "dn_conv", "dn_a_log",
             "dn_dt_bias", "dn_out_norm", "hy_conv", "hy_w1", "hy_b1", "hy_w2", "hy_b2", "hy_w3", "hy_bias",
             "na_rpb", "w_branch", "w_out", "norm2_g", "peer_wq", "peer_keys", "peer_u", "peer_v")
    vals = (norm1_g, w_in, mla_q_norm, mla_w_qb, mla_kv_norm, mla_w_kvb, dn_conv, dn_a_log, dn_dt_bias,
            dn_out_norm, hy_conv, hy_w1, hy_b1, hy_w2, hy_b2, hy_w3, hy_bias, na_rpb, w_branch, w_out, norm2_g,
            peer_wq, peer_keys, peer_u, peer_v)
    dft = {L: _dft_matrices(L) for L in {Lc, Ll}}
    na_bias = _na_bias_tables(na_rpb.reshape((depth * NA_HEADS,) + na_rpb.shape[2:]), Ll // GRID_W)
    na_bias = na_bias.reshape((depth, NA_HEADS) + na_bias.shape[1:])
    ctx = []
    for l in range(depth):
        lp = {n: v[l] for n, v in zip(names, vals)}
        lp["dft"] = dft
        lp["na_bias"] = na_bias[l]
        mods = ada_modulation(cond, w_ada[l], b_ada[l]).reshape(cond_rows * 6, 1, D)
        caches = (cache_mla_ckv[:, l], cache_mla_krope[:, l], cache_na_k[:, l], cache_na_v[:, l],
                  state_dn_fwd[:, l], state_dn_bwd[:, l])
        x, ctx_out = _layer(x, lp, mods, (Bc, Lc, Bl, Ll), caches)
        ctx.append(ctx_out)
    y_ctx = rmsnorm(x, final_g, rows=Tc)
    y_lat = rmsnorm(x, final_g, start=Tc, rows=Bl * Ll)
    stack = lambda k: jnp.stack([t[k] for t in ctx], axis=1)
    return (y_ctx.reshape(Bc, Lc, D), y_lat.reshape(Bl, Ll, D), stack(0), stack(1), stack(2), stack(3),
            stack(4), stack(5))
```

```python
import functools
import math

import numpy as np
import jax
import jax.numpy as jnp
from jax import lax
from jax.experimental import pallas as pl
from jax.experimental.pallas import tpu as pltpu

F32 = jnp.float32
BF16 = jnp.bfloat16

V7X_VMEM_BYTES = 64 * 1024 * 1024
VMEM_LIMIT = V7X_VMEM_BYTES - 8 * 1024 * 1024
LANES = 128
SUBLANES = 8

D_MODEL = 2048
GRID_W = 64
EPS = 1e-6
N_BRANCH = 4
BRANCH_W = D_MODEL // 2

MLA_HEADS = 8
MLA_NOPE = 128
MLA_ROPE = 64
MLA_V = BRANCH_W // MLA_HEADS
MLA_Q_RANK = D_MODEL // 4
MLA_KV_RANK = D_MODEL // 8
MLA_SCALE = (MLA_NOPE + MLA_ROPE) ** -0.5
ROPE_THETA = 10000.0

DN_HEADS = 8
DN_DK = 128
DN_DV = BRANCH_W // DN_HEADS
DN_CONV = 5
DN_CHUNK = 64
DN_CONV_CH = 2 * DN_HEADS * DN_DK + DN_HEADS * DN_DV

HY_WIDTH = BRANCH_W
HY_ORDER = 2
HY_SHORT = 3
HY_EMB = 33
HY_FAST_DECAY = 0.3
HY_SLOW_DECAY = 1.5
HY_DECAY_TARGET = 1e-2

NA_HEADS = 8
NA_HD = BRANCH_W // NA_HEADS
NA_KH = 8
NA_KW = 16
NA_SCALE = NA_HD ** -0.5
NA_QROWS = 4
NA_REGION = NA_KH + NA_QROWS
NEG_BIG = -1e30

PEER_HEADS = 8
PEER_NKEYS = 128
PEER_N = PEER_NKEYS * PEER_NKEYS
PEER_DKEY = 256
PEER_TOPK = 16

IN_SIZES = (MLA_Q_RANK, MLA_KV_RANK, MLA_ROPE, DN_CONV_CH, DN_HEADS * DN_DV, 2 * DN_HEADS, 2 * DN_HEADS,
            (HY_ORDER + 1) * HY_WIDTH, 3 * NA_HEADS * NA_HD, N_BRANCH * D_MODEL)


def _params(*sem):
    return pltpu.CompilerParams(dimension_semantics=sem, vmem_limit_bytes=VMEM_LIMIT)


def _pick(n, prefs):
    for p in prefs:
        if n % p == 0:
            return p
    return n


def _mm_kernel(x_ref, w_ref, o_ref):
    o_ref[...] = jnp.dot(x_ref[...], w_ref[...], preferred_element_type=F32).astype(o_ref.dtype)


def matmul(x, w, out_dtype=F32, tm=None, tn=None, rows=None):
    M, K = x.shape
    M = rows or M
    N = w.shape[1]
    tm = tm or _pick(M, (1024, 512, 256, 128))
    tn = tn or _pick(N, (512, 256, 128))
    return pl.pallas_call(
        _mm_kernel,
        grid=(M // tm, N // tn),
        in_specs=[pl.BlockSpec((tm, K), lambda i, j: (i, 0)),
                  pl.BlockSpec((K, tn), lambda i, j: (0, j))],
        out_specs=pl.BlockSpec((tm, tn), lambda i, j: (i, j)),
        out_shape=jax.ShapeDtypeStruct((M, N), out_dtype),
        compiler_params=_params("parallel", "parallel"),
    )(x, w)


def _ada_kernel(c_ref, w_ref, b_ref, o_ref):
    c = c_ref[...]
    o_ref[...] = jnp.dot(c * jax.nn.sigmoid(c), w_ref[...], preferred_element_type=F32) + b_ref[...]


def ada_modulation(cond, w_ada, b_ada):
    R, K = cond.shape
    N = w_ada.shape[1]
    tn = _pick(N, (1536, 1024, 512))
    return pl.pallas_call(
        _ada_kernel,
        grid=(N // tn,),
        in_specs=[pl.BlockSpec((R, K), lambda j: (0, 0)),
                  pl.BlockSpec((K, tn), lambda j: (0, j)),
                  pl.BlockSpec((1, tn), lambda j: (0, j))],
        out_specs=pl.BlockSpec((R, tn), lambda j: (0, j)),
        out_shape=jax.ShapeDtypeStruct((R, N), F32),
        compiler_params=_params("parallel"),
    )(cond, w_ada, b_ada.reshape(1, N))


def _rms(x, g):
    return x * lax.rsqrt(jnp.mean(x * x, axis=-1, keepdims=True) + EPS) * g


def _modulate_kernel(x_ref, g_ref, sh_ref, sc_ref, o_ref):
    y = _rms(x_ref[...], g_ref[...])
    o_ref[...] = (y * (1.0 + sc_ref[0]) + sh_ref[0]).astype(o_ref.dtype)


def modulate(x, g, mods, cidx, tm, shift_slot, scale_slot):
    T, D = x.shape
    return pl.pallas_call(
        _modulate_kernel,
        grid=(T // tm,),
        in_specs=[pl.BlockSpec((tm, D), lambda i: (i, 0)),
                  pl.BlockSpec((1, D), lambda i: (0, 0)),
                  pl.BlockSpec((1, 1, D), lambda i: (cidx(i) * 6 + shift_slot, 0, 0)),
                  pl.BlockSpec((1, 1, D), lambda i: (cidx(i) * 6 + scale_slot, 0, 0))],
        out_specs=pl.BlockSpec((tm, D), lambda i: (i, 0)),
        out_shape=jax.ShapeDtypeStruct((T, D), BF16),
        compiler_params=_params("parallel"),
    )(x, g.reshape(1, D), mods, mods)


def _rmsnorm_kernel(x_ref, g_ref, o_ref):
    o_ref[...] = _rms(x_ref[...].astype(F32), g_ref[...]).astype(o_ref.dtype)


def rmsnorm(x, g, out_dtype=F32, start=0, rows=None):
    T, D = x.shape
    T = rows or T
    tm = _pick(math.gcd(T, start) if start else T, (1024, 512, 256, 128))
    first = start // tm
    return pl.pallas_call(
        _rmsnorm_kernel,
        grid=(T // tm,),
        in_specs=[pl.BlockSpec((tm, D), lambda i: (i + first, 0)),
                  pl.BlockSpec((1, D), lambda i: (0, 0))],
        out_specs=pl.BlockSpec((tm, D), lambda i: (i, 0)),
        out_shape=jax.ShapeDtypeStruct((T, D), out_dtype),
        compiler_params=_params("parallel"),
    )(x, g.reshape(1, D))


def _merge_kernel(*refs, n_ctx_tiles):
    ctx_refs, lat_refs = refs[:N_BRANCH], refs[N_BRANCH:2 * N_BRANCH]
    gate_refs = refs[2 * N_BRANCH:3 * N_BRANCH]
    w_ref, o_ref = refs[3 * N_BRANCH:]

    def merge(branch_refs):
        acc = None
        for n, (br, gl) in enumerate(zip(branch_refs, gate_refs)):
            proj = jnp.dot(br[...], w_ref[n], preferred_element_type=F32)
            term = jax.nn.sigmoid(gl[...].astype(F32)) * proj
            acc = term if acc is None else acc + term
        o_ref[...] = acc.astype(o_ref.dtype)

    is_ctx = pl.program_id(0) < n_ctx_tiles
    pl.when(is_ctx)(functools.partial(merge, ctx_refs))
    pl.when(jnp.logical_not(is_ctx))(functools.partial(merge, lat_refs))


def merge_branches(branches_ctx, branches_lat, gate_logits, w_branch, tm):
    Tc, Tl = branches_ctx[0].shape[0], branches_lat[0].shape[0]
    nct, nlt = Tc // tm, Tl // tm
    tn = 512
    ctx_spec = pl.BlockSpec((tm, BRANCH_W), lambda i, j: (jnp.minimum(i, nct - 1), 0))
    lat_spec = pl.BlockSpec((tm, BRANCH_W), lambda i, j: (jnp.maximum(i - nct, 0), 0))
    nj = D_MODEL // tn
    gl_specs = [pl.BlockSpec((tm, tn), functools.partial(lambda i, j, n: (i, n * nj + j), n=n))
                for n in range(N_BRANCH)]
    return pl.pallas_call(
        functools.partial(_merge_kernel, n_ctx_tiles=nct),
        grid=(nct + nlt, D_MODEL // tn),
        in_specs=[ctx_spec] * N_BRANCH + [lat_spec] * N_BRANCH + gl_specs
                 + [pl.BlockSpec((N_BRANCH, BRANCH_W, tn), lambda i, j: (0, 0, j))],
        out_specs=pl.BlockSpec((tm, tn), lambda i, j: (i, j)),
        out_shape=jax.ShapeDtypeStruct((Tc + Tl, D_MODEL), BF16),
        compiler_params=_params("parallel", "parallel"),
    )(*branches_ctx, *branches_lat, gate_logits, gate_logits, gate_logits, gate_logits, w_branch)


def _mm_resid_kernel(m_ref, w_ref, x_ref, g_ref, o_ref):
    o_ref[...] = x_ref[...] + g_ref[0] * jnp.dot(m_ref[...], w_ref[...], preferred_element_type=F32)


def matmul_gated_residual(m, w, x, mods, cidx, tm, gate_slot):
    T, K = m.shape
    N = w.shape[1]
    tn = 512
    return pl.pallas_call(
        _mm_resid_kernel,
        grid=(T // tm, N // tn),
        in_specs=[pl.BlockSpec((tm, K), lambda i, j: (i, 0)),
                  pl.BlockSpec((K, tn), lambda i, j: (0, j)),
                  pl.BlockSpec((tm, tn), lambda i, j: (i, j)),
                  pl.BlockSpec((1, 1, tn), lambda i, j: (cidx(i) * 6 + gate_slot, 0, j))],
        out_specs=pl.BlockSpec((tm, tn), lambda i, j: (i, j)),
        out_shape=jax.ShapeDtypeStruct((T, N), F32),
        compiler_params=_params("parallel", "parallel"),
    )(m, w, x, mods)


def _attn_kernel(q_ref, k_ref, v_ref, o_ref, *, scale):
    s = lax.dot_general(q_ref[0], k_ref[0], (((1,), (1,)), ((), ())), preferred_element_type=F32) * scale
    p = jnp.exp(s - jnp.max(s, axis=-1, keepdims=True))
    l = jnp.sum(p, axis=-1, keepdims=True)
    o = jnp.dot(p.astype(BF16), v_ref[0], preferred_element_type=F32)
    o_ref[0] = (o / l).astype(o_ref.dtype)


def attention(q, k, v, heads, dqk, dv, scale, q_off=0, k_off=0, v_off=0, v_stride=1, k_stride=1, batch=None,
              b_off=0):
    B, Lq, _ = q.shape
    B = batch or B
    Lk = k.shape[1]
    tq = _pick(Lq, (256, 128))
    return pl.pallas_call(
        functools.partial(_attn_kernel, scale=scale),
        grid=(B, heads, Lq // tq),
        in_specs=[pl.BlockSpec((1, tq, dqk), lambda b, h, i: (b + b_off, i, q_off + h)),
                  pl.BlockSpec((1, Lk, dqk), lambda b, h, i: (b + b_off, 0, k_off + h * k_stride)),
                  pl.BlockSpec((1, Lk, dv), lambda b, h, i: (b + b_off, 0, v_off + h * v_stride))],
        out_specs=pl.BlockSpec((1, tq, dv), lambda b, h, i: (b, i, h)),
        out_shape=jax.ShapeDtypeStruct((B, Lq, heads * dv), BF16),
        compiler_params=_params("parallel", "parallel", "parallel"),
    )(q, k, v)


def _mla_attn_kernel(qn_ref, qr_ref, kn_ref, kr_ref, v_ref, *rest):
    q = jnp.concatenate([qn_ref[0], qr_ref[0, 0]], axis=1)
    k = jnp.concatenate([kn_ref[0], kr_ref[0]], axis=1)
    v = v_ref[0]
    if len(rest) == 4:
        kn2_ref, kr2_ref, v2_ref, o_ref = rest
        k = jnp.concatenate([k, jnp.concatenate([kn2_ref[0], kr2_ref[0]], axis=1)], axis=0)
        v = jnp.concatenate([v, v2_ref[0]], axis=0)
    else:
        (o_ref,) = rest
    s = lax.dot_general(q, k, (((1,), (1,)), ((), ())), preferred_element_type=F32) * MLA_SCALE
    p = jnp.exp(s - jnp.max(s, axis=-1, keepdims=True))
    l = jnp.sum(p, axis=-1, keepdims=True)
    o = jnp.dot(p.astype(BF16), v, preferred_element_type=F32)
    o_ref[0] = (o / l).astype(o_ref.dtype)


def mla_attention(q_nope, q_rope, kv, k_rope, B, q_off, kv_off, cache=None):
    _, Lq, _ = q_nope.shape
    H = MLA_HEADS
    tq = _pick(Lq, (256, 128))

    def kv_specs(L, off):
        return [pl.BlockSpec((1, L, MLA_NOPE), lambda b, h, i: (b + off, 0, 2 * h)),
                pl.BlockSpec((1, L, MLA_ROPE), lambda b, h, i: (b, 0, 0)),
                pl.BlockSpec((1, L, MLA_V), lambda b, h, i: (b + off, 0, 2 * h + 1))]

    specs, args = kv_specs(kv.shape[1], kv_off), [kv, k_rope, kv]
    if cache is not None:
        specs += kv_specs(cache[0].shape[1], 0)
        args += [cache[0], cache[1], cache[0]]
    return pl.pallas_call(
        _mla_attn_kernel,
        grid=(B, H, Lq // tq),
        in_specs=[pl.BlockSpec((1, tq, MLA_NOPE), lambda b, h, i: (b + q_off, i, h)),
                  pl.BlockSpec((1, 1, tq, MLA_ROPE), lambda b, h, i: (b, h, i, 0))] + specs,
        out_specs=pl.BlockSpec((1, tq, MLA_V), lambda b, h, i: (b, i, h)),
        out_shape=jax.ShapeDtypeStruct((B, Lq, H * MLA_V), BF16),
        compiler_params=_params("parallel", "parallel", "parallel"),
    )(q_nope, q_rope, *args)


def _na_kernel(q_ref, k_ref, v_ref, kc_ref, vc_ref, bias_ref, o_ref, *, rows):
    nblk = rows // NA_QROWS
    tq = NA_QROWS * GRID_W
    nkeys = NA_REGION * GRID_W
    nt = (((1,), (1,)), ((), ()))
    kc = kc_ref[0]
    vc = vc_ref[0]
    for rb in range(nblk):
        start = min(max(rb * NA_QROWS - NA_KH // 2, 0), rows - NA_REGION) * GRID_W
        kind = 0 if rb == 0 else (2 if rb == nblk - 1 else 1)
        q = q_ref[0, rb * tq:(rb + 1) * tq, :]
        kr = k_ref[0, start:start + nkeys, :]
        vr = v_ref[0, start:start + nkeys, :]
        s_win = lax.dot_general(q, kr, nt, preferred_element_type=F32) * NA_SCALE + bias_ref[0, kind]
        s_ctx = lax.dot_general(q, kc, nt, preferred_element_type=F32) * NA_SCALE
        m = jnp.maximum(jnp.max(s_win, axis=-1, keepdims=True), jnp.max(s_ctx, axis=-1, keepdims=True))
        p_win = jnp.exp(s_win - m)
        p_ctx = jnp.exp(s_ctx - m)
        l = jnp.sum(p_win, axis=-1, keepdims=True) + jnp.sum(p_ctx, axis=-1, keepdims=True)
        o = (jnp.dot(p_win.astype(BF16), vr, preferred_element_type=F32)
             + jnp.dot(p_ctx.astype(BF16), vc, preferred_element_type=F32))
        o_ref[0, rb * tq:(rb + 1) * tq, :] = (o / l).astype(o_ref.dtype)


def _na_bias_tables(rpb, rows):
    H = rpb.shape[0]
    n_dr, n_dc = 2 * NA_KH - 1, 2 * NA_KW - 1
    assert rows >= NA_REGION + NA_QROWS and rows % NA_QROWS == 0
    nblk = rows // NA_QROWS
    span = 2 * GRID_W
    lo = GRID_W - NA_KW
    v = jnp.pad(rpb.astype(F32), ((0, 0), (0, 0), (lo, span - lo - n_dc)), constant_values=NEG_BIG)
    skew = jnp.tile(v, (1, 1, GRID_W))[..., :GRID_W * (span - 1)].reshape(H, n_dr, GRID_W, span - 1)
    band = skew[..., GRID_W - 1:]
    c = np.arange(GRID_W)[:, None]
    kc = np.arange(GRID_W)[None, :]
    col0 = np.clip(c - NA_KW // 2, 0, GRID_W - NA_KW)
    col_ok = (kc >= col0) & (kc < col0 + NA_KW)
    band = jnp.where(col_ok, band, NEG_BIG)
    band = jnp.concatenate([band, jnp.full((H, 1, GRID_W, GRID_W), NEG_BIG, F32)], axis=1)
    tiles = []
    for rb in (0, 1, nblk - 1):
        start = int(np.clip(rb * NA_QROWS - NA_KH // 2, 0, rows - NA_REGION))
        for qr in range(NA_QROWS):
            r = rb * NA_QROWS + qr
            row0 = int(np.clip(r - NA_KH // 2, 0, rows - NA_KH))
            for j in range(NA_REGION):
                kr = start + j
                tiles.append(kr - r + NA_KH - 1 if row0 <= kr < row0 + NA_KH else n_dr)
    t = jnp.take(band, np.asarray(tiles, np.int32), axis=1).reshape(H, 3, NA_QROWS, NA_REGION, GRID_W, GRID_W)
    return t.transpose(0, 1, 2, 4, 3, 5).reshape(H, 3, NA_QROWS * GRID_W, NA_REGION * GRID_W)


def neighbourhood_attention(na_in, B, b_off, k_ctx, v_ctx, bias):
    _, L, _ = na_in.shape
    Lc = k_ctx.shape[1]
    rows = L // GRID_W
    tq = NA_QROWS * GRID_W
    nkeys = NA_REGION * GRID_W
    H = NA_HEADS
    seq = lambda col: pl.BlockSpec((1, L, NA_HD), lambda h, b: (b + b_off, 0, col * H + h))
    ctx = pl.BlockSpec((1, Lc, NA_HD), lambda h, b: (b, 0, h))
    return pl.pallas_call(
        functools.partial(_na_kernel, rows=rows),
        grid=(H, B),
        in_specs=[seq(0), seq(1), seq(2), ctx, ctx,
                  pl.BlockSpec((1, 3, tq, nkeys), lambda h, b: (h, 0, 0, 0))],
        out_specs=pl.BlockSpec((1, L, NA_HD), lambda h, b: (b, 0, h)),
        out_shape=jax.ShapeDtypeStruct((B, L, H * NA_HD), BF16),
        compiler_params=_params("parallel", "parallel"),
    )(na_in, na_in, na_in, k_ctx, v_ctx, bias)


def _top_rows(works, k):
    outs = [[] for _ in works]
    for _ in range(k):
        mx = [jnp.max(w, axis=0, keepdims=True) for w in works]
        for out, m in zip(outs, mx):
            out.append(m)
        works = [jnp.where(w == m, -jnp.inf, w) for w, m in zip(works, mx)]
    return outs


_PEER_PAIRS = [(i, k) for i in range(PEER_TOPK) for k in range(PEER_TOPK) if (i + 1) * (k + 1) <= PEER_TOPK]
PEER_ROUTE_HEADS = 2


def _peer_route_kernel(q_ref, keys_ref, s1_ref, s2_ref, e1_ref, e2_ref, th_ref):
    half = PEER_DKEY // 2
    nt = (((1,), (1,)), ((), ()))
    heads = range(s1_ref.shape[0])
    scores = []
    for g in heads:
        q = q_ref[:, g * PEER_DKEY:(g + 1) * PEER_DKEY]
        scores.append(lax.dot_general(keys_ref[0], q[:, :half], nt, preferred_element_type=F32))
        scores.append(lax.dot_general(keys_ref[1], q[:, half:], nt, preferred_element_type=F32))
    tops = _top_rows(scores, PEER_TOPK)
    cands = [jnp.concatenate([tops[2 * g][i] + tops[2 * g + 1][k] for i, k in _PEER_PAIRS], axis=0) for g in heads]
    thetas = [t[-1] for t in _top_rows(cands, PEER_TOPK)]
    for g in heads:
        s1, s2 = scores[2 * g], scores[2 * g + 1]
        m1, m2 = tops[2 * g][0], tops[2 * g + 1][0]
        z = jnp.sum(jnp.where(cands[g] >= thetas[g], jnp.exp(cands[g] - (m1 + m2)), 0.0), axis=0, keepdims=True)
        s1_ref[g] = s1
        s2_ref[g] = s2
        e1_ref[g] = jnp.exp(s1 - m1) / z
        e2_ref[g] = jnp.exp(s2 - m2)
        th_ref[g] = thetas[g]


def peer_route(q, keys):
    T = q.shape[0]
    tt = _pick(T, (512, 256, 128))
    H = PEER_HEADS
    hp = PEER_ROUTE_HEADS
    big = jax.ShapeDtypeStruct((H, PEER_NKEYS, T), F32)
    big_spec = pl.BlockSpec((hp, PEER_NKEYS, tt), lambda i, h: (h, 0, i))
    return pl.pallas_call(
        _peer_route_kernel,
        name="peer_route",
        grid=(T // tt, H // hp),
        in_specs=[pl.BlockSpec((tt, hp * PEER_DKEY), lambda i, h: (i, h)),
                  pl.BlockSpec((2, PEER_NKEYS, PEER_DKEY // 2), lambda i, h: (0, 0, 0))],
        out_specs=[big_spec, big_spec, big_spec, big_spec,
                   pl.BlockSpec((hp, 1, tt), lambda i, h: (h, 0, i))],
        out_shape=[big, big, big, big, jax.ShapeDtypeStruct((H, 1, T), F32)],
        compiler_params=_params("parallel", "parallel"),
    )(q, keys)


PEER_TE = 8 * PEER_NKEYS


def _peer_dense_kernel(h_ref, u_ref, v_ref, s1_ref, s2_ref, e1_ref, e2_ref, th_ref, x_ref, g_ref, o_ref,
                       xu_ref, w_ref, acc_ref):
    j = pl.program_id(1)
    tt = h_ref.shape[0]

    @pl.when(j == 0)
    def _():
        acc_ref[...] = jnp.zeros_like(acc_ref)

    xu_ref[...] = lax.dot_general(u_ref[...], h_ref[...], (((1,), (1,)), ((), ())), preferred_element_type=F32)

    for aa in range(PEER_TE // PEER_NKEYS):
        rows = slice(aa * PEER_NKEYS, (aa + 1) * PEER_NKEYS)
        for tc in range(tt // LANES):
            cols = slice(tc * LANES, (tc + 1) * LANES)
            gate = jnp.zeros((PEER_NKEYS, LANES), F32)
            for h in range(PEER_HEADS):
                s1 = s1_ref[h, aa:aa + 1, cols]
                e1 = e1_ref[h, aa:aa + 1, cols]
                gate = gate + jnp.where(s1 + s2_ref[h, :, cols] >= th_ref[h, :, cols], e1 * e2_ref[h, :, cols], 0.0)
            pre = xu_ref[rows, cols]
            act = 0.5 * pre * (1.0 + lax.erf(pre * (2.0 ** -0.5)))
            w_ref[rows, cols] = (gate * act).astype(w_ref.dtype)

    acc_ref[...] += lax.dot_general(w_ref[...], v_ref[...], (((0,), (0,)), ((), ())), preferred_element_type=F32)

    @pl.when(j == pl.num_programs(1) - 1)
    def _():
        o_ref[...] = x_ref[...] + g_ref[0] * acc_ref[...]


def peer_dense(h, u, v, route, x, mods, cidx_tt, tt, gate_slot):
    T, D = h.shape
    s1, s2, e1, e2, th = route
    H = PEER_HEADS
    na = PEER_TE // PEER_NKEYS
    once = pl.Buffered(1)
    tok = pl.BlockSpec((H, PEER_NKEYS, tt), lambda i, j: (0, 0, i), pipeline_mode=once)
    sub = pl.BlockSpec((H, na, tt), lambda i, j: (0, j, i))
    return pl.pallas_call(
        _peer_dense_kernel,
        name="peer_dense",
        grid=(T // tt, PEER_N // PEER_TE),
        in_specs=[pl.BlockSpec((tt, D), lambda i, j: (i, 0), pipeline_mode=once),
                  pl.BlockSpec((PEER_TE, D), lambda i, j: (j, 0)),
                  pl.BlockSpec((PEER_TE, D), lambda i, j: (j, 0)),
                  sub, tok, sub, tok,
                  pl.BlockSpec((H, 1, tt), lambda i, j: (0, 0, i), pipeline_mode=once),
                  pl.BlockSpec((tt, D), lambda i, j: (i, 0), pipeline_mode=once),
                  pl.BlockSpec((1, 1, D), lambda i, j: (cidx_tt(i) * 6 + gate_slot, 0, 0))],
        out_specs=pl.BlockSpec((tt, D), lambda i, j: (i, 0)),
        out_shape=jax.ShapeDtypeStruct((T, D), F32),
        scratch_shapes=[pltpu.VMEM((PEER_TE, tt), F32),
                        pltpu.VMEM((PEER_TE, tt), BF16),
                        pltpu.VMEM((tt, D), F32)],
        compiler_params=_params("parallel", "arbitrary"),
    )(h, u, v, s1, s2, e1, e2, th, x, mods)


HY_HID = 64
HY_BANDS = (HY_EMB - 1) // 2
HY_FREQ_CHUNK = 512


def _hy_filter_kernel(w1t_ref, w1c_ref, w1s_ref, b1_ref, w2_ref, b2_ref, w3_ref, absd_ref, sum_ref, dif_ref, *, L):
    tl = sum_ref.shape[0]
    W = HY_WIDTH
    pos_i = pl.program_id(0) * tl + lax.broadcasted_iota(jnp.int32, (tl, 1), 0)
    pos = pos_i.astype(F32)
    t01 = pos * (1.0 / (L - 1))
    w = (2.0 * math.pi) * pos / L
    band = lax.broadcasted_iota(jnp.int32, (1, HY_BANDS), 1).astype(F32)
    f = 1e-4 + band * ((HY_BANDS - 1 - 1e-4) / (HY_BANDS - 1))
    fw = f * w
    pre = (t01 * w1t_ref[...] + jnp.dot(jnp.cos(fw), w1c_ref[...], preferred_element_type=F32)
           - jnp.dot(jnp.sin(fw), w1s_ref[...], preferred_element_type=F32) + b1_ref[...])
    h = jnp.sin(pre)
    h = jnp.sin(jnp.dot(h, w2_ref[...], preferred_element_type=F32) + b2_ref[...])
    h = jnp.dot(h, w3_ref[...], preferred_element_type=F32)
    window = jnp.exp(-t01 * absd_ref[...])
    for o in range(HY_ORDER):
        hf = h[:, (2 * o) * W:(2 * o + 1) * W] * window
        hb = jnp.where(pos_i == 0, 0.0, h[:, (2 * o + 1) * W:(2 * o + 2) * W] * window)
        sum_ref[:, o * W:(o + 1) * W] = (hf + hb).astype(sum_ref.dtype)
        dif_ref[:, o * W:(o + 1) * W] = (hf - hb).astype(dif_ref.dtype)


def hyena_filter_terms(L, lp):
    W = HY_WIDTH
    tl = _pick(L, (256, 128))
    max_decay = math.log(HY_DECAY_TARGET) / HY_FAST_DECAY
    min_decay = math.log(HY_DECAY_TARGET) / HY_SLOW_DECAY
    absd = jnp.abs(jnp.linspace(min_decay, max_decay, W, dtype=F32)).reshape(1, W)
    w1 = lp["hy_w1"]
    full = lambda a: pl.BlockSpec(a.shape, lambda i: (0,) * a.ndim)
    args = (w1[0:1], w1[1:1 + HY_BANDS], w1[1 + HY_BANDS:], lp["hy_b1"].reshape(1, -1), lp["hy_w2"],
            lp["hy_b2"].reshape(1, -1), lp["hy_w3"], absd)
    out = jax.ShapeDtypeStruct((L, HY_ORDER * W), BF16)
    return pl.pallas_call(
        functools.partial(_hy_filter_kernel, L=L),
        grid=(L // tl,),
        in_specs=[full(a) for a in args],
        out_specs=[pl.BlockSpec((tl, HY_ORDER * W), lambda i: (i, 0))] * 2,
        out_shape=[out, out],
        compiler_params=_params("parallel"),
    )(*args)


def _dft_matrices(L):
    k = jnp.arange(L, dtype=jnp.int32)
    m = (k[:, None] * k[None, :]) % (2 * L)
    ang = m.astype(F32) * (math.pi / L)
    return jnp.stack([jnp.cos(ang), -jnp.sin(ang)]).astype(BF16)


def _hy_conv_kernel(v_ref, x1_ref, x2_ref, cv_ref, c1_ref, c2_ref, f_ref, ka_ref, kb_ref, ks_ref, bias_ref, o_ref,
                    y_ref, yb_ref, conv_ref):
    L = v_ref.shape[1]
    N = 2 * L
    tw = o_ref.shape[2]
    row = lax.broadcasted_iota(jnp.int32, (L, 1), 0)
    nyq = jnp.where(row % 2 == 0, 1.0, -1.0)
    fchunk = min(L, HY_FREQ_CHUNK)

    def short_conv(x_ref, c_ref):
        x = x_ref[0].astype(F32)
        prev = jnp.where(row >= 1, pltpu.roll(x, 1, 0), 0.0)
        nxt = jnp.where(row <= L - 2, pltpu.roll(x, L - 1, 0), 0.0)
        return prev * c_ref[0:1, :] + x * c_ref[1:2, :] + nxt * c_ref[2:3, :]

    y_ref[...] = short_conv(v_ref, cv_ref)
    for o, (x_ref, c_ref) in enumerate(((x1_ref, c1_ref), (x2_ref, c2_ref))):
        cols = slice(o * tw, (o + 1) * tw)
        y = y_ref[...]
        k_nyq = jnp.sum(nyq * ks_ref[:, cols].astype(F32), axis=0, keepdims=True)
        u_nyq = jnp.sum(nyq * y, axis=0, keepdims=True)
        conv_ref[...] = nyq * (u_nyq * k_nyq * (1.0 / N)) + bias_ref[o:o + 1, :] * y
        yb_ref[...] = y.astype(BF16)

        def freq_chunk(i, carry):
            f0 = pl.multiple_of(i * fchunk, fchunk)
            fr = pl.ds(f0, fchunk)
            yb = yb_ref[...]
            ka = ka_ref[fr, cols].astype(F32)
            kb = kb_ref[fr, cols].astype(F32)
            ua = jnp.dot(f_ref[0, fr, :], yb, preferred_element_type=F32)
            ub = jnp.dot(f_ref[1, fr, :], yb, preferred_element_type=F32)
            k_idx = f0 + lax.broadcasted_iota(jnp.int32, (fchunk, 1), 0)
            sc = jnp.where(k_idx == 0, 1.0 / N, 2.0 / N)
            ya = (sc * (ua * ka - ub * kb)).astype(BF16)
            yb2 = (sc * (ua * kb + ub * ka)).astype(BF16)
            conv_ref[...] += (jnp.dot(f_ref[0, :, fr], ya, preferred_element_type=F32)
                              + jnp.dot(f_ref[1, :, fr], yb2, preferred_element_type=F32))
            return carry

        lax.fori_loop(0, L // fchunk, freq_chunk, 0)
        y_ref[...] = short_conv(x_ref, c_ref) * conv_ref[...]
    o_ref[0] = y_ref[...].astype(o_ref.dtype)


def hyena(hy_raw, B, b_off, lp):
    _, L, _ = hy_raw.shape
    W = HY_WIDTH
    tw = 256
    nw = W // tw
    fsum, fdif = hyena_filter_terms(L, lp)
    F = lp["dft"][L]
    ka = matmul(F[0], fsum, BF16)
    kb = matmul(F[1], fdif, BF16)
    regroup = lambda a: a.reshape(L, HY_ORDER, nw, tw).transpose(0, 2, 1, 3).reshape(L, nw * HY_ORDER * tw)
    ka, kb, ks = regroup(ka), regroup(kb), regroup(fsum)
    cw = lp["hy_conv"]
    xspec = lambda g: pl.BlockSpec((1, L, tw), lambda j, b: (b + b_off, 0, g * nw + j))
    cspec = lambda g: pl.BlockSpec((HY_SHORT, tw), lambda j, b: (0, g * nw + j))
    kspec = pl.BlockSpec((L, HY_ORDER * tw), lambda j, b: (0, j), pipeline_mode=pl.Buffered(1))
    return pl.pallas_call(
        _hy_conv_kernel,
        name="hy_conv",
        grid=(nw, B),
        in_specs=[xspec(0), xspec(1), xspec(2), cspec(0), cspec(1), cspec(2),
                  pl.BlockSpec((2, L, L), lambda j, b: (0, 0, 0), pipeline_mode=pl.Buffered(1)),
                  kspec, kspec, kspec,
                  pl.BlockSpec((HY_ORDER, tw), lambda j, b: (0, j))],
        out_specs=pl.BlockSpec((1, L, tw), lambda j, b: (b, 0, j)),
        out_shape=jax.ShapeDtypeStruct((B, L, W), BF16),
        scratch_shapes=[pltpu.VMEM((L, tw), F32), pltpu.VMEM((L, tw), BF16), pltpu.VMEM((L, tw), F32)],
        compiler_params=_params("parallel", "parallel"),
    )(hy_raw, hy_raw, hy_raw, cw, cw, cw, F, ka, kb, ks, lp["hy_bias"])


def _dn_prep_kernel(x_ref, w_ref, o_ref, *, n_q_tiles):
    L = x_ref.shape[1]
    tc = x_ref.shape[2]
    x = x_ref[0].astype(F32)
    row = lax.broadcasted_iota(jnp.int32, (L, 1), 0)
    half = DN_CONV // 2
    acc = x * w_ref[half:half + 1, :]
    for d in range(-half, half + 1):
        if d == 0:
            continue
        shifted = pltpu.roll(x, (-d) % L, 0)
        valid = jnp.logical_and(row + d >= 0, row + d <= L - 1)
        acc = acc + jnp.where(valid, shifted, 0.0) * w_ref[half + d:half + d + 1, :]
    y = acc * jax.nn.sigmoid(acc)
    j = pl.program_id(1)
    for g in range(tc // DN_DK):
        yg = y[:, g * DN_DK:(g + 1) * DN_DK]
        inv = lax.rsqrt(jnp.sum(yg * yg, axis=-1, keepdims=True) + EPS)
        fac = jnp.where(j < n_q_tiles, inv * DN_DK ** -0.5, jnp.where(j < 2 * n_q_tiles, inv, 1.0))
        o_ref[0, :, g * DN_DK:(g + 1) * DN_DK] = yg * fac


def dn_prep(qkv_raw, B, b_off, conv_w):
    _, L, CH = qkv_raw.shape
    tc = 256
    return pl.pallas_call(
        functools.partial(_dn_prep_kernel, n_q_tiles=DN_HEADS * DN_DK // tc),
        grid=(B, CH // tc),
        in_specs=[pl.BlockSpec((1, L, tc), lambda b, j: (b + b_off, 0, j)),
                  pl.BlockSpec((DN_CONV, tc), lambda b, j: (0, j))],
        out_specs=pl.BlockSpec((1, L, tc), lambda b, j: (b, 0, j)),
        out_shape=jax.ShapeDtypeStruct((B, L, CH), F32),
        compiler_params=_params("parallel", "parallel"),
    )(qkv_raw, conv_w)


def _dn_gate_kernel(ab_ref, alog_ref, dt_ref, o_ref):
    n = 2 * DN_HEADS
    a = ab_ref[:, :n] + dt_ref[...]
    softplus = jnp.maximum(a, 0.0) + jnp.log1p(jnp.exp(-jnp.abs(a)))
    o_ref[:, :n] = -jnp.exp(alog_ref[...]) * softplus
    o_ref[:, n:] = jax.nn.sigmoid(ab_ref[:, n:])


def dn_gates(p_ab, a_log, dt_bias):
    T, n2 = p_ab.shape
    tm = _pick(T, (2048, 1024, 512, 256, 128))
    n = 2 * DN_HEADS
    return pl.pallas_call(
        _dn_gate_kernel,
        grid=(T // tm,),
        in_specs=[pl.BlockSpec((tm, n2), lambda i: (i, 0)),
                  pl.BlockSpec((1, n), lambda i: (0, 0)),
                  pl.BlockSpec((1, n), lambda i: (0, 0))],
        out_specs=pl.BlockSpec((tm, n2), lambda i: (i, 0)),
        out_shape=jax.ShapeDtypeStruct((T, n2), F32),
        compiler_params=_params("parallel"),
    )(p_ab, a_log.reshape(1, n), dt_bias.reshape(1, n))


def _split3(x):
    hi = x.astype(BF16)
    r1 = x - hi.astype(F32)
    mid = r1.astype(BF16)
    lo = (r1 - mid.astype(F32)).astype(BF16)
    return hi, mid, lo


def _dn_chunk_kernel(qf_ref, qb_ref, gcf_ref, gcb_ref, grf_ref, grb_ref, s0_ref, of_ref, ob_ref, sout_ref, s_ref):
    n = pl.program_id(1)
    C = DN_CHUNK
    H = DN_HEADS
    nt = (((1,), (1,)), ((), ()))

    @pl.when(n == 0)
    def _():
        s_ref[...] = s0_ref[0]

    r = lax.broadcasted_iota(jnp.int32, (C, C), 0)
    c = lax.broadcasted_iota(jnp.int32, (C, C), 1)
    eye = (r == c).astype(F32)

    def mm(x, y, dims=None):
        x, y = x.astype(BF16), y.astype(BF16)
        if dims is None:
            return jnp.dot(x, y, preferred_element_type=F32)
        return lax.dot_general(x, y, dims, preferred_element_type=F32)

    incl, strict, gcol, gc_col, gc_row, g_tot = [], [], [], [], [], []
    for d, (gc_ref, gr_ref) in enumerate(((gcf_ref, grf_ref), (gcb_ref, grb_ref))):
        lag = (r - c) if d == 0 else (c - r)
        incl.append(lag >= 0)
        strict.append(lag > 0)
        tri = jnp.where(incl[d], 1.0, 0.0).astype(BF16)
        g_c = gc_ref[0, 0]
        g_r = gr_ref[0, 0, 0]
        gcol.append(g_c)
        gc_col.append(sum(jnp.dot(tri, p, preferred_element_type=F32) for p in _split3(g_c)))
        row = sum(lax.dot_general(p, tri, nt, preferred_element_type=F32) for p in _split3(g_r))
        gc_row.append(row)
        g_tot.append(row[:, C - 1:C] if d == 0 else row[:, 0:1])

    chains = [(d, h) for d in range(2) for h in range(H)]
    idx = range(len(chains))
    qkv = (qf_ref, qb_ref)
    q = [qkv[d][0, :, h * DN_DK:(h + 1) * DN_DK] for d, h in chains]
    k = [qkv[d][0, :, (H + h) * DN_DK:(H + h + 1) * DN_DK] for d, h in chains]
    v = [qkv[d][0, :, 2 * H * DN_DK + h * DN_DV:2 * H * DN_DK + (h + 1) * DN_DV] for d, h in chains]
    gc = [gc_col[d][:, h:h + 1] for d, h in chains]
    beta = [gcol[d][:, H + h:H + h + 1] for d, h in chains]
    g_last = [g_tot[d][h:h + 1, :] for d, h in chains]
    eg = [jnp.exp(gc[i]) for i in idx]
    decay = [jnp.exp(jnp.where(incl[d], gc[i] - gc_row[d][h:h + 1, :], NEG_BIG)) for i, (d, h) in enumerate(chains)]
    kk = [mm(k[i], k[i], nt) for i in idx]
    qk = [mm(q[i], k[i], nt) for i in idx]
    a = [jnp.where(strict[d], beta[i] * kk[i] * decay[i], 0.0) for i, (d, h) in enumerate(chains)]
    qk = [jnp.where(incl[d], qk[i] * decay[i], 0.0) for i, (d, h) in enumerate(chains)]
    inv = None
    for l in range(C.bit_length() - 1):
        couple = jnp.logical_and((r >> (l + 1)) == (c >> (l + 1)), (r >> l) != (c >> l))
        a_l = [jnp.where(couple, a[i], 0.0) for i in idx]
        if inv is None:
            inv = [eye - a_l[i] for i in idx]
        else:
            t = [mm(inv[i], a_l[i]) for i in idx]
            inv = [inv[i] - mm(t[i], inv[i]) for i in idx]
    rhs = [jnp.concatenate([v[i] * beta[i], k[i] * (beta[i] * eg[i])], axis=1) for i in idx]
    sol = [mm(inv[i], rhs[i]) for i in idx]
    S = [s_ref[d, h] for d, h in chains]
    v_new = [sol[i][:, :DN_DV] - mm(sol[i][:, DN_DV:], S[i]) for i in idx]
    o = [mm(q[i] * eg[i], S[i]) + mm(qk[i], v_new[i]) for i in idx]
    upd = [mm(k[i] * jnp.exp(g_last[i] - gc[i]), v_new[i], (((0,), (0,)), ((), ()))) for i in idx]
    o_refs = (of_ref, ob_ref)
    for i, (d, h) in enumerate(chains):
        o_refs[d][0, :, h * DN_DV:(h + 1) * DN_DV] = o[i]
        s_ref[d, h] = S[i] * jnp.exp(g_last[i]) + upd[i]

    @pl.when(n == pl.num_programs(1) - 1)
    def _():
        sout_ref[0] = s_ref[...]


def dn_scan(qkv, gates, s0):
    B, L, CH = qkv.shape
    C, H = DN_CHUNK, DN_HEADS
    N = L // C
    g4 = gates.reshape(B, L, 2, 2, H)
    gcol = g4.transpose(0, 3, 1, 2, 4).reshape(B, 2, L, 2 * H)
    grow = gcol.reshape(B, 2, N, C, 2 * H).transpose(0, 1, 2, 4, 3)
    rev = lambda n: N - 1 - n
    out = jax.ShapeDtypeStruct((B, L, H * DN_DV), F32)
    return pl.pallas_call(
        _dn_chunk_kernel,
        name="dn_chunk",
        grid=(B, N),
        in_specs=[pl.BlockSpec((1, C, CH), lambda b, n: (b, n, 0)),
                  pl.BlockSpec((1, C, CH), lambda b, n: (b, rev(n), 0)),
                  pl.BlockSpec((1, 1, C, 2 * H), lambda b, n: (b, 0, n, 0)),
                  pl.BlockSpec((1, 1, C, 2 * H), lambda b, n: (b, 1, rev(n), 0)),
                  pl.BlockSpec((1, 1, 1, 2 * H, C), lambda b, n: (b, 0, n, 0, 0)),
                  pl.BlockSpec((1, 1, 1, 2 * H, C), lambda b, n: (b, 1, rev(n), 0, 0)),
                  pl.BlockSpec((1, 2, H, DN_DK, DN_DV), lambda b, n: (b, 0, 0, 0, 0))],
        out_specs=[pl.BlockSpec((1, C, H * DN_DV), lambda b, n: (b, n, 0)),
                   pl.BlockSpec((1, C, H * DN_DV), lambda b, n: (b, rev(n), 0)),
                   pl.BlockSpec((1, 2, H, DN_DK, DN_DV), lambda b, n: (b, 0, 0, 0, 0))],
        out_shape=[out, out, jax.ShapeDtypeStruct((B, 2, H, DN_DK, DN_DV), F32)],
        scratch_shapes=[pltpu.VMEM((2, H, DN_DK, DN_DV), F32)],
        compiler_params=_params("parallel", "arbitrary"),
    )(qkv, qkv, gcol, gcol, grow, grow, s0)


def _dn_out_kernel(of_ref, ob_ref, z_ref, g_ref, o_ref):
    for h in range(DN_HEADS):
        cols = slice(h * DN_DV, (h + 1) * DN_DV)
        o = of_ref[0, :, cols] + ob_ref[0, :, cols]
        z = z_ref[0, :, cols].astype(F32)
        o_ref[0, :, cols] = (_rms(o, g_ref[...]) * (z * jax.nn.sigmoid(z))).astype(o_ref.dtype)


def dn_output(o_f, o_b, z, b_off, g):
    B, L, W = o_f.shape
    tm = _pick(L, (512, 256, 128))
    spec = pl.BlockSpec((1, tm, W), lambda b, i: (b, i, 0))
    return pl.pallas_call(
        _dn_out_kernel,
        grid=(B, L // tm),
        in_specs=[spec, spec,
                  pl.BlockSpec((1, tm, W), lambda b, i: (b + b_off, i, 0)),
                  pl.BlockSpec((1, DN_DV), lambda b, i: (0, 0))],
        out_specs=spec,
        out_shape=jax.ShapeDtypeStruct((B, L, W), BF16),
        compiler_params=_params("parallel", "parallel"),
    )(o_f, o_b, z, g.reshape(1, DN_DV))


def gated_deltanet(qkv_raw, z, B, b_off, gates, lp, s0):
    qkv = dn_prep(qkv_raw, B, b_off, lp["dn_conv"])
    o_f, o_b, s_fin = dn_scan(qkv, gates, s0)
    return dn_output(o_f, o_b, z, b_off, lp["dn_out_norm"]), s_fin


def _axial_rope(x):
    L = x.shape[1]
    half = x.shape[-1] // 2
    t = jnp.arange(L)
    inv = ROPE_THETA ** (-jnp.arange(0, half, 2, dtype=F32) / half)
    out = []
    for pos, xa in ((t // GRID_W, x[..., :half]), (t % GRID_W, x[..., half:])):
        ang = pos.astype(F32)[:, None] * inv[None, :]
        cos, sin = jnp.cos(ang)[:, None, :], jnp.sin(ang)[:, None, :]
        x1, x2 = xa[..., : half // 2], xa[..., half // 2:]
        out += [x1 * cos - x2 * sin, x2 * cos + x1 * sin]
    return jnp.concatenate(out, axis=-1)


def _layer(x, lp, mods, dims, caches):
    Bc, Lc, Bl, Ll = dims
    Tc, Tl = Bc * Lc, Bl * Ll
    T = Tc + Tl
    tm = _pick(math.gcd(Tc, Ll), (1024, 512, 256, 128))

    def make_cidx(tile):
        nct, tpl = Tc // tile, Ll // tile
        return lambda i: jnp.where(i < nct, 0, 1 + (i - nct) // tpl)

    cidx = make_cidx(tm)
    h = modulate(x, lp["norm1_g"], mods, cidx, tm, 0, 1)

    w_in = lp["w_in"]
    cuts = np.cumsum((0,) + IN_SIZES)
    col = lambda a, b: w_in[:, cuts[a]:cuts[b]].astype(BF16)
    p_mla = matmul(h, col(0, 3))
    p_dn = matmul(h, col(3, 4), BF16)
    p_z = matmul(h, col(4, 5), BF16)
    p_ab = matmul(h, col(5, 7))
    p_hy = matmul(h, col(7, 8), BF16)
    p_na = matmul(h, col(8, 9), BF16)
    na_kv_ctx = matmul(h, w_in[:, cuts[8] + BRANCH_W:cuts[9]].astype(BF16), F32, rows=Tc)
    p_gate = matmul(h, col(9, 10), BF16)

    cq, ckv, krope = p_mla[:, :MLA_Q_RANK], p_mla[:, MLA_Q_RANK:MLA_Q_RANK + MLA_KV_RANK], p_mla[:, -MLA_ROPE:]

    cq_n = rmsnorm(cq, lp["mla_q_norm"], BF16)
    ckv_n = rmsnorm(ckv, lp["mla_kv_norm"], F32)
    def seqs(p, L, start, B):
        if start % L == 0 and T % L == 0:
            return p.reshape(T // L, L, -1), start // L
        return p[start:start + B * L].reshape(B, L, -1), 0

    ctx_of = lambda p: seqs(p, Lc, 0, Bc)
    lat_of = lambda p: seqs(p, Ll, Tc, Bl)

    w_qb = lp["mla_w_qb"].reshape(MLA_Q_RANK, MLA_HEADS, MLA_NOPE + MLA_ROPE)
    q_nope = matmul(cq_n, w_qb[:, :, :MLA_NOPE].reshape(MLA_Q_RANK, -1).astype(BF16), BF16)
    q_rope = matmul(cq_n, w_qb[:, :, MLA_NOPE:].reshape(MLA_Q_RANK, -1).astype(BF16))
    w_kvb = lp["mla_w_kvb"].astype(BF16)
    kv_all = matmul(ckv_n.astype(BF16), w_kvb, BF16)
    ckv_ctx, krope_ctx, nak_ctx, nav_ctx, s_f0, s_b0 = caches
    kv_cache = matmul(ckv_ctx.reshape(-1, MLA_KV_RANK).astype(BF16), w_kvb, BF16)
    Lp = ckv_ctx.shape[1]
    head_major = lambda t: t.transpose(0, 2, 1, 3).astype(BF16)

    (qn_c, qoff_c), (kv_c, koff_c) = ctx_of(q_nope), ctx_of(kv_all)
    o_a_ctx = mla_attention(qn_c, head_major(q_rope[:Tc].reshape(Bc, Lc, MLA_HEADS, MLA_ROPE)), kv_c,
                            krope[:Tc].reshape(Bc, Lc, MLA_ROPE).astype(BF16), Bc, qoff_c, koff_c)
    qn_l, qoff_l = lat_of(q_nope)
    qr_l = head_major(_axial_rope(q_rope[Tc:].reshape(Bl, Ll, MLA_HEADS, MLA_ROPE)))
    kr_l = _axial_rope(krope[Tc:].reshape(Bl, Ll, 1, MLA_ROPE)).reshape(Bl, Ll, MLA_ROPE).astype(BF16)
    kv_l, koff_l = lat_of(kv_all)
    o_a_lat = mla_attention(qn_l, qr_l, kv_l, kr_l, Bl, qoff_l, koff_l,
                            cache=(kv_cache.reshape(Bl, Lp, -1), krope_ctx.astype(BF16)))
    o_a = (o_a_ctx.reshape(Tc, BRANCH_W), o_a_lat.reshape(Tl, BRANCH_W))

    gates = dn_gates(p_ab, lp["dn_a_log"], lp["dn_dt_bias"])
    zero_state = jnp.zeros((Bc, 2, DN_HEADS, DN_DK, DN_DV), F32)
    (dn_c, off_c), (dn_l, off_l) = ctx_of(p_dn), lat_of(p_dn)
    o_b_ctx, s_ctx = gated_deltanet(dn_c, ctx_of(p_z)[0], Bc, off_c, gates[:Tc].reshape(Bc, Lc, -1), lp, zero_state)
    o_b_lat, _ = gated_deltanet(dn_l, lat_of(p_z)[0], Bl, off_l, gates[Tc:].reshape(Bl, Ll, -1), lp,
                                jnp.stack([s_f0, s_b0], axis=1))
    s_f, s_b = s_ctx[:, 0], s_ctx[:, 1]
    o_b = (o_b_ctx.reshape(Tc, BRANCH_W), o_b_lat.reshape(Tl, BRANCH_W))

    (hy_c, off_c), (hy_l, off_l) = ctx_of(p_hy), lat_of(p_hy)
    o_c = (hyena(hy_c, Bc, off_c, lp).reshape(Tc, BRANCH_W), hyena(hy_l, Bl, off_l, lp).reshape(Tl, BRANCH_W))

    (na_c, off_c), (na_l, off_l) = ctx_of(p_na), lat_of(p_na)
    o_d_ctx = attention(na_c, na_c, na_c, NA_HEADS, NA_HD, NA_HD, NA_SCALE, k_off=NA_HEADS, v_off=2 * NA_HEADS,
                        batch=Bc, b_off=off_c)
    o_d_lat = neighbourhood_attention(na_l, Bl, off_l, nak_ctx.reshape(Bl, Lp, -1).astype(BF16),
                                      nav_ctx.reshape(Bl, Lp, -1).astype(BF16), lp["na_bias"])
    o_d = (o_d_ctx.reshape(Tc, BRANCH_W), o_d_lat.reshape(Tl, BRANCH_W))

    outs = (o_a, o_b, o_c, o_d)
    merged = merge_branches([o[0] for o in outs], [o[1] for o in outs], p_gate, lp["w_branch"].astype(BF16), tm)
    x = matmul_gated_residual(merged, lp["w_out"].astype(BF16), x, mods, cidx, tm, 2)

    h2 = modulate(x, lp["norm2_g"], mods, cidx, tm, 3, 4)
    pq = matmul(h2, lp["peer_wq"].astype(BF16))
    route = peer_route(pq, lp["peer_keys"])
    tt = _pick(math.gcd(Tc, Ll), (512, 256, 128))
    x = peer_dense(h2, lp["peer_u"].astype(BF16), lp["peer_v"].astype(BF16), route, x, mods, make_cidx(tt), tt, 5)

    ctx_out = (ckv_n[:Tc].reshape(Bc, Lc, MLA_KV_RANK), krope[:Tc].reshape(Bc, Lc, MLA_ROPE),
               na_kv_ctx[:, :BRANCH_W].reshape(Bc, Lc, NA_HEADS, NA_HD),
               na_kv_ctx[:, BRANCH_W:].reshape(Bc, Lc, NA_HEADS, NA_HD), s_f, s_b)
    return x, ctx_out


def kernel(x_prompt, x_sample, cache_mla_ckv, cache_mla_krope, cache_na_k, cache_na_v, state_dn_fwd, state_dn_bwd, c, c_ctx, norm1_g, w_ada, b_ada, w_in, mla_q_norm, mla_w_qb, mla_kv_norm, mla_w_kvb, dn_conv, dn_a_log, dn_dt_bias, dn_out_norm, hy_conv, hy_w1, hy_b1, hy_w2, hy_b2, hy_w3, hy_bias, na_rpb, w_branch, w_out, norm2_g, peer_wq, peer_keys, peer_u, peer_v, final_g):
    Bc, Lc, D = x_prompt.shape
    Bl, Ll, _ = x_sample.shape
    depth = w_in.shape[0]
    Tc = Bc * Lc
    x = jnp.concatenate([x_prompt.reshape(Tc, D), x_sample.reshape(Bl * Ll, D)])
    n_cond = 1 + Bl
    cond_rows = -(-n_cond // SUBLANES) * SUBLANES
    cond = jnp.concatenate([c_ctx[None], c, jnp.zeros((cond_rows - n_cond, D), F32)])

    names = ("norm1_g", "w_in", "mla_q_norm", "mla_w_qb", "mla_kv_norm", "mla_w_kvb", "dn_conv", "dn_a_log",
             "dn_dt_bias", "dn_out_norm", "hy_conv", "hy_w1", "hy_b1", "hy_w2", "hy_b2", "hy_w3", "hy_bias",
             "na_rpb", "w_branch", "w_out", "norm2_g", "peer_wq", "peer_keys", "peer_u", "peer_v")
    vals = (norm1_g, w_in, mla_q_norm, mla_w_qb, mla_kv_norm, mla_w_kvb, dn_conv, dn_a_log, dn_dt_bias,
            dn_out_norm, hy_conv, hy_w1, hy_b1, hy_w2, hy_b2, hy_w3, hy_bias, na_rpb, w_branch, w_out, norm2_g,
            peer_wq, peer_keys, peer_u, peer_v)
    dft = {L: _dft_matrices(L) for L in {Lc, Ll}}
    na_bias = _na_bias_tables(na_rpb.reshape((depth * NA_HEADS,) + na_rpb.shape[2:]), Ll // GRID_W)
    na_bias = na_bias.reshape((depth, NA_HEADS) + na_bias.shape[1:])
    ctx = []
    for l in range(depth):
        lp = {n: v[l] for n, v in zip(names, vals)}
        lp["dft"] = dft
        lp["na_bias"] = na_bias[l]
        mods = ada_modulation(cond, w_ada[l], b_ada[l]).reshape(cond_rows * 6, 1, D)
        caches = (cache_mla_ckv[:, l], cache_mla_krope[:, l], cache_na_k[:, l], cache_na_v[:, l],
                  state_dn_fwd[:, l], state_dn_bwd[:, l])
        x, ctx_out = _layer(x, lp, mods, (Bc, Lc, Bl, Ll), caches)
        ctx.append(ctx_out)
    y_ctx = rmsnorm(x, final_g, rows=Tc)
    y_lat = rmsnorm(x, final_g, start=Tc, rows=Bl * Ll)
    stack = lambda k: jnp.stack([t[k] for t in ctx], axis=1)
    return (y_ctx.reshape(Bc, Lc, D), y_lat.reshape(Bl, Ll, D), stack(0), stack(1), stack(2), stack(3),
            stack(4), stack(5))
```
